```python
import math
import jax, jax.numpy as jnp
from jax import lax
import numpy as np

D_MODEL = 2048
BATCH = 4
SEQ = 2048
DEPTH = 2

HEAD_DIM = 128
Q_BLOCK = 128
ROPE_THETA = 500000.0
ROT_FRACTION_DIV = 4
NORM_EPS = 1e-6
N_BRANCH = 4
BRANCH_WIDTH = 512

N_HEADS_A = 4
DIFF_DIM = HEAD_DIM // 2
N_HEADS_B = 4
Q_LORA = 512
KV_LORA = 512
NOPE_DIM = 128
ROPE_DIM_B = 64
V_DIM_B = 128
DIL_PAIRS = ((128, 1), (512, 4), (2048, 16))
N_DIL_GROUPS = 3
HEADS_PER_GROUP = 4
N_HEADS_C = N_DIL_GROUPS * HEADS_PER_GROUP
N_HEADS_D = 4
D_FF = 5632
N_EXPERTS = 8
TOP_K = 2
D_FF_EXPERT = 5632
N_DENSE = (DEPTH + 1) // 2
N_MOE = DEPTH // 2

A_COLS = 3 * N_HEADS_A * HEAD_DIM
B_COLS = Q_LORA + KV_LORA + ROPE_DIM_B
C_COLS = 3 * N_HEADS_C * HEAD_DIM
D_COLS = 3 * N_HEADS_D * HEAD_DIM
GATE_COLS = N_BRANCH * D_MODEL
IN_COLS = A_COLS + B_COLS + C_COLS + D_COLS + GATE_COLS

kernel_name = 'hybrid_gated_mixer_block'


def rms_norm(x, gain):
    xf = x.astype(jnp.float32)
    y = xf * lax.rsqrt(jnp.mean(xf * xf, axis=-1, keepdims=True) + NORM_EPS)
    return (y * gain.astype(jnp.float32)).astype(x.dtype)


def rope_partial(x, rot_dim):
    seq = x.shape[1]
    pos = jnp.arange(seq, dtype=jnp.float32)
    inv_freq = ROPE_THETA ** (-jnp.arange(0, rot_dim, 2, dtype=jnp.float32) / rot_dim)
    ang = pos[:, None] * inv_freq[None, :]
    cos = jnp.cos(ang)[:, None, :]
    sin = jnp.sin(ang)[:, None, :]
    half = rot_dim // 2
    xf = x.astype(jnp.float32)
    x1 = xf[..., :half]
    x2 = xf[..., half:rot_dim]
    out = jnp.concatenate([x1 * cos - x2 * sin, x1 * sin + x2 * cos, xf[..., rot_dim:]], axis=-1)
    return out.astype(x.dtype)


def to_blocks(t):
    b, s = t.shape[:2]
    return jnp.moveaxis(t.reshape((b, s // Q_BLOCK, Q_BLOCK) + t.shape[2:]), 1, 0)


def from_blocks(t):
    nb, b, qb = t.shape[:3]
    return jnp.moveaxis(t, 0, 1).reshape((b, nb * qb) + t.shape[3:])


def block_starts(seq):
    return jnp.arange(seq // Q_BLOCK, dtype=jnp.int32) * Q_BLOCK


def causal_probs(qb, k, start, scale):
    s = jnp.einsum('bqhd,bkhd->bhqk', qb, k).astype(jnp.float32) * scale
    qpos = start + jnp.arange(Q_BLOCK, dtype=jnp.int32)
    kpos = jnp.arange(k.shape[1], dtype=jnp.int32)
    s = jnp.where(kpos[None, :] <= qpos[:, None], s, -jnp.inf)
    return jax.nn.softmax(s, axis=-1)


def diff_attention(qa, ka, va, lam_q1, lam_k1, lam_q2, lam_k2, gain, lam_init):
    b, s = qa.shape[:2]
    rot = DIFF_DIM // ROT_FRACTION_DIV
    q1 = rope_partial(qa[..., :DIFF_DIM], rot)
    q2 = rope_partial(qa[..., DIFF_DIM:], rot)
    k1 = rope_partial(ka[..., :DIFF_DIM], rot)
    k2 = rope_partial(ka[..., DIFF_DIM:], rot)
    lam = (jnp.exp(jnp.sum(lam_q1.astype(jnp.float32) * lam_k1.astype(jnp.float32)))
           - jnp.exp(jnp.sum(lam_q2.astype(jnp.float32) * lam_k2.astype(jnp.float32)))
           + lam_init)
    scale = 1.0 / math.sqrt(DIFF_DIM)

    def blk(args):
        q1b, q2b, start = args
        p = causal_probs(q1b, k1, start, scale) - lam * causal_probs(q2b, k2, start, scale)
        return jnp.einsum('bhqk,bkhd->bqhd', p.astype(va.dtype), va)

    out = from_blocks(lax.map(blk, (to_blocks(q1), to_blocks(q2), block_starts(s))))
    out = rms_norm(out, gain) * (1.0 - lam_init)
    return out.reshape(b, s, N_HEADS_A * HEAD_DIM)


def mla_attention(c_q, c_kv, k_rope_in, dq_norm, dkv_norm, w_uq, w_ukv):
    b, s = c_q.shape[:2]
    q = (rms_norm(c_q, dq_norm) @ w_uq).reshape(b, s, N_HEADS_B, NOPE_DIM + ROPE_DIM_B)
    kv = (rms_norm(c_kv, dkv_norm) @ w_ukv).reshape(b, s, N_HEADS_B, NOPE_DIM + V_DIM_B)
    q = jnp.concatenate([q[..., :NOPE_DIM], rope_partial(q[..., NOPE_DIM:], ROPE_DIM_B)], axis=-1)
    k_rope = rope_partial(k_rope_in[:, :, None, :], ROPE_DIM_B)
    k = jnp.concatenate([kv[..., :NOPE_DIM],
                         jnp.broadcast_to(k_rope, (b, s, N_HEADS_B, ROPE_DIM_B))], axis=-1)
    v = kv[..., NOPE_DIM:]
    scale = 1.0 / math.sqrt(NOPE_DIM + ROPE_DIM_B)

    def blk(args):
        qb, start = args
        p = causal_probs(qb, k, start, scale)
        return jnp.einsum('bhqk,bkhd->bqhd', p.astype(v.dtype), v)

    out = from_blocks(lax.map(blk, (to_blocks(q), block_starts(s))))
    return out.reshape(b, s, N_HEADS_B * V_DIM_B)


def dilated_attention(qc, kc, vc):
    b, s = qc.shape[:2]
    rot = HEAD_DIM // ROT_FRACTION_DIV
    grp = (b, s, N_DIL_GROUPS, HEADS_PER_GROUP, HEAD_DIM)
    q = rope_partial(qc, rot).reshape(grp)
    k = rope_partial(kc, rot).reshape(grp)
    v = vc.reshape(grp)
    k_groups = [k[:, :, g] for g in range(N_DIL_GROUPS)]
    v_groups = [v[:, :, g] for g in range(N_DIL_GROUPS)]
    scale = 1.0 / math.sqrt(HEAD_DIM)

    def blk(args):
        qb, start = args
        qpos = start + jnp.arange(Q_BLOCK, dtype=jnp.int32)
        outs, lses = [], []
        for g, (window, dilation) in enumerate(DIL_PAIRS):
            n_keys = window // dilation + 1
            kidx = qpos[:, None] - dilation * jnp.arange(n_keys, dtype=jnp.int32)[None, :]
            valid = kidx >= 0
            kidx = jnp.maximum(kidx, 0)
            kg = k_groups[g][:, kidx]
            vg = v_groups[g][:, kidx]
            sc = jnp.einsum('bqhd,bqjhd->bhqj', qb[:, :, g], kg).astype(jnp.float32) * scale
            sc = jnp.where(valid[None, None], sc, -jnp.inf)
            m = jnp.max(sc, axis=-1, keepdims=True)
            e = jnp.exp(sc - m)
            den = jnp.sum(e, axis=-1, keepdims=True)
            outs.append(jnp.einsum('bhqj,bqjhd->bqhd', (e / den).astype(vg.dtype), vg))
            lses.append((m + jnp.log(den))[..., 0])
        w = jax.nn.softmax(jnp.stack(lses, axis=0), axis=0)
        w = jnp.transpose(w, (0, 1, 3, 2))[..., None]
        return jnp.sum(w.astype(qb.dtype) * jnp.stack(outs, axis=0), axis=0)

    out = from_blocks(lax.map(blk, (to_blocks(q), block_starts(s))))
    return out.reshape(b, s, HEADS_PER_GROUP * HEAD_DIM)


def stick_breaking_attention(qd, kd, vd):
    b, s = qd.shape[:2]
    scale = 1.0 / math.sqrt(HEAD_DIM)
    kpos = jnp.arange(s, dtype=jnp.int32)

    def blk(args):
        qb, start = args
        z = jnp.einsum('bqhd,bkhd->bhqk', qb, kd).astype(jnp.float32) * scale
        qpos = start + jnp.arange(Q_BLOCK, dtype=jnp.int32)
        strict = kpos[None, :] < qpos[:, None]
        log_not = jnp.where(strict, jax.nn.log_sigmoid(-z), 0.0)
        csum = jnp.cumsum(log_not, axis=-1)
        later = csum[..., -1:] - csum
        a = jnp.where(strict, jnp.exp(jax.nn.log_sigmoid(z) + later), 0.0)
        return jnp.einsum('bhqk,bkhd->bqhd', a.astype(vd.dtype), vd)

    out = from_blocks(lax.map(blk, (to_blocks(qd), block_starts(s))))
    return out.reshape(b, s, N_HEADS_D * HEAD_DIM)


def swiglu(h, w_in, w_out):
    gu = h @ w_in
    g, u = jnp.split(gu, 2, axis=-1)
    return (jax.nn.silu(g) * u) @ w_out


def moe_swiglu(h, w_router, w_moe_in, w_moe_out):
    b, s, d = h.shape
    hf = h.reshape(b * s, d)
    logits = (hf @ w_router).astype(jnp.float32)
    top_vals, top_idx = lax.top_k(logits, TOP_K)
    gates = jax.nn.softmax(top_vals, axis=-1)
    combine = jnp.sum(jax.nn.one_hot(top_idx, N_EXPERTS, dtype=jnp.float32) * gates[..., None], axis=1)
    combine = combine.astype(h.dtype)
    y = jnp.zeros_like(hf)
    for e in range(N_EXPERTS):
        y = y + combine[:, e:e + 1] * swiglu(hf, w_moe_in[e], w_moe_out[e])
    return y.reshape(b, s, d)


def setup_inputs(seed: int = 0) -> dict:
    key = jax.random.key(seed)
    ks = jax.random.split(key, 24)

    def nrm(k, shape, scale):
        return jax.random.normal(k, shape, jnp.float32) * scale

    def gain(k, shape):
        return 1.0 + 0.01 * jax.random.normal(k, shape, jnp.float32)

    return {
        'x': nrm(ks[0], (BATCH, SEQ, D_MODEL), 1.0),
        'norm_mix': gain(ks[1], (DEPTH, D_MODEL)),
        'w_in': nrm(ks[2], (DEPTH, D_MODEL, IN_COLS), D_MODEL ** -0.5),
        'dq_norm': gain(ks[3], (DEPTH, Q_LORA)),
        'dkv_norm': gain(ks[4], (DEPTH, KV_LORA)),
        'w_uq': nrm(ks[5], (DEPTH, Q_LORA, N_HEADS_B * (NOPE_DIM + ROPE_DIM_B)), Q_LORA ** -0.5),
        'w_ukv': nrm(ks[6], (DEPTH, KV_LORA, N_HEADS_B * (NOPE_DIM + V_DIM_B)), KV_LORA ** -0.5),
        'lam_q1': nrm(ks[7], (DEPTH, DIFF_DIM), 0.1),
        'lam_k1': nrm(ks[8], (DEPTH, DIFF_DIM), 0.1),
        'lam_q2': nrm(ks[9], (DEPTH, DIFF_DIM), 0.1),
        'lam_k2': nrm(ks[10], (DEPTH, DIFF_DIM), 0.1),
        'diff_norm': gain(ks[11], (DEPTH, HEAD_DIM)),
        'w_branch': nrm(ks[12], (DEPTH, N_BRANCH, BRANCH_WIDTH, D_MODEL), BRANCH_WIDTH ** -0.5),
        'w_out': nrm(ks[13], (DEPTH, D_MODEL, D_MODEL), D_MODEL ** -0.5),
        'norm_ffn': gain(ks[14], (DEPTH, D_MODEL)),
        'w_dense_in': nrm(ks[15], (N_DENSE, D_MODEL, 2 * D_FF), D_MODEL ** -0.5),
        'w_dense_out': nrm(ks[16], (N_DENSE, D_FF, D_MODEL), D_FF ** -0.5),
        'w_router': nrm(ks[17], (N_MOE, D_MODEL, N_EXPERTS), D_MODEL ** -0.5),
        'w_moe_in': nrm(ks[18], (N_MOE, N_EXPERTS, D_MODEL, 2 * D_FF_EXPERT), D_MODEL ** -0.5),
        'w_moe_out': nrm(ks[19], (N_MOE, N_EXPERTS, D_FF_EXPERT, D_MODEL), D_FF_EXPERT ** -0.5),
        'norm_final': gain(ks[20], (D_MODEL,)),
    }


def reference(x, norm_mix, w_in, dq_norm, dkv_norm, w_uq, w_ukv, lam_q1, lam_k1, lam_q2, lam_k2,
              diff_norm, w_branch, w_out, norm_ffn, w_dense_in, w_dense_out, w_router,
              w_moe_in, w_moe_out, norm_final):
    b, s, _ = x.shape
    split_points = [A_COLS, A_COLS + B_COLS, A_COLS + B_COLS + C_COLS,
                    A_COLS + B_COLS + C_COLS + D_COLS]
    for layer in range(DEPTH):
        h = rms_norm(x, norm_mix[layer])
        proj = h @ w_in[layer]
        pa, pb, pc, pd, pg = jnp.split(proj, split_points, axis=-1)

        qa, ka, va = [t.reshape(b, s, N_HEADS_A, HEAD_DIM) for t in jnp.split(pa, 3, axis=-1)]
        lam_init = 0.8 - 0.6 * math.exp(-0.3 * layer)
        y_a = diff_attention(qa, ka, va, lam_q1[layer], lam_k1[layer], lam_q2[layer],
                             lam_k2[layer], diff_norm[layer], lam_init)

        c_q, c_kv, k_rope = jnp.split(pb, [Q_LORA, Q_LORA + KV_LORA], axis=-1)
        y_b = mla_attention(c_q, c_kv, k_rope, dq_norm[layer], dkv_norm[layer],
                            w_uq[layer], w_ukv[layer])

        qc, kc, vc = [t.reshape(b, s, N_HEADS_C, HEAD_DIM) for t in jnp.split(pc, 3, axis=-1)]
        y_c = dilated_attention(qc, kc, vc)

        qd, kd, vd = [t.reshape(b, s, N_HEADS_D, HEAD_DIM) for t in jnp.split(pd, 3, axis=-1)]
        y_d = stick_breaking_attention(qd, kd, vd)

        gates = jax.nn.sigmoid(pg.astype(jnp.float32)).astype(x.dtype).reshape(b, s, N_BRANCH, D_MODEL)
        merged = jnp.zeros_like(x)
        for i, y_i in enumerate((y_a, y_b, y_c, y_d)):
            merged = merged + gates[:, :, i] * (y_i @ w_branch[layer, i])
        x = x + merged @ w_out[layer]

        h = rms_norm(x, norm_ffn[layer])
        if layer % 2 == 0:
            x = x + swiglu(h, w_dense_in[layer // 2], w_dense_out[layer // 2])
        else:
            x = x + moe_swiglu(h, w_router[layer // 2], w_moe_in[layer // 2], w_moe_out[layer // 2])
    return rms_norm(x, norm_final)
```

```python
import functools
import math

import numpy as np
import jax
import jax.numpy as jnp
from jax import lax
from jax.experimental import pallas as pl
from jax.experimental.pallas import tpu as pltpu

F32 = jnp.float32
BF16 = jnp.bfloat16

HEAD_DIM = 128
ROPE_THETA = 500000.0
NORM_EPS = 1e-6
N_BRANCH = 4
BRANCH_WIDTH = 512
N_HEADS = 4
DIFF_DIM = 64
Q_LORA = 512
KV_LORA = 512
NOPE_DIM = 128
ROPE_DIM_B = 64
MLA_QK = 256
DIL_PAIRS = ((128, 1), (512, 4), (2048, 16))
N_DIL_GROUPS = 3
N_EXPERTS = 8
TOP_K = 2
LANES = 128

A_OFF = 0
B_OFF = 1536
C_OFF = 3072
D_OFF = 7680
QKV_COLS = 9216

VMEM_LIMIT = 56 * 1024 * 1024


def _cparams(sem, vmem=VMEM_LIMIT):
    return pltpu.CompilerParams(dimension_semantics=sem, vmem_limit_bytes=vmem)


def _dot(a, b):
    return jnp.dot(a, b, preferred_element_type=F32)


def _dot_nt(a, b):
    return lax.dot_general(a, b, (((1,), (1,)), ((), ())), preferred_element_type=F32)


def _rms(x, gain):
    return x * lax.rsqrt(jnp.mean(x * x, axis=-1, keepdims=True) + NORM_EPS) * gain


def _sigmoid(x):
    return 1.0 / (1.0 + jnp.exp(-x))


def _rope_tables(seq, segments):
    pos = jnp.arange(seq, dtype=F32)
    c = jnp.ones((seq, LANES), F32)
    s = jnp.zeros((seq, LANES), F32)
    r = np.zeros((LANES, LANES), np.float32)
    for start, rot in segments:
        half = rot // 2
        inv_freq = ROPE_THETA ** (-jnp.arange(0, rot, 2, dtype=F32) / rot)
        ang = pos[:, None] * inv_freq[None, :]
        cos, sin = jnp.cos(ang), jnp.sin(ang)
        c = c.at[:, start:start + half].set(cos).at[:, start + half:start + rot].set(cos)
        s = s.at[:, start:start + half].set(-sin).at[:, start + half:start + rot].set(sin)
        for i in range(half):
            r[start + half + i, start + i] = 1.0
            r[start + i, start + half + i] = 1.0
    return c, s, jnp.asarray(r, BF16)


def _rope(x_bf16, rot, c, s):
    return x_bf16.astype(F32) * c + _dot(x_bf16, rot) * s


def _inproj_kernel(x_ref, g_ref, w_ref, o_ref, h_ref, *, sigmoid):
    @pl.when(pl.program_id(1) == 0)
    def _():
        h_ref[...] = _rms(x_ref[...], g_ref[...]).astype(BF16)

    acc = _dot(h_ref[...], w_ref[...])
    if sigmoid:
        acc = _sigmoid(acc)
    o_ref[...] = acc.astype(o_ref.dtype)


def _inproj(x, gain, w, *, sigmoid, tm=1024, tn=512):
    t, d = x.shape
    n = w.shape[1]
    return pl.pallas_call(
        functools.partial(_inproj_kernel, sigmoid=sigmoid),
        out_shape=jax.ShapeDtypeStruct((t, n), BF16),
        grid=(t // tm, n // tn),
        in_specs=[pl.BlockSpec((tm, d), lambda i, j: (i, 0)),
                  pl.BlockSpec((1, d), lambda i, j: (0, 0)),
                  pl.BlockSpec((d, tn), lambda i, j: (0, j))],
        out_specs=pl.BlockSpec((tm, tn), lambda i, j: (i, j)),
        scratch_shapes=[pltpu.VMEM((tm, d), BF16)],
        compiler_params=_cparams(("parallel", "arbitrary")),
        name="inproj_gate" if sigmoid else "inproj_qkv",
    )(x, gain.reshape(1, d), w)


def _osm(s, v, m, l, acc):
    m_new = jnp.maximum(m, jnp.max(s, axis=-1, keepdims=True))
    alpha = jnp.exp(m - m_new)
    p = jnp.exp(s - m_new)
    l = alpha * l + jnp.sum(p, axis=-1, keepdims=True)
    acc = alpha * acc + _dot(p.astype(BF16), v)
    return m_new, l, acc


def _causal_keep(t):
    r = lax.broadcasted_iota(jnp.int32, (t, t), 0)
    c = lax.broadcasted_iota(jnp.int32, (t, t), 1)
    return c <= r


def _diff_kernel(lam_ref, gain_ref, rot_ref, cq_ref, sq_ref, ck_ref, sk_ref,
                 q_ref, k_ref, v_ref, o_ref, kr_ref, *, tq, lam_init):
    qi = pl.program_id(2)

    @pl.when(qi == 0)
    def _():
        kr_ref[...] = _rope(k_ref[...], rot_ref[...], ck_ref[...], sk_ref[...]).astype(BF16)

    qf = _rope(q_ref[...], rot_ref[...], cq_ref[...], sq_ref[...])
    lane = lax.broadcasted_iota(jnp.int32, qf.shape, 1)
    q1 = jnp.where(lane < DIFF_DIM, qf, 0.0).astype(BF16)
    q2 = jnp.where(lane >= DIFF_DIM, qf, 0.0).astype(BF16)
    scale = 1.0 / math.sqrt(DIFF_DIM)

    def step(j, carry, masked):
        m1, l1, a1, m2, l2, a2 = carry
        off = pl.multiple_of(j * tq, tq)
        k = kr_ref[pl.ds(off, tq), :]
        v = v_ref[pl.ds(off, tq), :]
        s1 = _dot_nt(q1, k) * scale
        s2 = _dot_nt(q2, k) * scale
        if masked:
            keep = _causal_keep(tq)
            s1 = jnp.where(keep, s1, -jnp.inf)
            s2 = jnp.where(keep, s2, -jnp.inf)
        m1, l1, a1 = _osm(s1, v, m1, l1, a1)
        m2, l2, a2 = _osm(s2, v, m2, l2, a2)
        return m1, l1, a1, m2, l2, a2

    neg = jnp.full((tq, 1), -jnp.inf, F32)
    zl = jnp.zeros((tq, 1), F32)
    za = jnp.zeros((tq, HEAD_DIM), F32)
    carry = lax.fori_loop(0, qi, lambda j, c: step(j, c, False), (neg, zl, za, neg, zl, za))
    m1, l1, a1, m2, l2, a2 = step(qi, carry, True)

    lam_rows = lam_ref[...]
    lam = (jnp.exp(jnp.sum(lam_rows[0:1] * lam_rows[1:2], axis=-1, keepdims=True))
           - jnp.exp(jnp.sum(lam_rows[2:3] * lam_rows[3:4], axis=-1, keepdims=True))
           + lam_init)
    out = a1 / l1 - lam * (a2 / l2)
    o_ref[...] = (_rms(out, gain_ref[...]) * (1.0 - lam_init)).astype(o_ref.dtype)


def _diff_attention(proj, lam_rows, gain, tables, *, batch, seq, lam_init, tq=256):
    c, s, rot = tables
    nq = seq // tq
    cb = A_OFF // HEAD_DIM
    return pl.pallas_call(
        functools.partial(_diff_kernel, tq=tq, lam_init=lam_init),
        out_shape=jax.ShapeDtypeStruct((batch * seq, BRANCH_WIDTH), BF16),
        grid=(batch, N_HEADS, nq),
        in_specs=[pl.BlockSpec((4, DIFF_DIM), lambda b, h, i: (0, 0)),
                  pl.BlockSpec((1, HEAD_DIM), lambda b, h, i: (0, 0)),
                  pl.BlockSpec((LANES, LANES), lambda b, h, i: (0, 0)),
                  pl.BlockSpec((tq, LANES), lambda b, h, i: (i, 0)),
                  pl.BlockSpec((tq, LANES), lambda b, h, i: (i, 0)),
                  pl.BlockSpec((seq, LANES), lambda b, h, i: (0, 0)),
                  pl.BlockSpec((seq, LANES), lambda b, h, i: (0, 0)),
                  pl.BlockSpec((tq, HEAD_DIM), lambda b, h, i: (b * nq + i, cb + h)),
                  pl.BlockSpec((seq, HEAD_DIM), lambda b, h, i: (b, cb + N_HEADS + h)),
                  pl.BlockSpec((seq, HEAD_DIM), lambda b, h, i: (b, cb + 2 * N_HEADS + h))],
        out_specs=pl.BlockSpec((tq, HEAD_DIM), lambda b, h, i: (b * nq + i, h)),
        scratch_shapes=[pltpu.VMEM((seq, HEAD_DIM), BF16)],
        compiler_params=_cparams(("parallel", "parallel", "arbitrary")),
        name="diff_attention",
    )(lam_rows, gain.reshape(1, HEAD_DIM), rot, c, s, c, s, proj, proj, proj)


def _mla_prep_kernel(cq_ref, ckv_ref, kr_ref, dqn_ref, dkvn_ref, wuq_ref, wuk_ref, wuv_ref,
                     rot_ref, c_ref, s_ref, q_out, k_out, v_out):
    rot = rot_ref[...]
    c = c_ref[...]
    s = s_ref[...]
    hq = _rms(cq_ref[...].astype(F32), dqn_ref[...]).astype(BF16)
    hkv = _rms(ckv_ref[...].astype(F32), dkvn_ref[...]).astype(BF16)
    q = _dot(hq, wuq_ref[...])
    kn = _dot(hkv, wuk_ref[...])
    v_out[...] = _dot(hkv, wuv_ref[...]).astype(BF16)
    k_rope = _rope(kr_ref[...], rot, c, s).astype(BF16)
    for h in range(N_HEADS):
        lo = h * MLA_QK
        q_out[:, lo:lo + NOPE_DIM] = q[:, lo:lo + NOPE_DIM].astype(BF16)
        q_out[:, lo + NOPE_DIM:lo + MLA_QK] = _rope(
            q[:, lo + NOPE_DIM:lo + MLA_QK].astype(BF16), rot, c, s).astype(BF16)
        k_out[:, lo:lo + NOPE_DIM] = kn[:, h * NOPE_DIM:(h + 1) * NOPE_DIM].astype(BF16)
        k_out[:, lo + NOPE_DIM:lo + MLA_QK] = k_rope


def _mla_prep(proj, dq_norm, dkv_norm, wuq, wuk, wuv, tables, *, seq, tm=512):
    c, s, rot = tables
    t = proj.shape[0]
    nb = seq // tm
    cb = B_OFF // Q_LORA
    full = lambda shape: pl.BlockSpec(shape, lambda i: (0, 0))
    return pl.pallas_call(
        _mla_prep_kernel,
        out_shape=(jax.ShapeDtypeStruct((t, N_HEADS * MLA_QK), BF16),
                   jax.ShapeDtypeStruct((t, N_HEADS * MLA_QK), BF16),
                   jax.ShapeDtypeStruct((t, N_HEADS * HEAD_DIM), BF16)),
        grid=(t // tm,),
        in_specs=[pl.BlockSpec((tm, Q_LORA), lambda i: (i, cb)),
                  pl.BlockSpec((tm, KV_LORA), lambda i: (i, cb + 1)),
                  pl.BlockSpec((tm, LANES), lambda i: (i, (B_OFF + Q_LORA + KV_LORA) // LANES)),
                  full((1, Q_LORA)), full((1, KV_LORA)),
                  full(wuq.shape), full(wuk.shape), full(wuv.shape),
                  full((LANES, LANES)),
                  pl.BlockSpec((tm, LANES), lambda i: (i % nb, 0)),
                  pl.BlockSpec((tm, LANES), lambda i: (i % nb, 0))],
        out_specs=(pl.BlockSpec((tm, N_HEADS * MLA_QK), lambda i: (i, 0)),
                   pl.BlockSpec((tm, N_HEADS * MLA_QK), lambda i: (i, 0)),
                   pl.BlockSpec((tm, N_HEADS * HEAD_DIM), lambda i: (i, 0))),
        compiler_params=_cparams(("parallel",)),
        name="mla_prep",
    )(proj, proj, proj, dq_norm.reshape(1, Q_LORA), dkv_norm.reshape(1, KV_LORA),
      wuq, wuk, wuv, rot, c, s)


def _flash_kernel(q_ref, k_ref, v_ref, o_ref, *, tq, scale):
    qi = pl.program_id(2)
    q = q_ref[...]

    def step(j, carry, masked):
        off = pl.multiple_of(j * tq, tq)
        s = _dot_nt(q, k_ref[pl.ds(off, tq), :]) * scale
        if masked:
            s = jnp.where(_causal_keep(tq), s, -jnp.inf)
        return _osm(s, v_ref[pl.ds(off, tq), :], *carry)

    init = (jnp.full((tq, 1), -jnp.inf, F32), jnp.zeros((tq, 1), F32),
            jnp.zeros((tq, HEAD_DIM), F32))
    carry = lax.fori_loop(0, qi, lambda j, c: step(j, c, False), init)
    _, l, acc = step(qi, carry, True)
    o_ref[...] = (acc / l).astype(o_ref.dtype)


def _mla_attention(q, k, v, *, batch, seq, tq=256):
    nq = seq // tq
    return pl.pallas_call(
        functools.partial(_flash_kernel, tq=tq, scale=1.0 / math.sqrt(NOPE_DIM + ROPE_DIM_B)),
        out_shape=jax.ShapeDtypeStruct((batch * seq, BRANCH_WIDTH), BF16),
        grid=(batch, N_HEADS, nq),
        in_specs=[pl.BlockSpec((tq, MLA_QK), lambda b, h, i: (b * nq + i, h)),
                  pl.BlockSpec((seq, MLA_QK), lambda b, h, i: (b, h)),
                  pl.BlockSpec((seq, HEAD_DIM), lambda b, h, i: (b, h))],
        out_specs=pl.BlockSpec((tq, HEAD_DIM), lambda b, h, i: (b * nq + i, h)),
        compiler_params=_cparams(("parallel", "parallel", "arbitrary")),
        name="mla_attention",
    )(q, k, v)


DIL_BLOCK = 128


def _dilated_kernel(rot_ref, c_ref, s_ref, *refs, seq):
    in_refs = refs[:9]
    o_ref = refs[9]
    qf, kf, vf, og, lse = refs[10:]
    rot = rot_ref[...]
    c = c_ref[...]
    s = s_ref[...]
    for g in range(N_DIL_GROUPS):
        qf[g] = _rope(in_refs[g][...], rot, c, s)
        kf[g] = _rope(in_refs[3 + g][...], rot, c, s)
        vf[g] = in_refs[6 + g][...].astype(F32)
    scale = 1.0 / math.sqrt(HEAD_DIM)
    blk = DIL_BLOCK

    def rows(start, size, stride):
        return pl.ds(start, size) if stride == 1 else pl.ds(start, size, stride=stride)

    def band(g, stride, q_start, k_start, nk):
        q = qf[g, rows(q_start, blk, stride), :].astype(BF16)
        k = kf[g, rows(k_start, nk, stride), :].astype(BF16)
        v = vf[g, rows(k_start, nk, stride), :].astype(BF16)
        sc = _dot_nt(q, k) * scale
        dist = (lax.broadcasted_iota(jnp.int32, (blk, nk), 0) + (nk - blk)
                - lax.broadcasted_iota(jnp.int32, (blk, nk), 1))
        sc = jnp.where(dist >= 0, jnp.where(dist <= blk, sc, -jnp.inf), -jnp.inf)
        m = jnp.max(sc, axis=-1, keepdims=True)
        e = jnp.exp(sc - m)
        den = jnp.sum(e, axis=-1, keepdims=True)
        og[g, rows(q_start, blk, stride), :] = _dot(e.astype(BF16), v) / den
        lse[g, rows(q_start, blk, stride), :] = jnp.broadcast_to(m + jnp.log(den), (blk, HEAD_DIM))

    for g, (window, dil) in enumerate(DIL_PAIRS):
        assert window == blk * dil
        n_sub = seq // (blk * dil)
        for r in range(dil):
            band(g, dil, r, r, blk)
            if n_sub > 1:
                span = blk * dil

                def body(cb, _, g=g, dil=dil, r=r, span=span):
                    q0 = r + cb * span
                    if dil == 1:
                        q0 = pl.multiple_of(q0, blk)
                    band(g, dil, q0, q0 - span, 2 * blk)
                    return 0

                lax.fori_loop(1, n_sub, body, 0)

    l0, l1, l2 = lse[0], lse[1], lse[2]
    mx = jnp.maximum(jnp.maximum(l0, l1), l2)
    w0, w1, w2 = jnp.exp(l0 - mx), jnp.exp(l1 - mx), jnp.exp(l2 - mx)
    o_ref[...] = ((w0 * og[0] + w1 * og[1] + w2 * og[2]) / (w0 + w1 + w2)).astype(o_ref.dtype)


def _dilated_attention(proj, tables, *, batch, seq):
    c, s, rot = tables
    cb = C_OFF // HEAD_DIM
    nh = N_DIL_GROUPS * N_HEADS

    def col(kind, g):
        return lambda b, h: (b, cb + kind * nh + g * N_HEADS + h)

    in_specs = [pl.BlockSpec((LANES, LANES), lambda b, h: (0, 0)),
                pl.BlockSpec((seq, LANES), lambda b, h: (0, 0)),
                pl.BlockSpec((seq, LANES), lambda b, h: (0, 0))]
    for kind in range(3):
        for g in range(N_DIL_GROUPS):
            in_specs.append(pl.BlockSpec((seq, HEAD_DIM), col(kind, g)))
    return pl.pallas_call(
        functools.partial(_dilated_kernel, seq=seq),
        out_shape=jax.ShapeDtypeStruct((batch * seq, BRANCH_WIDTH), BF16),
        grid=(batch, N_HEADS),
        in_specs=in_specs,
        out_specs=pl.BlockSpec((seq, HEAD_DIM), lambda b, h: (b, h)),
        scratch_shapes=[pltpu.VMEM((N_DIL_GROUPS, seq, HEAD_DIM), F32) for _ in range(5)],
        compiler_params=_cparams(("parallel", "parallel")),
        name="dilated_attention",
    )(rot, c, s, *([proj] * 9))


def _stick_kernel(q_ref, k_ref, v_ref, o_ref, *, tq, scale):
    qi = pl.program_id(2)
    q = q_ref[...]
    r = lax.broadcasted_iota(jnp.int32, (tq, tq), 0)
    c = lax.broadcasted_iota(jnp.int32, (tq, tq), 1)
    later_keys = jnp.where(r > c, 1.0, 0.0).astype(BF16)

    def step(j, carry, diag):
        tail, acc = carry
        off = pl.multiple_of(j * tq, tq)
        z = _dot_nt(q, k_ref[pl.ds(off, tq), :]) * scale
        sp = jnp.maximum(z, 0.0) + jnp.log1p(jnp.exp(-jnp.abs(z)))
        log_not = -sp
        if diag:
            strict = c < r
            log_not = jnp.where(strict, log_not, 0.0)
        hi = log_not.astype(BF16)
        lo = (log_not - hi.astype(F32)).astype(BF16)
        later = _dot(hi, later_keys) + _dot(lo, later_keys) + tail
        a = jnp.exp((z - sp) + later)
        if diag:
            a = jnp.where(strict, a, 0.0)
        acc = acc + _dot(a.astype(BF16), v_ref[pl.ds(off, tq), :])
        tail = tail + jnp.sum(log_not, axis=-1, keepdims=True)
        return tail, acc

    carry = step(qi, (jnp.zeros((tq, 1), F32), jnp.zeros((tq, HEAD_DIM), F32)), True)
    _, acc = lax.fori_loop(0, qi, lambda t, cr: step(qi - 1 - t, cr, False), carry)
    o_ref[...] = acc.astype(o_ref.dtype)


def _stick_attention(proj, *, batch, seq, tq=256):
    nq = seq // tq
    cb = D_OFF // HEAD_DIM
    return pl.pallas_call(
        functools.partial(_stick_kernel, tq=tq, scale=1.0 / math.sqrt(HEAD_DIM)),
        out_shape=jax.ShapeDtypeStruct((batch * seq, BRANCH_WIDTH), BF16),
        grid=(batch, N_HEADS, nq),
        in_specs=[pl.BlockSpec((tq, HEAD_DIM), lambda b, h, i: (b * nq + i, cb + h)),
                  pl.BlockSpec((seq, HEAD_DIM), lambda b, h, i: (b, cb + N_HEADS + h)),
                  pl.BlockSpec((seq, HEAD_DIM), lambda b, h, i: (b, cb + 2 * N_HEADS + h))],
        out_specs=pl.BlockSpec((tq, HEAD_DIM), lambda b, h, i: (b * nq + i, h)),
        compiler_params=_cparams(("parallel", "parallel", "arbitrary")),
        name="stick_attention",
    )(proj, proj, proj)


def _merge_kernel(ya, yb, yc, yd, g0, g1, g2, g3, wb_ref, o_ref):
    acc = None
    for i, (y, g) in enumerate(((ya, g0), (yb, g1), (yc, g2), (yd, g3))):
        t = g[...].astype(F32) * _dot(y[...], wb_ref[i])
        acc = t if acc is None else acc + t
    o_ref[...] = acc.astype(o_ref.dtype)


def _merge(ys, gates, wb, *, tm=1024, tn=512):
    t = gates.shape[0]
    d = wb.shape[2]
    nn = d // tn
    y_spec = pl.BlockSpec((tm, BRANCH_WIDTH), lambda i, j: (i, 0))
    g_specs = [pl.BlockSpec((tm, tn), functools.partial(lambda i, j, b: (i, b * nn + j), b=b))
               for b in range(N_BRANCH)]
    return pl.pallas_call(
        _merge_kernel,
        out_shape=jax.ShapeDtypeStruct((t, d), BF16),
        grid=(t // tm, nn),
        in_specs=[y_spec] * N_BRANCH + g_specs
                 + [pl.BlockSpec((N_BRANCH, BRANCH_WIDTH, tn), lambda i, j: (0, 0, j))],
        out_specs=pl.BlockSpec((tm, tn), lambda i, j: (i, j)),
        compiler_params=_cparams(("parallel", "arbitrary")),
        name="branch_merge",
    )(*ys, gates, gates, gates, gates, wb)


def _outproj_kernel(x_ref, m_ref, w_ref, o_ref):
    o_ref[...] = x_ref[...] + _dot(m_ref[...], w_ref[...])


def _outproj(x, merged, w, *, tm=1024, tn=512):
    t, d = x.shape
    return pl.pallas_call(
        _outproj_kernel,
        out_shape=jax.ShapeDtypeStruct((t, d), F32),
        grid=(t // tm, d // tn),
        in_specs=[pl.BlockSpec((tm, tn), lambda i, j: (i, j)),
                  pl.BlockSpec((tm, d), lambda i, j: (i, 0)),
                  pl.BlockSpec((d, tn), lambda i, j: (0, j))],
        out_specs=pl.BlockSpec((tm, tn), lambda i, j: (i, j)),
        compiler_params=_cparams(("parallel", "arbitrary")),
        name="out_proj",
    )(x, merged, w)


def _swiglu_step(h, wg, wu, wo):
    g = _dot(h, wg)
    u = _dot(h, wu)
    return _dot((g * _sigmoid(g) * u).astype(BF16), wo)


def _ffn_kernel(x_ref, gain_ref, wg_ref, wu_ref, wo_ref, o_ref, h_ref, acc_ref):
    f = pl.program_id(1)

    @pl.when(f == 0)
    def _():
        h_ref[...] = _rms(x_ref[...], gain_ref[...]).astype(BF16)
        acc_ref[...] = jnp.zeros_like(acc_ref)

    acc_ref[...] += _swiglu_step(h_ref[...], wg_ref[...].astype(BF16),
                                 wu_ref[...].astype(BF16), wo_ref[...].astype(BF16))

    @pl.when(f == pl.num_programs(1) - 1)
    def _():
        o_ref[...] = x_ref[...] + acc_ref[...]


def _dense_ffn(x, gain, w_in, w_out, *, tm=512, tf=512):
    t, d = x.shape
    ff = w_out.shape[0]
    nf = ff // tf
    return pl.pallas_call(
        _ffn_kernel,
        out_shape=jax.ShapeDtypeStruct((t, d), F32),
        grid=(t // tm, nf),
        in_specs=[pl.BlockSpec((tm, d), lambda i, f: (i, 0)),
                  pl.BlockSpec((1, d), lambda i, f: (0, 0)),
                  pl.BlockSpec((d, tf), lambda i, f: (0, f)),
                  pl.BlockSpec((d, tf), lambda i, f: (0, nf + f)),
                  pl.BlockSpec((tf, d), lambda i, f: (f, 0))],
        out_specs=pl.BlockSpec((tm, d), lambda i, f: (i, 0)),
        scratch_shapes=[pltpu.VMEM((tm, d), BF16), pltpu.VMEM((tm, d), F32)],
        compiler_params=_cparams(("parallel", "arbitrary")),
        name="dense_ffn",
    )(x, gain.reshape(1, d), w_in, w_in, w_out)


def _router_kernel(x_ref, gain_ref, wr_ref, idx_ref, gate_ref):
    h = _rms(x_ref[...], gain_ref[...])
    logits = jnp.dot(h, wr_ref[...], preferred_element_type=F32, precision=lax.Precision.HIGHEST)
    lane = lax.broadcasted_iota(jnp.int32, logits.shape, 1)
    lanef = lane.astype(F32)
    lg = jnp.where(lane < N_EXPERTS, logits, -jnp.inf)
    v1 = jnp.max(lg, axis=-1, keepdims=True)
    i1 = jnp.min(jnp.where(lg == v1, lanef, float(LANES)), axis=-1, keepdims=True)
    lg2 = jnp.where(lanef == i1, -jnp.inf, lg)
    v2 = jnp.max(lg2, axis=-1, keepdims=True)
    i2 = jnp.min(jnp.where(lg2 == v2, lanef, float(LANES)), axis=-1, keepdims=True)
    e2 = jnp.exp(v2 - v1)
    g1 = 1.0 / (1.0 + e2)
    g2 = e2 / (1.0 + e2)
    idx_ref[...] = jnp.where(lane == 0, i1, jnp.where(lane == 1, i2, 0.0)).astype(jnp.int32)
    gate_ref[...] = jnp.where(lane == 0, g1, jnp.where(lane == 1, g2, 0.0))


def _router(x, gain, w_router, *, tm=512):
    t, d = x.shape
    wr = jnp.zeros((d, LANES), F32).at[:, :N_EXPERTS].set(w_router)
    return pl.pallas_call(
        _router_kernel,
        out_shape=(jax.ShapeDtypeStruct((t, LANES), jnp.int32),
                   jax.ShapeDtypeStruct((t, LANES), F32)),
        grid=(t // tm,),
        in_specs=[pl.BlockSpec((tm, d), lambda i: (i, 0)),
                  pl.BlockSpec((1, d), lambda i: (0, 0)),
                  pl.BlockSpec((d, LANES), lambda i: (0, 0))],
        out_specs=(pl.BlockSpec((tm, LANES), lambda i: (i, 0)),
                   pl.BlockSpec((tm, LANES), lambda i: (i, 0))),
        compiler_params=_cparams(("parallel",)),
        name="router",
    )(x, gain.reshape(1, d), wr)


def _gather_rows(idx_ref, base, n, src_hbm, dst, sem):
    def copy(r, row):
        return pltpu.make_async_copy(src_hbm.at[pl.ds(row, 1), :], dst.at[pl.ds(r, 1), :], sem)

    def start(r, _):
        copy(r, idx_ref[base + r]).start()
        return 0

    def wait(r, _):
        copy(r, 0).wait()
        return 0

    lax.fori_loop(0, n, start, 0)
    lax.fori_loop(0, n, wait, 0)


def _moe_kernel(tile_e_ref, tile_ok_ref, row_tok_ref, x_hbm, gain_ref, rg_ref,
                wg_ref, wu_ref, wo_ref, o_ref, xbuf, h_ref, acc_ref, sem, *, tm):
    del tile_e_ref
    i = pl.program_id(0)
    f = pl.program_id(1)
    ok = tile_ok_ref[i] > 0

    @pl.when(jnp.logical_and(ok, f == 0))
    def _():
        _gather_rows(row_tok_ref, i * tm, tm, x_hbm, xbuf, sem)
        h_ref[...] = _rms(xbuf[...], gain_ref[...]).astype(BF16)
        acc_ref[...] = jnp.zeros_like(acc_ref)

    @pl.when(ok)
    def _():
        acc_ref[...] += _swiglu_step(h_ref[...], wg_ref[...].astype(BF16),
                                     wu_ref[...].astype(BF16), wo_ref[...].astype(BF16))

    @pl.when(f == pl.num_programs(1) - 1)
    def _():
        @pl.when(ok)
        def _():
            o_ref[...] = rg_ref[...] * acc_ref[...]

        @pl.when(jnp.logical_not(ok))
        def _():
            o_ref[...] = jnp.zeros_like(o_ref)


def _moe_experts(x, gain, tile_e, tile_ok, row_tok, row_gate, w_in, w_out, *, tm, tf=512):
    t, d = x.shape
    ff = w_out.shape[1]
    nf = ff // tf
    p = row_tok.shape[0]

    def f_eff(i, f, ok):
        return jnp.where(ok[i] > 0, f, nf - 1)

    grid_spec = pltpu.PrefetchScalarGridSpec(
        num_scalar_prefetch=3,
        grid=(p // tm, nf),
        in_specs=[pl.BlockSpec(memory_space=pl.ANY),
                  pl.BlockSpec((1, d), lambda i, f, e, ok, rt: (0, 0)),
                  pl.BlockSpec((tm, 1), lambda i, f, e, ok, rt: (i, 0)),
                  pl.BlockSpec((None, d, tf), lambda i, f, e, ok, rt: (e[i], 0, f_eff(i, f, ok))),
                  pl.BlockSpec((None, d, tf), lambda i, f, e, ok, rt: (e[i], 0, nf + f_eff(i, f, ok))),
                  pl.BlockSpec((None, tf, d), lambda i, f, e, ok, rt: (e[i], f_eff(i, f, ok), 0))],
        out_specs=pl.BlockSpec((tm, d), lambda i, f, e, ok, rt: (i, 0)),
        scratch_shapes=[pltpu.VMEM((tm, d), F32), pltpu.VMEM((tm, d), BF16),
                        pltpu.VMEM((tm, d), F32), pltpu.SemaphoreType.DMA],
    )
    return pl.pallas_call(
        functools.partial(_moe_kernel, tm=tm),
        out_shape=jax.ShapeDtypeStruct((p, d), F32),
        grid_spec=grid_spec,
        compiler_params=_cparams(("arbitrary", "arbitrary")),
        name="moe_experts",
    )(tile_e, tile_ok, row_tok, x, gain.reshape(1, d), row_gate.reshape(p, 1), w_in, w_in, w_out)


def _combine_kernel(pos_ref, x_ref, y_hbm, gain_ref, o_ref, buf0, buf1, sem0, sem1, *, tm):
    i = pl.program_id(0)
    _gather_rows(pos_ref, i * tm, tm, y_hbm, buf0, sem0)
    _gather_rows(pos_ref, pos_ref.shape[0] // 2 + i * tm, tm, y_hbm, buf1, sem1)
    x = x_ref[...] + (buf0[...] + buf1[...])
    if gain_ref is None:
        o_ref[...] = x
    else:
        o_ref[...] = _rms(x, gain_ref[...])


def _moe_combine(x, y_rows, pos, final_gain, *, tm=256):
    t, d = x.shape
    in_specs = [pl.BlockSpec((tm, d), lambda i, ps: (i, 0)),
                pl.BlockSpec(memory_space=pl.ANY)]
    args = [x, y_rows]
    if final_gain is not None:
        in_specs.append(pl.BlockSpec((1, d), lambda i, ps: (0, 0)))
        args.append(final_gain.reshape(1, d))
        body = functools.partial(_combine_kernel, tm=tm)
    else:
        def body(pos_ref, x_ref, y_hbm, o_ref, *scratch):
            _combine_kernel(pos_ref, x_ref, y_hbm, None, o_ref, *scratch, tm=tm)
    grid_spec = pltpu.PrefetchScalarGridSpec(
        num_scalar_prefetch=1,
        grid=(t // tm,),
        in_specs=in_specs,
        out_specs=pl.BlockSpec((tm, d), lambda i, ps: (i, 0)),
        scratch_shapes=[pltpu.VMEM((tm, d), F32), pltpu.VMEM((tm, d), F32),
                        pltpu.SemaphoreType.DMA, pltpu.SemaphoreType.DMA],
    )
    return pl.pallas_call(
        body,
        out_shape=jax.ShapeDtypeStruct((t, d), F32),
        grid_spec=grid_spec,
        compiler_params=_cparams(("arbitrary",)),
        name="moe_combine",
    )(pos, *args)


def _final_norm_kernel(x_ref, g_ref, o_ref):
    o_ref[...] = _rms(x_ref[...], g_ref[...])


def _final_norm(x, gain, *, tm=512):
    t, d = x.shape
    return pl.pallas_call(
        _final_norm_kernel,
        out_shape=jax.ShapeDtypeStruct((t, d), F32),
        grid=(t // tm,),
        in_specs=[pl.BlockSpec((tm, d), lambda i: (i, 0)), pl.BlockSpec((1, d), lambda i: (0, 0))],
        out_specs=pl.BlockSpec((tm, d), lambda i: (i, 0)),
        compiler_params=_cparams(("parallel",)),
        name="final_norm",
    )(x, gain.reshape(1, d))


def _routing_tables(idx, gates, tm):
    t = idx.shape[0]
    e_flat = idx[:, :TOP_K].T.reshape(-1)
    g_flat = gates[:, :TOP_K].T.reshape(-1)
    tok = jnp.tile(jnp.arange(t, dtype=jnp.int32), TOP_K)
    onehot = (e_flat[:, None] == jnp.arange(N_EXPERTS, dtype=jnp.int32)[None, :]).astype(jnp.int32)
    rank = jnp.sum((jnp.cumsum(onehot, axis=0) - onehot) * onehot, axis=1)
    counts = jnp.sum(onehot, axis=0)
    padded = ((counts + tm - 1) // tm) * tm
    ends = jnp.cumsum(padded)
    starts = ends - padded
    dest = (starts[e_flat] + rank).astype(jnp.int32)
    p = TOP_K * t + N_EXPERTS * tm
    row_tok = jnp.zeros((p,), jnp.int32).at[dest].set(tok)
    row_gate = jnp.zeros((p,), F32).at[dest].set(g_flat)
    tile_start = jnp.arange(p // tm, dtype=jnp.int32) * tm
    tile_ok = (tile_start < ends[-1]).astype(jnp.int32)
    tile_e = jnp.searchsorted(ends, tile_start, side="right").astype(jnp.int32)
    last_e = tile_e[ends[-1] // tm - 1]
    tile_e = jnp.where(tile_ok > 0, tile_e, last_e)
    return tile_e, tile_ok, row_tok, row_gate, dest


def _relayout_w_in(w):
    b_end = 1536 + Q_LORA + KV_LORA + ROPE_DIM_B
    gate0 = w.shape[1] - N_BRANCH * w.shape[0]
    pad = jnp.zeros((w.shape[0], C_OFF - b_end), BF16)
    w_qkv = jnp.concatenate([w[:, :b_end].astype(BF16), pad, w[:, b_end:gate0].astype(BF16)], axis=1)
    return w_qkv, w[:, gate0:].astype(BF16)


def _relayout_mla(w_uq, w_ukv):
    dq = NOPE_DIM + ROPE_DIM_B
    wq = w_uq.reshape(Q_LORA, N_HEADS, dq)
    wq = jnp.concatenate([wq, jnp.zeros((Q_LORA, N_HEADS, MLA_QK - dq), w_uq.dtype)], axis=2)
    wkv = w_ukv.reshape(KV_LORA, N_HEADS, NOPE_DIM + HEAD_DIM)
    wk = wkv[:, :, :NOPE_DIM].reshape(KV_LORA, N_HEADS * NOPE_DIM)
    wv = wkv[:, :, NOPE_DIM:].reshape(KV_LORA, N_HEADS * HEAD_DIM)
    return wq.reshape(Q_LORA, N_HEADS * MLA_QK).astype(BF16), wk.astype(BF16), wv.astype(BF16)


def kernel(x, norm_mix, w_in, dq_norm, dkv_norm, w_uq, w_ukv, lam_q1, lam_k1, lam_q2, lam_k2,
           diff_norm, w_branch, w_out, norm_ffn, w_dense_in, w_dense_out, w_router,
           w_moe_in, w_moe_out, norm_final):
    batch, seq, d = x.shape
    depth = w_in.shape[0]
    t = batch * seq
    xt = x.reshape(t, d)

    rot_a = DIFF_DIM // 4
    tab_a = _rope_tables(seq, ((0, rot_a), (DIFF_DIM, rot_a)))
    tab_b = _rope_tables(seq, ((0, ROPE_DIM_B),))
    tab_c = _rope_tables(seq, ((0, HEAD_DIM // 4),))
    moe_tm = 512

    for layer in range(depth):
        w_qkv, w_gate = _relayout_w_in(w_in[layer])
        proj = _inproj(xt, norm_mix[layer], w_qkv, sigmoid=False)
        gates = _inproj(xt, norm_mix[layer], w_gate, sigmoid=True)

        lam_init = 0.8 - 0.6 * math.exp(-0.3 * layer)
        lam_rows = jnp.stack([lam_q1[layer], lam_k1[layer], lam_q2[layer], lam_k2[layer]])
        y_a = _diff_attention(proj, lam_rows, diff_norm[layer], tab_a,
                              batch=batch, seq=seq, lam_init=lam_init)

        wuq, wuk, wuv = _relayout_mla(w_uq[layer], w_ukv[layer])
        q_b, k_b, v_b = _mla_prep(proj, dq_norm[layer], dkv_norm[layer], wuq, wuk, wuv, tab_b, seq=seq)
        y_b = _mla_attention(q_b, k_b, v_b, batch=batch, seq=seq)

        y_c = _dilated_attention(proj, tab_c, batch=batch, seq=seq)
        y_d = _stick_attention(proj, batch=batch, seq=seq)

        merged = _merge((y_a, y_b, y_c, y_d), gates, w_branch[layer].astype(BF16))
        xt = _outproj(xt, merged, w_out[layer].astype(BF16))

        last = layer == depth - 1
        if layer % 2 == 0:
            xt = _dense_ffn(xt, norm_ffn[layer], w_dense_in[layer // 2].astype(BF16),
                            w_dense_out[layer // 2].astype(BF16))
            if last:
                xt = _final_norm(xt, norm_final)
        else:
            m = layer // 2
            idx, gate = _router(xt, norm_ffn[layer], w_router[m])
            tile_e, tile_ok, row_tok, row_gate, dest = _routing_tables(idx, gate, moe_tm)
            y_rows = _moe_experts(xt, norm_ffn[layer], tile_e, tile_ok, row_tok, row_gate,
                                  w_moe_in[m].astype(BF16), w_moe_out[m].astype(BF16), tm=moe_tm)
            xt = _moe_combine(xt, y_rows, dest, norm_final if last else None)
    return xt.reshape(batch, seq, d)
```

```python
import functools
import math

import numpy as np
import jax
import jax.numpy as jnp
from jax import lax
from jax.experimental import pallas as pl
from jax.experimental.pallas import tpu as pltpu

F32 = jnp.float32
BF16 = jnp.bfloat16

HEAD_DIM = 128
ROPE_THETA = 500000.0
NORM_EPS = 1e-6
N_BRANCH = 4
BRANCH_WIDTH = 512
N_HEADS = 4
DIFF_DIM = 64
Q_LORA = 512
KV_LORA = 512
NOPE_DIM = 128
ROPE_DIM_B = 64
MLA_QK = 256
DIL_PAIRS = ((128, 1), (512, 4), (2048, 16))
N_DIL_GROUPS = 3
N_EXPERTS = 8
TOP_K = 2
LANES = 128
LOG2_E = math.log2(math.e)

A_OFF = 0
B_OFF = 1536
C_OFF = 3072
D_OFF = 7680
QKV_COLS = 9216

VMEM_LIMIT = 56 * 1024 * 1024


def _cparams(sem, vmem=VMEM_LIMIT):
    return pltpu.CompilerParams(dimension_semantics=sem, vmem_limit_bytes=vmem)


def _dot(a, b):
    return jnp.dot(a, b, preferred_element_type=F32)


def _dot_nt(a, b):
    return lax.dot_general(a, b, (((1,), (1,)), ((), ())), preferred_element_type=F32)


def _rms(x, gain):
    return x * lax.rsqrt(jnp.mean(x * x, axis=-1, keepdims=True) + NORM_EPS) * gain


def _sigmoid(x):
    return 1.0 / (1.0 + jnp.exp(-x))


def _rope_tables(seq, segments):
    pos = jnp.arange(seq, dtype=F32)
    c = jnp.ones((seq, LANES), F32)
    s = jnp.zeros((seq, LANES), F32)
    r = np.zeros((LANES, LANES), np.float32)
    for start, rot in segments:
        half = rot // 2
        inv_freq = ROPE_THETA ** (-jnp.arange(0, rot, 2, dtype=F32) / rot)
        ang = pos[:, None] * inv_freq[None, :]
        cos, sin = jnp.cos(ang), jnp.sin(ang)
        c = c.at[:, start:start + half].set(cos).at[:, start + half:start + rot].set(cos)
        s = s.at[:, start:start + half].set(-sin).at[:, start + half:start + rot].set(sin)
        for i in range(half):
            r[start + half + i, start + i] = 1.0
            r[start + i, start + half + i] = 1.0
    return c, s, jnp.asarray(r, BF16)


def _rope(x_bf16, rot, c, s):
    return x_bf16.astype(F32) * c + _dot(x_bf16, rot) * s


def _inproj_kernel(x_ref, g_ref, w_ref, o_ref, h_ref, *, sigmoid):
    @pl.when(pl.program_id(1) == 0)
    def _():
        h_ref[...] = _rms(x_ref[...], g_ref[...]).astype(BF16)

    acc = _dot(h_ref[...], w_ref[...])
    if sigmoid:
        acc = _sigmoid(acc)
    o_ref[...] = acc.astype(o_ref.dtype)


def _inproj(x, gain, w, *, sigmoid, tm=1024, tn=512):
    t, d = x.shape
    n = w.shape[1]
    return pl.pallas_call(
        functools.partial(_inproj_kernel, sigmoid=sigmoid),
        out_shape=jax.ShapeDtypeStruct((t, n), BF16),
        grid=(t // tm, n // tn),
        in_specs=[pl.BlockSpec((tm, d), lambda i, j: (i, 0)),
                  pl.BlockSpec((1, d), lambda i, j: (0, 0)),
                  pl.BlockSpec((d, tn), lambda i, j: (0, j))],
        out_specs=pl.BlockSpec((tm, tn), lambda i, j: (i, j)),
        scratch_shapes=[pltpu.VMEM((tm, d), BF16)],
        compiler_params=_cparams(("parallel", "arbitrary")),
        name="inproj_gate" if sigmoid else "inproj_qkv",
    )(x, gain.reshape(1, d), w)


def _osm(scores, values, carries):
    stats = []
    for s, (m, l, _) in zip(scores, carries):
        m_new = jnp.maximum(m, jnp.max(s, axis=-1, keepdims=True))
        alpha = jnp.exp2(m - m_new)
        p = jnp.exp2(s - m_new)
        stats.append((m_new, alpha * l + jnp.sum(p, axis=-1, keepdims=True), alpha, p))
    return tuple((m_new, l, alpha * acc + _dot(p.astype(BF16), v))
                 for (m_new, l, alpha, p), v, (_, _, acc) in zip(stats, values, carries))


def _causal_keep(t):
    r = lax.broadcasted_iota(jnp.int32, (t, t), 0)
    c = lax.broadcasted_iota(jnp.int32, (t, t), 1)
    return c <= r


def _diff_kernel(lam_ref, gain_ref, rot_ref, cq_ref, sq_ref, ck_ref, sk_ref,
                 q_ref, k_ref, v_ref, o_ref, kr_ref, *, tq, hp, lam_init):
    qi = pl.program_id(2)
    rot = rot_ref[...]
    heads = [slice(h * HEAD_DIM, (h + 1) * HEAD_DIM) for h in range(hp)]

    @pl.when(qi == 0)
    def _():
        for hs in heads:
            kr_ref[:, hs] = _rope(k_ref[:, hs], rot, ck_ref[...], sk_ref[...]).astype(BF16)

    scale2 = LOG2_E / math.sqrt(DIFF_DIM)
    lane = lax.broadcasted_iota(jnp.int32, (tq, HEAD_DIM), 1)
    chains = []
    for hs in heads:
        qf = _rope(q_ref[:, hs], rot, cq_ref[...], sq_ref[...])
        chains.append((jnp.where(lane < DIFF_DIM, qf, 0.0).astype(BF16), hs))
        chains.append((jnp.where(lane >= DIFF_DIM, qf, 0.0).astype(BF16), hs))

    def step(j, carry, masked):
        off = pl.multiple_of(j * tq, tq)
        scores = [_dot_nt(q, kr_ref[pl.ds(off, tq), hs]) * scale2 for q, hs in chains]
        if masked:
            keep = _causal_keep(tq)
            scores = [jnp.where(keep, s, -jnp.inf) for s in scores]
        return _osm(scores, [v_ref[pl.ds(off, tq), hs] for _, hs in chains], carry)

    init = (jnp.full((tq, 1), -jnp.inf, F32), jnp.zeros((tq, 1), F32),
            jnp.zeros((tq, HEAD_DIM), F32))
    carry = lax.fori_loop(0, qi, lambda j, c: step(j, c, False), tuple(init for _ in chains))
    carry = step(qi, carry, True)

    lam_rows = lam_ref[...]
    lam = (jnp.exp(jnp.sum(lam_rows[0:1] * lam_rows[1:2], axis=-1, keepdims=True))
           - jnp.exp(jnp.sum(lam_rows[2:3] * lam_rows[3:4], axis=-1, keepdims=True))
           + lam_init)
    for i, hs in enumerate(heads):
        (_, l1, a1), (_, l2, a2) = carry[2 * i], carry[2 * i + 1]
        out = a1 / l1 - lam * (a2 / l2)
        o_ref[:, hs] = (_rms(out, gain_ref[...]) * (1.0 - lam_init)).astype(o_ref.dtype)


def _diff_attention(proj, lam_rows, gain, tables, *, batch, seq, lam_init, tq=256, hp=4):
    c, s, rot = tables
    nq = seq // tq
    w = hp * HEAD_DIM
    cb = A_OFF // w
    ng = N_HEADS // hp
    return pl.pallas_call(
        functools.partial(_diff_kernel, tq=tq, hp=hp, lam_init=lam_init),
        out_shape=jax.ShapeDtypeStruct((batch * seq, BRANCH_WIDTH), BF16),
        grid=(batch, ng, nq),
        in_specs=[pl.BlockSpec((4, DIFF_DIM), lambda b, h, i: (0, 0)),
                  pl.BlockSpec((1, HEAD_DIM), lambda b, h, i: (0, 0)),
                  pl.BlockSpec((LANES, LANES), lambda b, h, i: (0, 0)),
                  pl.BlockSpec((tq, LANES), lambda b, h, i: (i, 0)),
                  pl.BlockSpec((tq, LANES), lambda b, h, i: (i, 0)),
                  pl.BlockSpec((seq, LANES), lambda b, h, i: (0, 0)),
                  pl.BlockSpec((seq, LANES), lambda b, h, i: (0, 0)),
                  pl.BlockSpec((tq, w), lambda b, h, i: (b * nq + i, cb + h)),
                  pl.BlockSpec((seq, w), lambda b, h, i: (b, cb + ng + h)),
                  pl.BlockSpec((seq, w), lambda b, h, i: (b, cb + 2 * ng + h))],
        out_specs=pl.BlockSpec((tq, w), lambda b, h, i: (b * nq + i, h)),
        scratch_shapes=[pltpu.VMEM((seq, w), BF16)],
        compiler_params=_cparams(("parallel", "parallel", "arbitrary")),
        name="diff_attention",
    )(lam_rows, gain.reshape(1, HEAD_DIM), rot, c, s, c, s, proj, proj, proj)


def _mla_prep_kernel(cq_ref, ckv_ref, kr_ref, dqn_ref, dkvn_ref, wuq_ref, wuk_ref, wuv_ref,
                     rot_ref, c_ref, s_ref, q_out, k_out, v_out):
    rot = rot_ref[...]
    c = c_ref[...]
    s = s_ref[...]
    hq = _rms(cq_ref[...].astype(F32), dqn_ref[...]).astype(BF16)
    hkv = _rms(ckv_ref[...].astype(F32), dkvn_ref[...]).astype(BF16)
    q = _dot(hq, wuq_ref[...])
    kn = _dot(hkv, wuk_ref[...])
    v_out[...] = _dot(hkv, wuv_ref[...]).astype(BF16)
    k_rope = _rope(kr_ref[...], rot, c, s).astype(BF16)
    for h in range(N_HEADS):
        lo = h * MLA_QK
        q_out[:, lo:lo + NOPE_DIM] = q[:, lo:lo + NOPE_DIM].astype(BF16)
        q_out[:, lo + NOPE_DIM:lo + MLA_QK] = _rope(
            q[:, lo + NOPE_DIM:lo + MLA_QK].astype(BF16), rot, c, s).astype(BF16)
        k_out[:, lo:lo + NOPE_DIM] = kn[:, h * NOPE_DIM:(h + 1) * NOPE_DIM].astype(BF16)
        k_out[:, lo + NOPE_DIM:lo + MLA_QK] = k_rope


def _mla_prep(proj, dq_norm, dkv_norm, wuq, wuk, wuv, tables, *, seq, tm=512):
    c, s, rot = tables
    t = proj.shape[0]
    nb = seq // tm
    cb = B_OFF // Q_LORA
    full = lambda shape: pl.BlockSpec(shape, lambda i: (0, 0))
    return pl.pallas_call(
        _mla_prep_kernel,
        out_shape=(jax.ShapeDtypeStruct((t, N_HEADS * MLA_QK), BF16),
                   jax.ShapeDtypeStruct((t, N_HEADS * MLA_QK), BF16),
                   jax.ShapeDtypeStruct((t, N_HEADS * HEAD_DIM), BF16)),
        grid=(t // tm,),
        in_specs=[pl.BlockSpec((tm, Q_LORA), lambda i: (i, cb)),
                  pl.BlockSpec((tm, KV_LORA), lambda i: (i, cb + 1)),
                  pl.BlockSpec((tm, LANES), lambda i: (i, (B_OFF + Q_LORA + KV_LORA) // LANES)),
                  full((1, Q_LORA)), full((1, KV_LORA)),
                  full(wuq.shape), full(wuk.shape), full(wuv.shape),
                  full((LANES, LANES)),
                  pl.BlockSpec((tm, LANES), lambda i: (i % nb, 0)),
                  pl.BlockSpec((tm, LANES), lambda i: (i % nb, 0))],
        out_specs=(pl.BlockSpec((tm, N_HEADS * MLA_QK), lambda i: (i, 0)),
                   pl.BlockSpec((tm, N_HEADS * MLA_QK), lambda i: (i, 0)),
                   pl.BlockSpec((tm, N_HEADS * HEAD_DIM), lambda i: (i, 0))),
        compiler_params=_cparams(("parallel",)),
        name="mla_prep",
    )(proj, proj, proj, dq_norm.reshape(1, Q_LORA), dkv_norm.reshape(1, KV_LORA),
      wuq, wuk, wuv, rot, c, s)


def _flash_kernel(q_ref, k_ref, v_ref, o_ref, *, tq, hp, dqk, scale):
    qi = pl.program_id(2)
    qk = [slice(h * dqk, (h + 1) * dqk) for h in range(hp)]
    hv = [slice(h * HEAD_DIM, (h + 1) * HEAD_DIM) for h in range(hp)]
    qs = [q_ref[:, sl] for sl in qk]

    def step(j, carry, masked):
        off = pl.multiple_of(j * tq, tq)
        scores = [_dot_nt(q, k_ref[pl.ds(off, tq), ks]) * (scale * LOG2_E) for q, ks in zip(qs, qk)]
        if masked:
            keep = _causal_keep(tq)
            scores = [jnp.where(keep, s, -jnp.inf) for s in scores]
        return _osm(scores, [v_ref[pl.ds(off, tq), vs] for vs in hv], carry)

    init = (jnp.full((tq, 1), -jnp.inf, F32), jnp.zeros((tq, 1), F32),
            jnp.zeros((tq, HEAD_DIM), F32))
    carry = lax.fori_loop(0, qi, lambda j, c: step(j, c, False), tuple(init for _ in qs))
    carry = step(qi, carry, True)
    for vs, (_, l, acc) in zip(hv, carry):
        o_ref[:, vs] = (acc / l).astype(o_ref.dtype)


def _mla_attention(q, k, v, *, batch, seq, tq=256, hp=4):
    nq = seq // tq
    ng = N_HEADS // hp
    return pl.pallas_call(
        functools.partial(_flash_kernel, tq=tq, hp=hp, dqk=MLA_QK,
                          scale=1.0 / math.sqrt(NOPE_DIM + ROPE_DIM_B)),
        out_shape=jax.ShapeDtypeStruct((batch * seq, BRANCH_WIDTH), BF16),
        grid=(batch, ng, nq),
        in_specs=[pl.BlockSpec((tq, hp * MLA_QK), lambda b, h, i: (b * nq + i, h)),
                  pl.BlockSpec((seq, hp * MLA_QK), lambda b, h, i: (b, h)),
                  pl.BlockSpec((seq, hp * HEAD_DIM), lambda b, h, i: (b, h))],
        out_specs=pl.BlockSpec((tq, hp * HEAD_DIM), lambda b, h, i: (b * nq + i, h)),
        compiler_params=_cparams(("parallel", "parallel", "arbitrary")),
        name="mla_attention",
    )(q, k, v)


DIL_BLOCK = 128


def _dilated_kernel(rot_ref, c_ref, s_ref, *refs, seq):
    in_refs = refs[:9]
    o_ref = refs[9]
    qf, kf, vf, og, lse = refs[10:]
    rot = rot_ref[...]
    c = c_ref[...]
    s = s_ref[...]
    for g in range(N_DIL_GROUPS):
        qf[g] = _rope(in_refs[g][...], rot, c, s)
        kf[g] = _rope(in_refs[3 + g][...], rot, c, s)
        vf[g] = in_refs[6 + g][...].astype(F32)
    scale = 1.0 / math.sqrt(HEAD_DIM)
    blk = DIL_BLOCK

    def rows(start, size, stride):
        return pl.ds(start, size) if stride == 1 else pl.ds(start, size, stride=stride)

    for g, (window, dil) in enumerate(DIL_PAIRS):
        assert window == blk * dil
        span = blk * dil
        n_sub = seq // span
        nk = 2 * blk if n_sub > 1 else blk
        assert n_sub & (n_sub - 1) == 0
        q_rows, k_rows = [], []
        for r in range(dil):
            for cb in range(n_sub):
                q_rows.append(rows(r + cb * span, blk, dil))
                k_rows.append(rows(r + max(cb - 1, 0) * span, nk, dil))
        nb = len(q_rows)
        q = jnp.stack([qf[g, qr, :] for qr in q_rows]).astype(BF16)
        k = jnp.stack([kf[g, kr, :] for kr in k_rows]).astype(BF16)
        v = jnp.stack([vf[g, kr, :] for kr in k_rows]).astype(BF16)
        sc = jnp.einsum("bqd,bkd->bqk", q, k, preferred_element_type=F32) * (scale * LOG2_E)
        first = (lax.broadcasted_iota(jnp.int32, (nb, blk, nk), 0) & (n_sub - 1)) == 0
        dist = (lax.broadcasted_iota(jnp.int32, (nb, blk, nk), 1)
                - lax.broadcasted_iota(jnp.int32, (nb, blk, nk), 2)
                + jnp.where(first, 0, nk - blk))
        sc = jnp.where(dist >= 0, jnp.where(dist <= blk, sc, -jnp.inf), -jnp.inf)
        m = jnp.max(sc, axis=-1, keepdims=True)
        e = jnp.exp2(sc - m)
        den = jnp.sum(e, axis=-1, keepdims=True)
        o = jnp.einsum("bqk,bkd->bqd", e.astype(BF16), v, preferred_element_type=F32) / den
        lg = jnp.broadcast_to(m + jnp.log2(den), (nb, blk, HEAD_DIM))
        for i, qr in enumerate(q_rows):
            og[g, qr, :] = o[i]
            lse[g, qr, :] = lg[i]

    l0, l1, l2 = lse[0], lse[1], lse[2]
    mx = jnp.maximum(jnp.maximum(l0, l1), l2)
    w0, w1, w2 = jnp.exp2(l0 - mx), jnp.exp2(l1 - mx), jnp.exp2(l2 - mx)
    o_ref[...] = ((w0 * og[0] + w1 * og[1] + w2 * og[2]) / (w0 + w1 + w2)).astype(o_ref.dtype)


def _dilated_attention(proj, tables, *, batch, seq):
    c, s, rot = tables
    cb = C_OFF // HEAD_DIM
    nh = N_DIL_GROUPS * N_HEADS

    def col(kind, g):
        return lambda b, h: (b, cb + kind * nh + g * N_HEADS + h)

    in_specs = [pl.BlockSpec((LANES, LANES), lambda b, h: (0, 0)),
                pl.BlockSpec((seq, LANES), lambda b, h: (0, 0)),
                pl.BlockSpec((seq, LANES), lambda b, h: (0, 0))]
    for kind in range(3):
        for g in range(N_DIL_GROUPS):
            in_specs.append(pl.BlockSpec((seq, HEAD_DIM), col(kind, g)))
    return pl.pallas_call(
        functools.partial(_dilated_kernel, seq=seq),
        out_shape=jax.ShapeDtypeStruct((batch * seq, BRANCH_WIDTH), BF16),
        grid=(batch, N_HEADS),
        in_specs=in_specs,
        out_specs=pl.BlockSpec((seq, HEAD_DIM), lambda b, h: (b, h)),
        scratch_shapes=[pltpu.VMEM((N_DIL_GROUPS, seq, HEAD_DIM), F32) for _ in range(5)],
        compiler_params=_cparams(("parallel", "parallel")),
        name="dilated_attention",
    )(rot, c, s, *([proj] * 9))


def _stick_kernel(q_ref, k_ref, v_ref, o_ref, *, tq, hp, scale):
    qi = pl.program_id(2)
    heads = [slice(h * HEAD_DIM, (h + 1) * HEAD_DIM) for h in range(hp)]
    qs = [q_ref[:, hs] for hs in heads]
    r = lax.broadcasted_iota(jnp.int32, (tq, tq), 0)
    c = lax.broadcasted_iota(jnp.int32, (tq, tq), 1)
    later_keys = jnp.where(r > c, 1.0, 0.0).astype(BF16)

    def step(j, carry, diag):
        off = pl.multiple_of(j * tq, tq)
        strict = c < r
        z2 = [_dot_nt(q, k_ref[pl.ds(off, tq), hs]) * (scale * LOG2_E) for q, hs in zip(qs, heads)]
        sp2 = [jnp.maximum(z, 0.0) + jnp.log2(1.0 + jnp.exp2(-jnp.abs(z))) for z in z2]
        log_not = [jnp.where(strict, -sp, 0.0) if diag else -sp for sp in sp2]
        later = []
        for ln, (tail, _) in zip(log_not, carry):
            hi = ln.astype(BF16)
            lo = (ln - hi.astype(F32)).astype(BF16)
            later.append(_dot(hi, later_keys) + _dot(lo, later_keys) + tail)
        a = [jnp.exp2((z - sp) + lt) for z, sp, lt in zip(z2, sp2, later)]
        if diag:
            a = [jnp.where(strict, x, 0.0) for x in a]
        out = []
        for x, ln, hs, (tail, acc) in zip(a, log_not, heads, carry):
            acc = acc + _dot(x.astype(BF16), v_ref[pl.ds(off, tq), hs])
            out.append((tail + jnp.sum(ln, axis=-1, keepdims=True), acc))
        return tuple(out)

    init = (jnp.zeros((tq, 1), F32), jnp.zeros((tq, HEAD_DIM), F32))
    carry = step(qi, tuple(init for _ in heads), True)
    carry = lax.fori_loop(0, qi, lambda t, cr: step(qi - 1 - t, cr, False), carry)
    for hs, (_, acc) in zip(heads, carry):
        o_ref[:, hs] = acc.astype(o_ref.dtype)


def _stick_attention(proj, *, batch, seq, tq=256, hp=4):
    nq = seq // tq
    w = hp * HEAD_DIM
    cb = D_OFF // w
    ng = N_HEADS // hp
    return pl.pallas_call(
        functools.partial(_stick_kernel, tq=tq, hp=hp, scale=1.0 / math.sqrt(HEAD_DIM)),
        out_shape=jax.ShapeDtypeStruct((batch * seq, BRANCH_WIDTH), BF16),
        grid=(batch, ng, nq),
        in_specs=[pl.BlockSpec((tq, w), lambda b, h, i: (b * nq + i, cb + h)),
                  pl.BlockSpec((seq, w), lambda b, h, i: (b, cb + ng + h)),
                  pl.BlockSpec((seq, w), lambda b, h, i: (b, cb + 2 * ng + h))],
        out_specs=pl.BlockSpec((tq, w), lambda b, h, i: (b * nq + i, h)),
        compiler_params=_cparams(("parallel", "parallel", "arbitrary")),
        name="stick_attention",
    )(proj, proj, proj)


def _merge_kernel(ya, yb, yc, yd, g0, g1, g2, g3, wb_ref, o_ref):
    acc = None
    for i, (y, g) in enumerate(((ya, g0), (yb, g1), (yc, g2), (yd, g3))):
        t = g[...].astype(F32) * _dot(y[...], wb_ref[i])
        acc = t if acc is None else acc + t
    o_ref[...] = acc.astype(o_ref.dtype)


def _merge(ys, gates, wb, *, tm=1024, tn=512):
    t = gates.shape[0]
    d = wb.shape[2]
    nn = d // tn
    y_spec = pl.BlockSpec((tm, BRANCH_WIDTH), lambda i, j: (i, 0))
    g_specs = [pl.BlockSpec((tm, tn), functools.partial(lambda i, j, b: (i, b * nn + j), b=b))
               for b in range(N_BRANCH)]
    return pl.pallas_call(
        _merge_kernel,
        out_shape=jax.ShapeDtypeStruct((t, d), BF16),
        grid=(t // tm, nn),
        in_specs=[y_spec] * N_BRANCH + g_specs
                 + [pl.BlockSpec((N_BRANCH, BRANCH_WIDTH, tn), lambda i, j: (0, 0, j))],
        out_specs=pl.BlockSpec((tm, tn), lambda i, j: (i, j)),
        compiler_params=_cparams(("parallel", "arbitrary")),
        name="branch_merge",
    )(*ys, gates, gates, gates, gates, wb)


def _outproj_kernel(x_ref, m_ref, w_ref, o_ref):
    o_ref[...] = x_ref[...] + _dot(m_ref[...], w_ref[...])


def _outproj(x, merged, w, *, tm=1024, tn=512):
    t, d = x.shape
    return pl.pallas_call(
        _outproj_kernel,
        out_shape=jax.ShapeDtypeStruct((t, d), F32),
        grid=(t // tm, d // tn),
        in_specs=[pl.BlockSpec((tm, tn), lambda i, j: (i, j)),
                  pl.BlockSpec((tm, d), lambda i, j: (i, 0)),
                  pl.BlockSpec((d, tn), lambda i, j: (0, j))],
        out_specs=pl.BlockSpec((tm, tn), lambda i, j: (i, j)),
        compiler_params=_cparams(("parallel", "arbitrary")),
        name="out_proj",
    )(x, merged, w)


def _swiglu_step(h, wg, wu, wo):
    g = _dot(h, wg)
    u = _dot(h, wu)
    return _dot((g * _sigmoid(g) * u).astype(BF16), wo)


def _ffn_kernel(x_ref, gain_ref, wg_ref, wu_ref, wo_ref, o_ref, h_ref, acc_ref):
    f = pl.program_id(1)

    @pl.when(f == 0)
    def _():
        h_ref[...] = _rms(x_ref[...], gain_ref[...]).astype(BF16)
        acc_ref[...] = jnp.zeros_like(acc_ref)

    acc_ref[...] += _swiglu_step(h_ref[...], wg_ref[...].astype(BF16),
                                 wu_ref[...].astype(BF16), wo_ref[...].astype(BF16))

    @pl.when(f == pl.num_programs(1) - 1)
    def _():
        o_ref[...] = x_ref[...] + acc_ref[...]


def _dense_ffn(x, gain, w_in, w_out, *, tm=512, tf=512):
    t, d = x.shape
    ff = w_out.shape[0]
    nf = ff // tf
    return pl.pallas_call(
        _ffn_kernel,
        out_shape=jax.ShapeDtypeStruct((t, d), F32),
        grid=(t // tm, nf),
        in_specs=[pl.BlockSpec((tm, d), lambda i, f: (i, 0)),
                  pl.BlockSpec((1, d), lambda i, f: (0, 0)),
                  pl.BlockSpec((d, tf), lambda i, f: (0, f)),
                  pl.BlockSpec((d, tf), lambda i, f: (0, nf + f)),
                  pl.BlockSpec((tf, d), lambda i, f: (f, 0))],
        out_specs=pl.BlockSpec((tm, d), lambda i, f: (i, 0)),
        scratch_shapes=[pltpu.VMEM((tm, d), BF16), pltpu.VMEM((tm, d), F32)],
        compiler_params=_cparams(("parallel", "arbitrary")),
        name="dense_ffn",
    )(x, gain.reshape(1, d), w_in, w_in, w_out)


def _router_kernel(x_ref, gain_ref, wr_ref, idx_ref, gate_ref):
    h = _rms(x_ref[...], gain_ref[...])
    logits = jnp.dot(h, wr_ref[...], preferred_element_type=F32, precision=lax.Precision.HIGHEST)
    lane = lax.broadcasted_iota(jnp.int32, logits.shape, 1)
    lanef = lane.astype(F32)
    lg = jnp.where(lane < N_EXPERTS, logits, -jnp.inf)
    v1 = jnp.max(lg, axis=-1, keepdims=True)
    i1 = jnp.min(jnp.where(lg == v1, lanef, float(LANES)), axis=-1, keepdims=True)
    lg2 = jnp.where(lanef == i1, -jnp.inf, lg)
    v2 = jnp.max(lg2, axis=-1, keepdims=True)
    i2 = jnp.min(jnp.where(lg2 == v2, lanef, float(LANES)), axis=-1, keepdims=True)
    e2 = jnp.exp(v2 - v1)
    g1 = 1.0 / (1.0 + e2)
    g2 = e2 / (1.0 + e2)
    idx_ref[...] = jnp.where(lane == 0, i1, jnp.where(lane == 1, i2, 0.0)).astype(jnp.int32)
    gate_ref[...] = jnp.where(lane == 0, g1, jnp.where(lane == 1, g2, 0.0))


def _router(x, gain, w_router, *, tm=512):
    t, d = x.shape
    wr = jnp.zeros((d, LANES), F32).at[:, :N_EXPERTS].set(w_router)
    return pl.pallas_call(
        _router_kernel,
        out_shape=(jax.ShapeDtypeStruct((t, LANES), jnp.int32),
                   jax.ShapeDtypeStruct((t, LANES), F32)),
        grid=(t // tm,),
        in_specs=[pl.BlockSpec((tm, d), lambda i: (i, 0)),
                  pl.BlockSpec((1, d), lambda i: (0, 0)),
                  pl.BlockSpec((d, LANES), lambda i: (0, 0))],
        out_specs=(pl.BlockSpec((tm, LANES), lambda i: (i, 0)),
                   pl.BlockSpec((tm, LANES), lambda i: (i, 0))),
        compiler_params=_cparams(("parallel",)),
        name="router",
    )(x, gain.reshape(1, d), wr)


def _gather_rows(idx_ref, base, n, src_hbm, dst, sem):
    def start(r, _):
        row = idx_ref[base + r]
        pltpu.make_async_copy(src_hbm.at[pl.ds(row, 1), :], dst.at[pl.ds(r, 1), :], sem).start()
        return 0

    lax.fori_loop(0, n, start, 0, unroll=8)
    pltpu.make_async_copy(src_hbm.at[pl.ds(0, n), :], dst.at[pl.ds(0, n), :], sem).wait()


def _moe_kernel(tile_e_ref, tile_ok_ref, row_tok_ref, x_hbm, gain_ref, rg_ref,
                wg_ref, wu_ref, wo_ref, o_ref, xbuf, h_ref, acc_ref, sem, *, tm):
    del tile_e_ref
    i = pl.program_id(0)
    f = pl.program_id(1)
    ok = tile_ok_ref[i] > 0

    @pl.when(jnp.logical_and(ok, f == 0))
    def _():
        _gather_rows(row_tok_ref, i * tm, tm, x_hbm, xbuf, sem)
        h_ref[...] = _rms(xbuf[...], gain_ref[...]).astype(BF16)
        acc_ref[...] = jnp.zeros_like(acc_ref)

    @pl.when(ok)
    def _():
        acc_ref[...] += _swiglu_step(h_ref[...], wg_ref[...].astype(BF16),
                                     wu_ref[...].astype(BF16), wo_ref[...].astype(BF16))

    @pl.when(f == pl.num_programs(1) - 1)
    def _():
        @pl.when(ok)
        def _():
            o_ref[...] = rg_ref[...] * acc_ref[...]

        @pl.when(jnp.logical_not(ok))
        def _():
            o_ref[...] = jnp.zeros_like(o_ref)


def _moe_experts(x, gain, tile_e, tile_ok, row_tok, row_gate, w_in, w_out, *, tm, tf=512):
    t, d = x.shape
    ff = w_out.shape[1]
    nf = ff // tf
    p = row_tok.shape[0]

    def f_eff(i, f, ok):
        return jnp.where(ok[i] > 0, f, nf - 1)

    grid_spec = pltpu.PrefetchScalarGridSpec(
        num_scalar_prefetch=3,
        grid=(p // tm, nf),
        in_specs=[pl.BlockSpec(memory_space=pl.ANY),
                  pl.BlockSpec((1, d), lambda i, f, e, ok, rt: (0, 0)),
                  pl.BlockSpec((tm, 1), lambda i, f, e, ok, rt: (i, 0)),
                  pl.BlockSpec((None, d, tf), lambda i, f, e, ok, rt: (e[i], 0, f_eff(i, f, ok))),
                  pl.BlockSpec((None, d, tf), lambda i, f, e, ok, rt: (e[i], 0, nf + f_eff(i, f, ok))),
                  pl.BlockSpec((None, tf, d), lambda i, f, e, ok, rt: (e[i], f_eff(i, f, ok), 0))],
        out_specs=pl.BlockSpec((tm, d), lambda i, f, e, ok, rt: (i, 0)),
        scratch_shapes=[pltpu.VMEM((tm, d), F32), pltpu.VMEM((tm, d), BF16),
                        pltpu.VMEM((tm, d), F32), pltpu.SemaphoreType.DMA],
    )
    return pl.pallas_call(
        functools.partial(_moe_kernel, tm=tm),
        out_shape=jax.ShapeDtypeStruct((p, d), F32),
        grid_spec=grid_spec,
        compiler_params=_cparams(("arbitrary", "arbitrary")),
        name="moe_experts",
    )(tile_e, tile_ok, row_tok, x, gain.reshape(1, d), row_gate.reshape(p, 1), w_in, w_in, w_out)


def _combine_kernel(pos_ref, x_ref, y_hbm, gain_ref, o_ref, buf0, buf1, sem0, sem1, *, tm):
    i = pl.program_id(0)
    _gather_rows(pos_ref, i * tm, tm, y_hbm, buf0, sem0)
    _gather_rows(pos_ref, pos_ref.shape[0] // 2 + i * tm, tm, y_hbm, buf1, sem1)
    x = x_ref[...] + (buf0[...] + buf1[...])
    if gain_ref is None:
        o_ref[...] = x
    else:
        o_ref[...] = _rms(x, gain_ref[...])


def _moe_combine(x, y_rows, pos, final_gain, *, tm=256):
    t, d = x.shape
    in_specs = [pl.BlockSpec((tm, d), lambda i, ps: (i, 0)),
                pl.BlockSpec(memory_space=pl.ANY)]
    args = [x, y_rows]
    if final_gain is not None:
        in_specs.append(pl.BlockSpec((1, d), lambda i, ps: (0, 0)))
        args.append(final_gain.reshape(1, d))
        body = functools.partial(_combine_kernel, tm=tm)
    else:
        def body(pos_ref, x_ref, y_hbm, o_ref, *scratch):
            _combine_kernel(pos_ref, x_ref, y_hbm, None, o_ref, *scratch, tm=tm)
    grid_spec = pltpu.PrefetchScalarGridSpec(
        num_scalar_prefetch=1,
        grid=(t // tm,),
        in_specs=in_specs,
        out_specs=pl.BlockSpec((tm, d), lambda i, ps: (i, 0)),
        scratch_shapes=[pltpu.VMEM((tm, d), F32), pltpu.VMEM((tm, d), F32),
                        pltpu.SemaphoreType.DMA, pltpu.SemaphoreType.DMA],
    )
    return pl.pallas_call(
        body,
        out_shape=jax.ShapeDtypeStruct((t, d), F32),
        grid_spec=grid_spec,
        compiler_params=_cparams(("arbitrary",)),
        name="moe_combine",
    )(pos, *args)


def _final_norm_kernel(x_ref, g_ref, o_ref):
    o_ref[...] = _rms(x_ref[...], g_ref[...])


def _final_norm(x, gain, *, tm=512):
    t, d = x.shape
    return pl.pallas_call(
        _final_norm_kernel,
        out_shape=jax.ShapeDtypeStruct((t, d), F32),
        grid=(t // tm,),
        in_specs=[pl.BlockSpec((tm, d), lambda i: (i, 0)), pl.BlockSpec((1, d), lambda i: (0, 0))],
        out_specs=pl.BlockSpec((tm, d), lambda i: (i, 0)),
        compiler_params=_cparams(("parallel",)),
        name="final_norm",
    )(x, gain.reshape(1, d))


def _routing_tables(idx, gates, tm):
    t = idx.shape[0]
    e_flat = idx[:, :TOP_K].T.reshape(-1)
    g_flat = gates[:, :TOP_K].T.reshape(-1)
    tok = jnp.tile(jnp.arange(t, dtype=jnp.int32), TOP_K)
    onehot = (e_flat[:, None] == jnp.arange(N_EXPERTS, dtype=jnp.int32)[None, :]).astype(jnp.int32)
    rank = jnp.sum((jnp.cumsum(onehot, axis=0) - onehot) * onehot, axis=1)
    counts = jnp.sum(onehot, axis=0)
    padded = ((counts + tm - 1) // tm) * tm
    ends = jnp.cumsum(padded)
    starts = ends - padded
    dest = (starts[e_flat] + rank).astype(jnp.int32)
    p = TOP_K * t + N_EXPERTS * tm
    row_tok = jnp.zeros((p,), jnp.int32).at[dest].set(tok)
    row_gate = jnp.zeros((p,), F32).at[dest].set(g_flat)
    tile_start = jnp.arange(p // tm, dtype=jnp.int32) * tm
    tile_ok = (tile_start < ends[-1]).astype(jnp.int32)
    tile_e = jnp.sum((tile_start[:, None] >= ends[None, :]).astype(jnp.int32), axis=1)
    last_e = tile_e[ends[-1] // tm - 1]
    tile_e = jnp.where(tile_ok > 0, tile_e, last_e)
    return tile_e, tile_ok, row_tok, row_gate, dest


def _relayout_w_in(w):
    b_end = 1536 + Q_LORA + KV_LORA + ROPE_DIM_B
    gate0 = w.shape[1] - N_BRANCH * w.shape[0]
    pad = jnp.zeros((w.shape[0], C_OFF - b_end), BF16)
    w_qkv = jnp.concatenate([w[:, :b_end].astype(BF16), pad, w[:, b_end:gate0].astype(BF16)], axis=1)
    return w_qkv, w[:, gate0:].astype(BF16)


def _relayout_mla(w_uq, w_ukv):
    dq = NOPE_DIM + ROPE_DIM_B
    wq = w_uq.reshape(Q_LORA, N_HEADS, dq)
    wq = jnp.concatenate([wq, jnp.zeros((Q_LORA, N_HEADS, MLA_QK - dq), w_uq.dtype)], axis=2)
    wkv = w_ukv.reshape(KV_LORA, N_HEADS, NOPE_DIM + HEAD_DIM)
    wk = wkv[:, :, :NOPE_DIM].reshape(KV_LORA, N_HEADS * NOPE_DIM)
    wv = wkv[:, :, NOPE_DIM:].reshape(KV_LORA, N_HEADS * HEAD_DIM)
    return wq.reshape(Q_LORA, N_HEADS * MLA_QK).astype(BF16), wk.astype(BF16), wv.astype(BF16)


def kernel(x, norm_mix, w_in, dq_norm, dkv_norm, w_uq, w_ukv, lam_q1, lam_k1, lam_q2, lam_k2,
           diff_norm, w_branch, w_out, norm_ffn, w_dense_in, w_dense_out, w_router,
           w_moe_in, w_moe_out, norm_final):
    batch, seq, d = x.shape
    depth = w_in.shape[0]
    t = batch * seq
    xt = x.reshape(t, d)

    rot_a = DIFF_DIM // 4
    tab_a = _rope_tables(seq, ((0, rot_a), (DIFF_DIM, rot_a)))
    tab_b = _rope_tables(seq, ((0, ROPE_DIM_B),))
    tab_c = _rope_tables(seq, ((0, HEAD_DIM // 4),))
    moe_tm = 512

    for layer in range(depth):
        w_qkv, w_gate = _relayout_w_in(w_in[layer])
        proj = _inproj(xt, norm_mix[layer], w_qkv, sigmoid=False)
        gates = _inproj(xt, norm_mix[layer], w_gate, sigmoid=True)

        lam_init = 0.8 - 0.6 * math.exp(-0.3 * layer)
        lam_rows = jnp.stack([lam_q1[layer], lam_k1[layer], lam_q2[layer], lam_k2[layer]])
        y_a = _diff_attention(proj, lam_rows, diff_norm[layer], tab_a,
                              batch=batch, seq=seq, lam_init=lam_init)

        wuq, wuk, wuv = _relayout_mla(w_uq[layer], w_ukv[layer])
        q_b, k_b, v_b = _mla_prep(proj, dq_norm[layer], dkv_norm[layer], wuq, wuk, wuv, tab_b, seq=seq)
        y_b = _mla_attention(q_b, k_b, v_b, batch=batch, seq=seq)

        y_c = _dilated_attention(proj, tab_c, batch=batch, seq=seq)
        y_d = _stick_attention(proj, batch=batch, seq=seq)

        merged = _merge((y_a, y_b, y_c, y_d), gates, w_branch[layer].astype(BF16))
        xt = _outproj(xt, merged, w_out[layer].astype(BF16))

        last = layer == depth - 1
        if layer % 2 == 0:
            xt = _dense_ffn(xt, norm_ffn[layer], w_dense_in[layer // 2].astype(BF16),
                            w_dense_out[layer // 2].astype(BF16))
            if last:
                xt = _final_norm(xt, norm_final)
        else:
            m = layer // 2
            idx, gate = _router(xt, norm_ffn[layer], w_router[m])
            tile_e, tile_ok, row_tok, row_gate, dest = _routing_tables(idx, gate, moe_tm)
            y_rows = _moe_experts(xt, norm_ffn[layer], tile_e, tile_ok, row_tok, row_gate,
                                  w_moe_in[m].astype(BF16), w_moe_out[m].astype(BF16), tm=moe_tm)
            xt = _moe_combine(xt, y_rows, dest, norm_final if last else None)
    return xt.reshape(batch, seq, d)
```

```python
import functools
import math

import numpy as np
import jax
import jax.numpy as jnp
from jax import lax
from jax.experimental import pallas as pl
from jax.experimental.pallas import tpu as pltpu

F32 = jnp.float32
BF16 = jnp.bfloat16

HEAD_DIM = 128
ROPE_THETA = 500000.0
NORM_EPS = 1e-6
N_BRANCH = 4
BRANCH_WIDTH = 512
N_HEADS = 4
DIFF_DIM = 64
Q_LORA = 512
KV_LORA = 512
NOPE_DIM = 128
ROPE_DIM_B = 64
MLA_QK = 256
DIL_PAIRS = ((128, 1), (512, 4), (2048, 16))
N_DIL_GROUPS = 3
N_EXPERTS = 8
TOP_K = 2
LANES = 128
LOG2_E = math.log2(math.e)

A_OFF = 0
B_OFF = 1536
C_OFF = 3072
D_OFF = 7680
QKV_COLS = 9216

VMEM_LIMIT = 56 * 1024 * 1024


def _cparams(sem, vmem=VMEM_LIMIT):
    return pltpu.CompilerParams(dimension_semantics=sem, vmem_limit_bytes=vmem)


def _dot(a, b):
    return jnp.dot(a, b, preferred_element_type=F32)


def _dot_nt(a, b):
    return lax.dot_general(a, b, (((1,), (1,)), ((), ())), preferred_element_type=F32)


def _rms(x, gain):
    return x * lax.rsqrt(jnp.mean(x * x, axis=-1, keepdims=True) + NORM_EPS) * gain


def _sigmoid(x):
    return 1.0 / (1.0 + jnp.exp(-x))


def _rope_tables(seq, segments):
    pos = np.arange(seq, dtype=np.float64)
    c = np.ones((seq, LANES), np.float64)
    s = np.zeros((seq, LANES), np.float64)
    r = np.zeros((LANES, LANES), np.float32)
    for start, rot in segments:
        half = rot // 2
        inv_freq = ROPE_THETA ** (-np.arange(0, rot, 2, dtype=np.float64) / rot)
        ang = pos[:, None] * inv_freq[None, :]
        cos, sin = np.cos(ang), np.sin(ang)
        c[:, start:start + half] = cos
        c[:, start + half:start + rot] = cos
        s[:, start:start + half] = -sin
        s[:, start + half:start + rot] = sin
        for i in range(half):
            r[start + half + i, start + i] = 1.0
            r[start + i, start + half + i] = 1.0
    return jnp.asarray(c, F32), jnp.asarray(s, F32), jnp.asarray(r, BF16)


def _rope(x_bf16, rot, c, s):
    return x_bf16.astype(F32) * c + _dot(x_bf16, rot) * s


def _inproj_kernel(x_ref, g_ref, w_ref, o_ref, h_ref, *, gate_block):
    j = pl.program_id(1)

    @pl.when(j == 0)
    def _():
        h_ref[...] = _rms(x_ref[...], g_ref[...]).astype(BF16)

    acc = _dot(h_ref[...], w_ref[...])
    o_ref[...] = jnp.where(j >= gate_block, _sigmoid(acc), acc).astype(o_ref.dtype)


def _inproj(x, gain, w, *, tm=1024, tn=1024):
    t, d = x.shape
    n = w.shape[1]
    return pl.pallas_call(
        functools.partial(_inproj_kernel, gate_block=QKV_COLS // tn),
        out_shape=jax.ShapeDtypeStruct((t, n), BF16),
        grid=(t // tm, n // tn),
        in_specs=[pl.BlockSpec((tm, d), lambda i, j: (i, 0)),
                  pl.BlockSpec((1, d), lambda i, j: (0, 0)),
                  pl.BlockSpec((d, tn), lambda i, j: (0, j))],
        out_specs=pl.BlockSpec((tm, tn), lambda i, j: (i, j)),
        scratch_shapes=[pltpu.VMEM((tm, d), BF16)],
        compiler_params=_cparams(("parallel", "arbitrary")),
        name="inproj",
    )(x, gain.reshape(1, d), w)


def _osm(scores, values, carries):
    stats = []
    for s, (m, l, _) in zip(scores, carries):
        m_new = jnp.maximum(m, jnp.max(s, axis=-1, keepdims=True))
        alpha = jnp.exp2(m - m_new)
        p = jnp.exp2(s - m_new)
        stats.append((m_new, alpha * l + jnp.sum(p, axis=-1, keepdims=True), alpha, p))
    return tuple((m_new, l, alpha * acc + _dot(p.astype(BF16), v))
                 for (m_new, l, alpha, p), v, (_, _, acc) in zip(stats, values, carries))


def _causal_keep(t):
    r = lax.broadcasted_iota(jnp.int32, (t, t), 0)
    c = lax.broadcasted_iota(jnp.int32, (t, t), 1)
    return c <= r


def _diff_kernel(lam_ref, gain_ref, rot_ref, cq_ref, sq_ref, ck_ref, sk_ref,
                 q_ref, k_ref, v_ref, o_ref, kr_ref, *, tq, hp, lam_init):
    qi = pl.program_id(2)
    rot = rot_ref[...]
    heads = [slice(h * HEAD_DIM, (h + 1) * HEAD_DIM) for h in range(hp)]

    @pl.when(qi == 0)
    def _():
        for hs in heads:
            kr_ref[:, hs] = _rope(k_ref[:, hs], rot, ck_ref[...], sk_ref[...]).astype(BF16)

    scale2 = LOG2_E / math.sqrt(DIFF_DIM)
    lane = lax.broadcasted_iota(jnp.int32, (tq, HEAD_DIM), 1)
    chains = []
    for hs in heads:
        qf = _rope(q_ref[:, hs], rot, cq_ref[...], sq_ref[...])
        chains.append((jnp.where(lane < DIFF_DIM, qf, 0.0).astype(BF16), hs))
        chains.append((jnp.where(lane >= DIFF_DIM, qf, 0.0).astype(BF16), hs))

    def step(j, carry, masked):
        off = pl.multiple_of(j * tq, tq)
        scores = [_dot_nt(q, kr_ref[pl.ds(off, tq), hs]) * scale2 for q, hs in chains]
        if masked:
            keep = _causal_keep(tq)
            scores = [jnp.where(keep, s, -jnp.inf) for s in scores]
        return _osm(scores, [v_ref[pl.ds(off, tq), hs] for _, hs in chains], carry)

    init = (jnp.full((tq, 1), -jnp.inf, F32), jnp.zeros((tq, 1), F32),
            jnp.zeros((tq, HEAD_DIM), F32))
    carry = lax.fori_loop(0, qi, lambda j, c: step(j, c, False), tuple(init for _ in chains))
    carry = step(qi, carry, True)

    lam_rows = lam_ref[...]
    lam = (jnp.exp(jnp.sum(lam_rows[0:1] * lam_rows[1:2], axis=-1, keepdims=True))
           - jnp.exp(jnp.sum(lam_rows[2:3] * lam_rows[3:4], axis=-1, keepdims=True))
           + lam_init)
    for i, hs in enumerate(heads):
        (_, l1, a1), (_, l2, a2) = carry[2 * i], carry[2 * i + 1]
        out = a1 / l1 - lam * (a2 / l2)
        o_ref[:, hs] = (_rms(out, gain_ref[...]) * (1.0 - lam_init)).astype(o_ref.dtype)


def _diff_attention(proj, lam_rows, gain, tables, *, batch, seq, lam_init, tq=256, hp=4):
    c, s, rot = tables
    nq = seq // tq
    w = hp * HEAD_DIM
    cb = A_OFF // w
    ng = N_HEADS // hp
    return pl.pallas_call(
        functools.partial(_diff_kernel, tq=tq, hp=hp, lam_init=lam_init),
        out_shape=jax.ShapeDtypeStruct((batch * seq, BRANCH_WIDTH), BF16),
        grid=(batch, ng, nq),
        in_specs=[pl.BlockSpec((4, DIFF_DIM), lambda b, h, i: (0, 0)),
                  pl.BlockSpec((1, HEAD_DIM), lambda b, h, i: (0, 0)),
                  pl.BlockSpec((LANES, LANES), lambda b, h, i: (0, 0)),
                  pl.BlockSpec((tq, LANES), lambda b, h, i: (i, 0)),
                  pl.BlockSpec((tq, LANES), lambda b, h, i: (i, 0)),
                  pl.BlockSpec((seq, LANES), lambda b, h, i: (0, 0)),
                  pl.BlockSpec((seq, LANES), lambda b, h, i: (0, 0)),
                  pl.BlockSpec((tq, w), lambda b, h, i: (b * nq + i, cb + h)),
                  pl.BlockSpec((seq, w), lambda b, h, i: (b, cb + ng + h)),
                  pl.BlockSpec((seq, w), lambda b, h, i: (b, cb + 2 * ng + h))],
        out_specs=pl.BlockSpec((tq, w), lambda b, h, i: (b * nq + i, h)),
        scratch_shapes=[pltpu.VMEM((seq, w), BF16)],
        compiler_params=_cparams(("parallel", "parallel", "arbitrary")),
        name="diff_attention",
    )(lam_rows, gain.reshape(1, HEAD_DIM), rot, c, s, c, s, proj, proj, proj)


def _mla_prep_kernel(cq_ref, ckv_ref, kr_ref, dqn_ref, dkvn_ref, wuq_ref, wuk_ref, wuv_ref,
                     rot_ref, c_ref, s_ref, q_out, k_out, v_out):
    rot = rot_ref[...]
    c = c_ref[...]
    s = s_ref[...]
    hq = _rms(cq_ref[...].astype(F32), dqn_ref[...]).astype(BF16)
    hkv = _rms(ckv_ref[...].astype(F32), dkvn_ref[...]).astype(BF16)
    q = _dot(hq, wuq_ref[...])
    kn = _dot(hkv, wuk_ref[...])
    v_out[...] = _dot(hkv, wuv_ref[...]).astype(BF16)
    k_rope = _rope(kr_ref[...], rot, c, s).astype(BF16)
    for h in range(N_HEADS):
        lo = h * MLA_QK
        q_out[:, lo:lo + NOPE_DIM] = q[:, lo:lo + NOPE_DIM].astype(BF16)
        q_out[:, lo + NOPE_DIM:lo + MLA_QK] = _rope(
            q[:, lo + NOPE_DIM:lo + MLA_QK].astype(BF16), rot, c, s).astype(BF16)
        k_out[:, lo:lo + NOPE_DIM] = kn[:, h * NOPE_DIM:(h + 1) * NOPE_DIM].astype(BF16)
        k_out[:, lo + NOPE_DIM:lo + MLA_QK] = k_rope


def _mla_prep(proj, dq_norm, dkv_norm, wuq, wuk, wuv, tables, *, seq, tm=512):
    c, s, rot = tables
    t = proj.shape[0]
    nb = seq // tm
    cb = B_OFF // Q_LORA
    full = lambda shape: pl.BlockSpec(shape, lambda i: (0, 0))
    return pl.pallas_call(
        _mla_prep_kernel,
        out_shape=(jax.ShapeDtypeStruct((t, N_HEADS * MLA_QK), BF16),
                   jax.ShapeDtypeStruct((t, N_HEADS * MLA_QK), BF16),
                   jax.ShapeDtypeStruct((t, N_HEADS * HEAD_DIM), BF16)),
        grid=(t // tm,),
        in_specs=[pl.BlockSpec((tm, Q_LORA), lambda i: (i, cb)),
                  pl.BlockSpec((tm, KV_LORA), lambda i: (i, cb + 1)),
                  pl.BlockSpec((tm, LANES), lambda i: (i, (B_OFF + Q_LORA + KV_LORA) // LANES)),
                  full((1, Q_LORA)), full((1, KV_LORA)),
                  full(wuq.shape), full(wuk.shape), full(wuv.shape),
                  full((LANES, LANES)),
                  pl.BlockSpec((tm, LANES), lambda i: (i % nb, 0)),
                  pl.BlockSpec((tm, LANES), lambda i: (i % nb, 0))],
        out_specs=(pl.BlockSpec((tm, N_HEADS * MLA_QK), lambda i: (i, 0)),
                   pl.BlockSpec((tm, N_HEADS * MLA_QK), lambda i: (i, 0)),
                   pl.BlockSpec((tm, N_HEADS * HEAD_DIM), lambda i: (i, 0))),
        compiler_params=_cparams(("parallel",)),
        name="mla_prep",
    )(proj, proj, proj, dq_norm.reshape(1, Q_LORA), dkv_norm.reshape(1, KV_LORA),
      wuq, wuk, wuv, rot, c, s)


def _flash_kernel(q_ref, k_ref, v_ref, o_ref, *, tq, hp, dqk, scale):
    qi = pl.program_id(2)
    qk = [slice(h * dqk, (h + 1) * dqk) for h in range(hp)]
    hv = [slice(h * HEAD_DIM, (h + 1) * HEAD_DIM) for h in range(hp)]
    qs = [q_ref[:, sl] for sl in qk]

    def step(j, carry, masked):
        off = pl.multiple_of(j * tq, tq)
        scores = [_dot_nt(q, k_ref[pl.ds(off, tq), ks]) * (scale * LOG2_E) for q, ks in zip(qs, qk)]
        if masked:
            keep = _causal_keep(tq)
            scores = [jnp.where(keep, s, -jnp.inf) for s in scores]
        return _osm(scores, [v_ref[pl.ds(off, tq), vs] for vs in hv], carry)

    init = (jnp.full((tq, 1), -jnp.inf, F32), jnp.zeros((tq, 1), F32),
            jnp.zeros((tq, HEAD_DIM), F32))
    carry = lax.fori_loop(0, qi, lambda j, c: step(j, c, False), tuple(init for _ in qs))
    carry = step(qi, carry, True)
    for vs, (_, l, acc) in zip(hv, carry):
        o_ref[:, vs] = (acc / l).astype(o_ref.dtype)


def _mla_attention(q, k, v, *, batch, seq, tq=256, hp=4):
    nq = seq // tq
    ng = N_HEADS // hp
    return pl.pallas_call(
        functools.partial(_flash_kernel, tq=tq, hp=hp, dqk=MLA_QK,
                          scale=1.0 / math.sqrt(NOPE_DIM + ROPE_DIM_B)),
        out_shape=jax.ShapeDtypeStruct((batch * seq, BRANCH_WIDTH), BF16),
        grid=(batch, ng, nq),
        in_specs=[pl.BlockSpec((tq, hp * MLA_QK), lambda b, h, i: (b * nq + i, h)),
                  pl.BlockSpec((seq, hp * MLA_QK), lambda b, h, i: (b, h)),
                  pl.BlockSpec((seq, hp * HEAD_DIM), lambda b, h, i: (b, h))],
        out_specs=pl.BlockSpec((tq, hp * HEAD_DIM), lambda b, h, i: (b * nq + i, h)),
        compiler_params=_cparams(("parallel", "parallel", "arbitrary")),
        name="mla_attention",
    )(q, k, v)


DIL_BLOCK = 128


def _dilated_kernel(rot_ref, c_ref, s_ref, *refs, seq):
    in_refs = refs[:9]
    o_ref = refs[9]
    qf, kf, vf, og, lse = refs[10:]
    rot = rot_ref[...]
    c = c_ref[...]
    s = s_ref[...]
    for g in range(N_DIL_GROUPS):
        qf[g] = _rope(in_refs[g][...], rot, c, s)
        kf[g] = _rope(in_refs[3 + g][...], rot, c, s)
        vf[g] = in_refs[6 + g][...].astype(F32)
    scale = 1.0 / math.sqrt(HEAD_DIM)
    blk = DIL_BLOCK

    def rows(start, size, stride):
        return pl.ds(start, size) if stride == 1 else pl.ds(start, size, stride=stride)

    for g, (window, dil) in enumerate(DIL_PAIRS):
        assert window == blk * dil
        span = blk * dil
        n_sub = seq // span
        nk = 2 * blk if n_sub > 1 else blk
        assert n_sub & (n_sub - 1) == 0
        q_rows, k_rows = [], []
        for r in range(dil):
            for cb in range(n_sub):
                q_rows.append(rows(r + cb * span, blk, dil))
                k_rows.append(rows(r + max(cb - 1, 0) * span, nk, dil))
        nb = len(q_rows)
        q = jnp.stack([qf[g, qr, :] for qr in q_rows]).astype(BF16)
        k = jnp.stack([kf[g, kr, :] for kr in k_rows]).astype(BF16)
        v = jnp.stack([vf[g, kr, :] for kr in k_rows]).astype(BF16)
        sc = jnp.einsum("bqd,bkd->bqk", q, k, preferred_element_type=F32) * (scale * LOG2_E)
        first = (lax.broadcasted_iota(jnp.int32, (nb, blk, nk), 0) & (n_sub - 1)) == 0
        dist = (lax.broadcasted_iota(jnp.int32, (nb, blk, nk), 1)
                - lax.broadcasted_iota(jnp.int32, (nb, blk, nk), 2)
                + jnp.where(first, 0, nk - blk))
        sc = jnp.where(dist >= 0, jnp.where(dist <= blk, sc, -jnp.inf), -jnp.inf)
        m = jnp.max(sc, axis=-1, keepdims=True)
        e = jnp.exp2(sc - m)
        den = jnp.sum(e, axis=-1, keepdims=True)
        o = jnp.einsum("bqk,bkd->bqd", e.astype(BF16), v, preferred_element_type=F32) / den
        lg = jnp.broadcast_to(m + jnp.log2(den), (nb, blk, HEAD_DIM))
        for i, qr in enumerate(q_rows):
            og[g, qr, :] = o[i]
            lse[g, qr, :] = lg[i]

    l0, l1, l2 = lse[0], lse[1], lse[2]
    mx = jnp.maximum(jnp.maximum(l0, l1), l2)
    w0, w1, w2 = jnp.exp2(l0 - mx), jnp.exp2(l1 - mx), jnp.exp2(l2 - mx)
    o_ref[...] = ((w0 * og[0] + w1 * og[1] + w2 * og[2]) / (w0 + w1 + w2)).astype(o_ref.dtype)


def _dilated_attention(proj, tables, *, batch, seq):
    c, s, rot = tables
    cb = C_OFF // HEAD_DIM
    nh = N_DIL_GROUPS * N_HEADS

    def col(kind, g):
        return lambda b, h: (b, cb + kind * nh + g * N_HEADS + h)

    in_specs = [pl.BlockSpec((LANES, LANES), lambda b, h: (0, 0)),
                pl.BlockSpec((seq, LANES), lambda b, h: (0, 0)),
                pl.BlockSpec((seq, LANES), lambda b, h: (0, 0))]
    for kind in range(3):
        for g in range(N_DIL_GROUPS):
            in_specs.append(pl.BlockSpec((seq, HEAD_DIM), col(kind, g)))
    return pl.pallas_call(
        functools.partial(_dilated_kernel, seq=seq),
        out_shape=jax.ShapeDtypeStruct((batch * seq, BRANCH_WIDTH), BF16),
        grid=(batch, N_HEADS),
        in_specs=in_specs,
        out_specs=pl.BlockSpec((seq, HEAD_DIM), lambda b, h: (b, h)),
        scratch_shapes=[pltpu.VMEM((N_DIL_GROUPS, seq, HEAD_DIM), F32) for _ in range(5)],
        compiler_params=_cparams(("parallel", "parallel")),
        name="dilated_attention",
    )(rot, c, s, *([proj] * 9))


def _stick_kernel(q_ref, k_ref, v_ref, o_ref, *, tq, hp, scale):
    qi = pl.program_id(2)
    heads = [slice(h * HEAD_DIM, (h + 1) * HEAD_DIM) for h in range(hp)]
    qs = [q_ref[:, hs] for hs in heads]
    r = lax.broadcasted_iota(jnp.int32, (tq, tq), 0)
    c = lax.broadcasted_iota(jnp.int32, (tq, tq), 1)
    later_keys = jnp.where(r > c, 1.0, 0.0).astype(BF16)

    def step(j, carry, diag):
        off = pl.multiple_of(j * tq, tq)
        strict = c < r
        z2 = [_dot_nt(q, k_ref[pl.ds(off, tq), hs]) * (scale * LOG2_E) for q, hs in zip(qs, heads)]
        sp2 = [jnp.maximum(z, 0.0) + jnp.log2(1.0 + jnp.exp2(-jnp.abs(z))) for z in z2]
        log_not = [jnp.where(strict, -sp, 0.0) if diag else -sp for sp in sp2]
        later = []
        for ln, (tail, _) in zip(log_not, carry):
            hi = ln.astype(BF16)
            lo = (ln - hi.astype(F32)).astype(BF16)
            later.append(_dot(hi, later_keys) + _dot(lo, later_keys) + tail)
        a = [jnp.exp2((z - sp) + lt) for z, sp, lt in zip(z2, sp2, later)]
        if diag:
            a = [jnp.where(strict, x, 0.0) for x in a]
        out = []
        for x, ln, hs, (tail, acc) in zip(a, log_not, heads, carry):
            acc = acc + _dot(x.astype(BF16), v_ref[pl.ds(off, tq), hs])
            out.append((tail + jnp.sum(ln, axis=-1, keepdims=True), acc))
        return tuple(out)

    init = (jnp.zeros((tq, 1), F32), jnp.zeros((tq, HEAD_DIM), F32))
    carry = step(qi, tuple(init for _ in heads), True)
    carry = lax.fori_loop(0, qi, lambda t, cr: step(qi - 1 - t, cr, False), carry)
    for hs, (_, acc) in zip(heads, carry):
        o_ref[:, hs] = acc.astype(o_ref.dtype)


def _stick_attention(proj, *, batch, seq, tq=256, hp=4):
    nq = seq // tq
    w = hp * HEAD_DIM
    cb = D_OFF // w
    ng = N_HEADS // hp
    return pl.pallas_call(
        functools.partial(_stick_kernel, tq=tq, hp=hp, scale=1.0 / math.sqrt(HEAD_DIM)),
        out_shape=jax.ShapeDtypeStruct((batch * seq, BRANCH_WIDTH), BF16),
        grid=(batch, ng, nq),
        in_specs=[pl.BlockSpec((tq, w), lambda b, h, i: (b * nq + i, cb + h)),
                  pl.BlockSpec((seq, w), lambda b, h, i: (b, cb + ng + h)),
                  pl.BlockSpec((seq, w), lambda b, h, i: (b, cb + 2 * ng + h))],
        out_specs=pl.BlockSpec((tq, w), lambda b, h, i: (b * nq + i, h)),
        compiler_params=_cparams(("parallel", "parallel", "arbitrary")),
        name="stick_attention",
    )(proj, proj, proj)


def _merge_kernel(ya, yb, yc, yd, g0, g1, g2, g3, wb_ref, o_ref):
    acc = None
    for i, (y, g) in enumerate(((ya, g0), (yb, g1), (yc, g2), (yd, g3))):
        t = g[...].astype(F32) * _dot(y[...], wb_ref[i])
        acc = t if acc is None else acc + t
    o_ref[...] = acc.astype(o_ref.dtype)


def _merge(ys, proj, wb, *, tm=1024, tn=1024):
    t = proj.shape[0]
    d = wb.shape[2]
    nn = d // tn
    g0 = QKV_COLS // tn
    y_spec = pl.BlockSpec((tm, BRANCH_WIDTH), lambda i, j: (i, 0))
    g_specs = [pl.BlockSpec((tm, tn), functools.partial(lambda i, j, b: (i, g0 + b * nn + j), b=b))
               for b in range(N_BRANCH)]
    return pl.pallas_call(
        _merge_kernel,
        out_shape=jax.ShapeDtypeStruct((t, d), BF16),
        grid=(t // tm, nn),
        in_specs=[y_spec] * N_BRANCH + g_specs
                 + [pl.BlockSpec((N_BRANCH, BRANCH_WIDTH, tn), lambda i, j: (0, 0, j))],
        out_specs=pl.BlockSpec((tm, tn), lambda i, j: (i, j)),
        compiler_params=_cparams(("parallel", "arbitrary")),
        name="branch_merge",
    )(*ys, proj, proj, proj, proj, wb)


def _outproj_kernel(x_ref, m_ref, w_ref, o_ref):
    o_ref[...] = x_ref[...] + _dot(m_ref[...], w_ref[...])


def _outproj(x, merged, w, *, tm=1024, tn=1024):
    t, d = x.shape
    return pl.pallas_call(
        _outproj_kernel,
        out_shape=jax.ShapeDtypeStruct((t, d), F32),
        grid=(t // tm, d // tn),
        in_specs=[pl.BlockSpec((tm, tn), lambda i, j: (i, j)),
                  pl.BlockSpec((tm, d), lambda i, j: (i, 0)),
                  pl.BlockSpec((d, tn), lambda i, j: (0, j))],
        out_specs=pl.BlockSpec((tm, tn), lambda i, j: (i, j)),
        compiler_params=_cparams(("parallel", "arbitrary")),
        name="out_proj",
    )(x, merged, w)


def _swiglu_step(h, wg, wu, wo):
    g = _dot(h, wg)
    u = _dot(h, wu)
    return _dot((g * _sigmoid(g) * u).astype(BF16), wo)


def _router_kernel(x_ref, gain_ref, wr_ref, idx_ref, gate_ref):
    h = _rms(x_ref[...], gain_ref[...])
    logits = jnp.dot(h, wr_ref[...], preferred_element_type=F32, precision=lax.Precision.HIGHEST)
    lane = lax.broadcasted_iota(jnp.int32, logits.shape, 1)
    lanef = lane.astype(F32)
    lg = jnp.where(lane < N_EXPERTS, logits, -jnp.inf)
    v1 = jnp.max(lg, axis=-1, keepdims=True)
    i1 = jnp.min(jnp.where(lg == v1, lanef, float(LANES)), axis=-1, keepdims=True)
    lg2 = jnp.where(lanef == i1, -jnp.inf, lg)
    v2 = jnp.max(lg2, axis=-1, keepdims=True)
    i2 = jnp.min(jnp.where(lg2 == v2, lanef, float(LANES)), axis=-1, keepdims=True)
    e2 = jnp.exp(v2 - v1)
    g1 = 1.0 / (1.0 + e2)
    g2 = e2 / (1.0 + e2)
    idx_ref[...] = jnp.where(lane == 0, i1, jnp.where(lane == 1, i2, 0.0)).astype(jnp.int32)
    gate_ref[...] = jnp.where(lane == 0, g1, jnp.where(lane == 1, g2, 0.0))


def _router(x, gain, w_router, *, tm=512):
    t, d = x.shape
    wr = jnp.zeros((d, LANES), F32).at[:, :N_EXPERTS].set(w_router)
    return pl.pallas_call(
        _router_kernel,
        out_shape=(jax.ShapeDtypeStruct((t, LANES), jnp.int32),
                   jax.ShapeDtypeStruct((t, LANES), F32)),
        grid=(t // tm,),
        in_specs=[pl.BlockSpec((tm, d), lambda i: (i, 0)),
                  pl.BlockSpec((1, d), lambda i: (0, 0)),
                  pl.BlockSpec((d, LANES), lambda i: (0, 0))],
        out_specs=(pl.BlockSpec((tm, LANES), lambda i: (i, 0)),
                   pl.BlockSpec((tm, LANES), lambda i: (i, 0))),
        compiler_params=_cparams(("parallel",)),
        name="router",
    )(x, gain.reshape(1, d), wr)


def _start_row_gather(idx_ref, base, n, src_hbm, dst, sem):
    def start(r, _):
        row = idx_ref[base + r]
        pltpu.make_async_copy(src_hbm.at[pl.ds(row, 1), :], dst.at[pl.ds(r, 1), :], sem).start()
        return 0

    lax.fori_loop(0, n, start, 0, unroll=8)


def _wait_row_gather(n, src_hbm, dst, sem):
    pltpu.make_async_copy(src_hbm.at[pl.ds(0, n), :], dst.at[pl.ds(0, n), :], sem).wait()


MOE_CHUNK = 256
MOE_VISIT_CHUNKS = 5


def _moe_kernel(vis_e_ref, vis_ok_ref, vis_row_ref, vis_nch_ref, tail_ref, row_tok_ref,
                x_hbm, gain_ref, wg_ref, wu_ref, wo_ref, y_hbm,
                stage, h_ref, acc_ref, gsem, osem, *, ch, dense):
    del vis_e_ref, vis_ok_ref
    v = pl.program_id(0)
    f = pl.program_id(1)
    nv = pl.num_programs(0)
    nf = pl.num_programs(1)
    nch = vis_nch_ref[v]
    row0 = vis_row_ref[v]
    active = nch > 0

    def rows(c):
        return pl.ds(pl.multiple_of(c * ch, ch), ch)

    def step(h):
        return _swiglu_step(h, wg_ref[...].astype(BF16), wu_ref[...].astype(BF16),
                            wo_ref[...].astype(BF16))

    def out_copy(c):
        dst = y_hbm.at[pl.ds(pl.multiple_of(row0 + c * ch, ch), ch), :]
        return pltpu.make_async_copy(acc_ref.at[rows(c), :], dst, osem)

    @pl.when(jnp.logical_and(active, f == 0))
    def _():
        def fetch(c, slot):
            if dense:
                src = x_hbm.at[pl.ds(pl.multiple_of(row0 + c * ch, ch), ch), :]
                pltpu.make_async_copy(src, stage.at[slot], gsem.at[slot]).start()
            else:
                _start_row_gather(row_tok_ref, row0 + c * ch, ch, x_hbm, stage.at[slot], gsem.at[slot])

        fetch(0, 0)

        def body(c, _):
            slot = lax.rem(c, 2)

            @pl.when(c + 1 < nch)
            def _():
                fetch(c + 1, 1 - slot)

            _wait_row_gather(ch, x_hbm, stage.at[slot], gsem.at[slot])
            x = stage[slot]
            h = _rms(x, gain_ref[...]).astype(BF16)
            h_ref[rows(c), :] = h
            acc_ref[rows(c), :] = x + step(h) if dense else step(h)
            return 0

        lax.fori_loop(0, nch, body, 0)

    @pl.when(jnp.logical_and(active, jnp.logical_and(f > 0, f < nf - 1)))
    def _():
        def body(c, _):
            acc_ref[rows(c), :] += step(h_ref[rows(c), :])
            return 0

        lax.fori_loop(0, nch, body, 0)

    @pl.when(jnp.logical_and(active, f == nf - 1))
    def _():
        def body(c, _):
            acc_ref[rows(c), :] += step(h_ref[rows(c), :])
            out_copy(c).start()
            return 0

        def drain(c, _):
            out_copy(c).wait()
            return 0

        lax.fori_loop(0, nch, body, 0)
        lax.fori_loop(0, nch, drain, 0)

    @pl.when(jnp.logical_and(v == nv - 1, f == nf - 1))
    def _():
        first = tail_ref[0]
        n_tail = y_hbm.shape[0] // ch - first
        stage[0] = jnp.zeros(stage.shape[1:], stage.dtype)

        def tail_copy(c):
            dst = y_hbm.at[pl.ds(pl.multiple_of((first + c) * ch, ch), ch), :]
            return pltpu.make_async_copy(stage.at[0], dst, osem)

        def fill(c, _):
            tail_copy(c).start()
            return 0

        def drain(c, _):
            tail_copy(c).wait()
            return 0

        lax.fori_loop(0, n_tail, fill, 0)
        lax.fori_loop(0, n_tail, drain, 0)


def _dense_visits(t, ch, max_chunks):
    span = ch * max_chunks
    row0 = np.arange(0, t, span, dtype=np.int32)
    n_ch = np.minimum(max_chunks, (t - row0) // ch).astype(np.int32)
    zeros = np.zeros_like(row0)
    return tuple(jnp.asarray(a) for a in (zeros, zeros + 1, row0, n_ch,
                                          np.array([t // ch], np.int32), np.zeros((1,), np.int32)))


def _swiglu_rows(x, gain, tables, w_in, w_out, *, dense, tf=512):
    vis_e, vis_ok, vis_row, vis_nch, tail, row_tok = tables
    t, d = x.shape
    ff = w_out.shape[1]
    nf = ff // tf
    p = t if dense else row_tok.shape[0]
    ch = MOE_CHUNK

    def wspec(shape, index):
        def index_map(v, f, e, ok, *_):
            return index(e[v], jnp.where(ok[v] > 0, f, nf - 1))
        return pl.BlockSpec(shape, index_map)

    grid_spec = pltpu.PrefetchScalarGridSpec(
        num_scalar_prefetch=6,
        grid=(vis_e.shape[0], nf),
        in_specs=[pl.BlockSpec(memory_space=pl.ANY),
                  pl.BlockSpec((1, d), lambda v, f, *_: (0, 0)),
                  wspec((None, d, tf), lambda e, f: (e, 0, f)),
                  wspec((None, d, tf), lambda e, f: (e, 0, nf + f)),
                  wspec((None, tf, d), lambda e, f: (e, f, 0))],
        out_specs=pl.BlockSpec(memory_space=pl.ANY),
        scratch_shapes=[pltpu.VMEM((2, ch, d), F32),
                        pltpu.VMEM((MOE_VISIT_CHUNKS * ch, d), BF16),
                        pltpu.VMEM((MOE_VISIT_CHUNKS * ch, d), F32),
                        pltpu.SemaphoreType.DMA((2,)), pltpu.SemaphoreType.DMA],
    )
    return pl.pallas_call(
        functools.partial(_moe_kernel, ch=ch, dense=dense),
        out_shape=jax.ShapeDtypeStruct((p, d), F32),
        grid_spec=grid_spec,
        compiler_params=_cparams(("arbitrary", "arbitrary"), vmem=60 * 1024 * 1024),
        name="dense_ffn" if dense else "moe_experts",
    )(vis_e, vis_ok, vis_row, vis_nch, tail, row_tok, x, gain.reshape(1, d), w_in, w_in, w_out)


def _combine_kernel(pos_ref, x_ref, gate_ref, y_hbm, gain_ref, o_ref, buf, sem, *, tm):
    i = pl.program_id(0)
    n = pl.num_programs(0)
    t = pos_ref.shape[0] // TOP_K
    slot = lax.rem(i, 2)

    def start(step, sl):
        for k in range(TOP_K):
            _start_row_gather(pos_ref, k * t + step * tm, tm, y_hbm, buf.at[sl, k], sem.at[sl, k])

    @pl.when(i == 0)
    def _():
        start(0, 0)

    @pl.when(i + 1 < n)
    def _():
        start(i + 1, 1 - slot)

    for k in range(TOP_K):
        _wait_row_gather(tm, y_hbm, buf.at[slot, k], sem.at[slot, k])
    g = gate_ref[...]
    x = x_ref[...] + (g[:, 0:1] * buf[slot, 0] + g[:, 1:2] * buf[slot, 1])
    if gain_ref is None:
        o_ref[...] = x
    else:
        o_ref[...] = _rms(x, gain_ref[...])


def _moe_combine(x, gate, y_rows, pos, final_gain, *, tm=256):
    t, d = x.shape
    in_specs = [pl.BlockSpec((tm, d), lambda i, ps: (i, 0)),
                pl.BlockSpec((tm, LANES), lambda i, ps: (i, 0)),
                pl.BlockSpec(memory_space=pl.ANY)]
    args = [x, gate, y_rows]
    if final_gain is not None:
        in_specs.append(pl.BlockSpec((1, d), lambda i, ps: (0, 0)))
        args.append(final_gain.reshape(1, d))
        body = functools.partial(_combine_kernel, tm=tm)
    else:
        def body(pos_ref, x_ref, gate_ref, y_hbm, o_ref, *scratch):
            _combine_kernel(pos_ref, x_ref, gate_ref, y_hbm, None, o_ref, *scratch, tm=tm)
    grid_spec = pltpu.PrefetchScalarGridSpec(
        num_scalar_prefetch=1,
        grid=(t // tm,),
        in_specs=in_specs,
        out_specs=pl.BlockSpec((tm, d), lambda i, ps: (i, 0)),
        scratch_shapes=[pltpu.VMEM((2, TOP_K, tm, d), F32), pltpu.SemaphoreType.DMA((2, TOP_K))],
    )
    return pl.pallas_call(
        body,
        out_shape=jax.ShapeDtypeStruct((t, d), F32),
        grid_spec=grid_spec,
        compiler_params=_cparams(("arbitrary",)),
        name="moe_combine",
    )(pos, *args)


def _final_norm_kernel(x_ref, g_ref, o_ref):
    o_ref[...] = _rms(x_ref[...], g_ref[...])


def _final_norm(x, gain, *, tm=512):
    t, d = x.shape
    return pl.pallas_call(
        _final_norm_kernel,
        out_shape=jax.ShapeDtypeStruct((t, d), F32),
        grid=(t // tm,),
        in_specs=[pl.BlockSpec((tm, d), lambda i: (i, 0)), pl.BlockSpec((1, d), lambda i: (0, 0))],
        out_specs=pl.BlockSpec((tm, d), lambda i: (i, 0)),
        compiler_params=_cparams(("parallel",)),
        name="final_norm",
    )(x, gain.reshape(1, d))


def _routing_tables(idx, ch, max_chunks):
    t = idx.shape[0]
    e_flat = idx[:, :TOP_K].T.reshape(-1)
    tok = jnp.tile(jnp.arange(t, dtype=jnp.int32), TOP_K)
    experts = jnp.arange(N_EXPERTS, dtype=jnp.int32)
    onehot = (e_flat[:, None] == experts[None, :]).astype(jnp.int32)
    rank = jnp.sum((jnp.cumsum(onehot, axis=0) - onehot) * onehot, axis=1)
    counts = jnp.sum(onehot, axis=0)
    n_chunk = (counts + ch - 1) // ch
    ends = jnp.cumsum(n_chunk * ch)
    starts = ends - n_chunk * ch
    dest = (starts[e_flat] + rank).astype(jnp.int32)
    p = TOP_K * t + N_EXPERTS * ch
    row_tok = jnp.zeros((p,), jnp.int32).at[dest].set(tok)

    n_vis = (n_chunk + max_chunks - 1) // max_chunks
    per_vis = (n_chunk + jnp.maximum(n_vis, 1) - 1) // jnp.maximum(n_vis, 1)
    v_end = jnp.cumsum(n_vis)
    v_start = v_end - n_vis
    nv_max = -(-p // (ch * max_chunks)) + N_EXPERTS
    slot = jnp.arange(nv_max, dtype=jnp.int32)
    ok = slot < v_end[-1]
    e_of = jnp.minimum(jnp.sum((slot[:, None] >= v_end[None, :]).astype(jnp.int32), axis=1),
                       N_EXPERTS - 1)
    e_of = jnp.where(ok, e_of, e_of[v_end[-1] - 1])
    k = slot - v_start[e_of]
    n_ch = jnp.where(ok, jnp.clip(n_chunk[e_of] - k * per_vis[e_of], 0, per_vis[e_of]), 0)
    row0 = jnp.where(ok, starts[e_of] + k * per_vis[e_of] * ch, 0)
    tail = (ends[-1:] // ch).astype(jnp.int32)
    tables = (e_of.astype(jnp.int32), ok.astype(jnp.int32), row0.astype(jnp.int32),
              n_ch.astype(jnp.int32), tail, row_tok)
    return tables, dest


def _relayout_w_in(w):
    b_end = 1536 + Q_LORA + KV_LORA + ROPE_DIM_B
    pad = jnp.zeros((w.shape[0], C_OFF - b_end), BF16)
    return jnp.concatenate([w[:, :b_end].astype(BF16), pad, w[:, b_end:].astype(BF16)], axis=1)


def _relayout_mla(w_uq, w_ukv):
    dq = NOPE_DIM + ROPE_DIM_B
    wq = w_uq.reshape(Q_LORA, N_HEADS, dq)
    wq = jnp.concatenate([wq, jnp.zeros((Q_LORA, N_HEADS, MLA_QK - dq), w_uq.dtype)], axis=2)
    wkv = w_ukv.reshape(KV_LORA, N_HEADS, NOPE_DIM + HEAD_DIM)
    wk = wkv[:, :, :NOPE_DIM].reshape(KV_LORA, N_HEADS * NOPE_DIM)
    wv = wkv[:, :, NOPE_DIM:].reshape(KV_LORA, N_HEADS * HEAD_DIM)
    return wq.reshape(Q_LORA, N_HEADS * MLA_QK).astype(BF16), wk.astype(BF16), wv.astype(BF16)


def kernel(x, norm_mix, w_in, dq_norm, dkv_norm, w_uq, w_ukv, lam_q1, lam_k1, lam_q2, lam_k2,
           diff_norm, w_branch, w_out, norm_ffn, w_dense_in, w_dense_out, w_router,
           w_moe_in, w_moe_out, norm_final):
    batch, seq, d = x.shape
    depth = w_in.shape[0]
    t = batch * seq
    xt = x.reshape(t, d)

    rot_a = DIFF_DIM // 4
    tab_a = _rope_tables(seq, ((0, rot_a), (DIFF_DIM, rot_a)))
    tab_b = _rope_tables(seq, ((0, ROPE_DIM_B),))
    tab_c = _rope_tables(seq, ((0, HEAD_DIM // 4),))

    for layer in range(depth):
        proj = _inproj(xt, norm_mix[layer], _relayout_w_in(w_in[layer]))

        lam_init = 0.8 - 0.6 * math.exp(-0.3 * layer)
        lam_rows = jnp.stack([lam_q1[layer], lam_k1[layer], lam_q2[layer], lam_k2[layer]])
        y_a = _diff_attention(proj, lam_rows, diff_norm[layer], tab_a,
                              batch=batch, seq=seq, lam_init=lam_init)

        wuq, wuk, wuv = _relayout_mla(w_uq[layer], w_ukv[layer])
        q_b, k_b, v_b = _mla_prep(proj, dq_norm[layer], dkv_norm[layer], wuq, wuk, wuv, tab_b, seq=seq)
        y_b = _mla_attention(q_b, k_b, v_b, batch=batch, seq=seq)

        y_c = _dilated_attention(proj, tab_c, batch=batch, seq=seq)
        y_d = _stick_attention(proj, batch=batch, seq=seq)

        merged = _merge((y_a, y_b, y_c, y_d), proj, w_branch[layer].astype(BF16))
        xt = _outproj(xt, merged, w_out[layer].astype(BF16))

        last = layer == depth - 1
        if layer % 2 == 0:
            m = layer // 2
            xt = _swiglu_rows(xt, norm_ffn[layer], _dense_visits(t, MOE_CHUNK, MOE_VISIT_CHUNKS),
                              w_dense_in[m:m + 1], w_dense_out[m:m + 1], dense=True)
            if last:
                xt = _final_norm(xt, norm_final)
        else:
            m = layer // 2
            idx, gate = _router(xt, norm_ffn[layer], w_router[m])
            tables, dest = _routing_tables(idx, MOE_CHUNK, MOE_VISIT_CHUNKS)
            y_rows = _swiglu_rows(xt, norm_ffn[layer], tables, w_moe_in[m], w_moe_out[m], dense=False)
            xt = _moe_combine(xt, gate, y_rows, dest, norm_final if last else None)
    return xt.reshape(batch, seq, d)
```

```python
import functools
import math

import numpy as np
import jax
import jax.numpy as jnp
from jax import lax
from jax.experimental import pallas as pl
from jax.experimental.pallas import tpu as pltpu

F32 = jnp.float32
BF16 = jnp.bfloat16

HEAD_DIM = 128
ROPE_THETA = 500000.0
NORM_EPS = 1e-6
N_BRANCH = 4
BRANCH_WIDTH = 512
N_HEADS = 4
DIFF_DIM = 64
Q_LORA = 512
KV_LORA = 512
NOPE_DIM = 128
ROPE_DIM_B = 64
MLA_QK = 256
DIL_PAIRS = ((128, 1), (512, 4), (2048, 16))
N_DIL_GROUPS = 3
N_EXPERTS = 8
TOP_K = 2
LANES = 128
LOG2_E = math.log2(math.e)

A_OFF = 0
B_OFF = 1536
C_OFF = 3072
D_OFF = 7680
QKV_COLS = 9216

VMEM_LIMIT = 56 * 1024 * 1024


def _cparams(sem, vmem=VMEM_LIMIT):
    return pltpu.CompilerParams(dimension_semantics=sem, vmem_limit_bytes=vmem)


def _dot(a, b):
    return jnp.dot(a, b, preferred_element_type=F32)


def _dot_nt(a, b):
    return lax.dot_general(a, b, (((1,), (1,)), ((), ())), preferred_element_type=F32)


def _rms(x, gain):
    return x * lax.rsqrt(jnp.mean(x * x, axis=-1, keepdims=True) + NORM_EPS) * gain


def _sigmoid(x):
    return 1.0 / (1.0 + jnp.exp(-x))


def _rope_tables(seq, segments):
    pos = np.arange(seq, dtype=np.float64)
    c = np.ones((seq, LANES), np.float64)
    s = np.zeros((seq, LANES), np.float64)
    r = np.zeros((LANES, LANES), np.float32)
    for start, rot in segments:
        half = rot // 2
        inv_freq = ROPE_THETA ** (-np.arange(0, rot, 2, dtype=np.float64) / rot)
        ang = pos[:, None] * inv_freq[None, :]
        cos, sin = np.cos(ang), np.sin(ang)
        c[:, start:start + half] = cos
        c[:, start + half:start + rot] = cos
        s[:, start:start + half] = -sin
        s[:, start + half:start + rot] = sin
        for i in range(half):
            r[start + half + i, start + i] = 1.0
            r[start + i, start + half + i] = 1.0
    return jnp.asarray(c, F32), jnp.asarray(s, F32), jnp.asarray(r, BF16)


def _rope(x_bf16, rot, c, s):
    return x_bf16.astype(F32) * c + _dot(x_bf16, rot) * s


def _inproj_kernel(x_ref, g_ref, w_ref, o_ref, h_ref, *, gate_block):
    j = pl.program_id(1)

    @pl.when(j == 0)
    def _():
        h_ref[...] = _rms(x_ref[...], g_ref[...]).astype(BF16)

    acc = _dot(h_ref[...], w_ref[...])
    o_ref[...] = jnp.where(j >= gate_block, _sigmoid(acc), acc).astype(o_ref.dtype)


def _inproj(x, gain, w, *, tm=1024, tn=1024):
    t, d = x.shape
    n = w.shape[1]
    return pl.pallas_call(
        functools.partial(_inproj_kernel, gate_block=QKV_COLS // tn),
        out_shape=jax.ShapeDtypeStruct((t, n), BF16),
        grid=(t // tm, n // tn),
        in_specs=[pl.BlockSpec((tm, d), lambda i, j: (i, 0)),
                  pl.BlockSpec((1, d), lambda i, j: (0, 0)),
                  pl.BlockSpec((d, tn), lambda i, j: (0, j))],
        out_specs=pl.BlockSpec((tm, tn), lambda i, j: (i, j)),
        scratch_shapes=[pltpu.VMEM((tm, d), BF16)],
        compiler_params=_cparams(("parallel", "arbitrary")),
        name="inproj",
    )(x, gain.reshape(1, d), w)


def _osm(scores, values, carries):
    stats = []
    for s, (m, l, _) in zip(scores, carries):
        m_new = jnp.maximum(m, jnp.max(s, axis=-1, keepdims=True))
        alpha = jnp.exp2(m - m_new)
        p = jnp.exp2(s - m_new)
        stats.append((m_new, alpha * l + jnp.sum(p, axis=-1, keepdims=True), alpha, p))
    return tuple((m_new, l, alpha * acc + _dot(p.astype(BF16), v))
                 for (m_new, l, alpha, p), v, (_, _, acc) in zip(stats, values, carries))


def _causal_keep(t):
    r = lax.broadcasted_iota(jnp.int32, (t, t), 0)
    c = lax.broadcasted_iota(jnp.int32, (t, t), 1)
    return c <= r


def _diff_kernel(lam_ref, gain_ref, rot_ref, cq_ref, sq_ref, ck_ref, sk_ref,
                 q_ref, k_ref, v_ref, o_ref, kr_ref, *, tq, hp, lam_init):
    qi = pl.program_id(2)
    rot = rot_ref[...]
    heads = [slice(h * HEAD_DIM, (h + 1) * HEAD_DIM) for h in range(hp)]

    @pl.when(qi == 0)
    def _():
        for hs in heads:
            kr_ref[:, hs] = _rope(k_ref[:, hs], rot, ck_ref[...], sk_ref[...]).astype(BF16)

    scale2 = LOG2_E / math.sqrt(DIFF_DIM)
    lane = lax.broadcasted_iota(jnp.int32, (tq, HEAD_DIM), 1)
    chains = []
    for hs in heads:
        qf = _rope(q_ref[:, hs], rot, cq_ref[...], sq_ref[...])
        chains.append((jnp.where(lane < DIFF_DIM, qf, 0.0).astype(BF16), hs))
        chains.append((jnp.where(lane >= DIFF_DIM, qf, 0.0).astype(BF16), hs))

    def step(j, carry, masked):
        off = pl.multiple_of(j * tq, tq)
        scores = [_dot_nt(q, kr_ref[pl.ds(off, tq), hs]) * scale2 for q, hs in chains]
        if masked:
            keep = _causal_keep(tq)
            scores = [jnp.where(keep, s, -jnp.inf) for s in scores]
        return _osm(scores, [v_ref[pl.ds(off, tq), hs] for _, hs in chains], carry)

    init = (jnp.full((tq, 1), -jnp.inf, F32), jnp.zeros((tq, 1), F32),
            jnp.zeros((tq, HEAD_DIM), F32))
    carry = lax.fori_loop(0, qi, lambda j, c: step(j, c, False), tuple(init for _ in chains))
    carry = step(qi, carry, True)

    lam_rows = lam_ref[...]
    lam = (jnp.exp(jnp.sum(lam_rows[0:1] * lam_rows[1:2], axis=-1, keepdims=True))
           - jnp.exp(jnp.sum(lam_rows[2:3] * lam_rows[3:4], axis=-1, keepdims=True))
           + lam_init)
    for i, hs in enumerate(heads):
        (_, l1, a1), (_, l2, a2) = carry[2 * i], carry[2 * i + 1]
        out = a1 / l1 - lam * (a2 / l2)
        o_ref[:, hs] = (_rms(out, gain_ref[...]) * (1.0 - lam_init)).astype(o_ref.dtype)


def _diff_attention(proj, lam_rows, gain, tables, *, batch, seq, lam_init, tq=256, hp=4):
    c, s, rot = tables
    nq = seq // tq
    w = hp * HEAD_DIM
    cb = A_OFF // w
    ng = N_HEADS // hp
    return pl.pallas_call(
        functools.partial(_diff_kernel, tq=tq, hp=hp, lam_init=lam_init),
        out_shape=jax.ShapeDtypeStruct((batch * seq, BRANCH_WIDTH), BF16),
        grid=(batch, ng, nq),
        in_specs=[pl.BlockSpec((4, DIFF_DIM), lambda b, h, i: (0, 0)),
                  pl.BlockSpec((1, HEAD_DIM), lambda b, h, i: (0, 0)),
                  pl.BlockSpec((LANES, LANES), lambda b, h, i: (0, 0)),
                  pl.BlockSpec((tq, LANES), lambda b, h, i: (i, 0)),
                  pl.BlockSpec((tq, LANES), lambda b, h, i: (i, 0)),
                  pl.BlockSpec((seq, LANES), lambda b, h, i: (0, 0)),
                  pl.BlockSpec((seq, LANES), lambda b, h, i: (0, 0)),
                  pl.BlockSpec((tq, w), lambda b, h, i: (b * nq + i, cb + h)),
                  pl.BlockSpec((seq, w), lambda b, h, i: (b, cb + ng + h)),
                  pl.BlockSpec((seq, w), lambda b, h, i: (b, cb + 2 * ng + h))],
        out_specs=pl.BlockSpec((tq, w), lambda b, h, i: (b * nq + i, h)),
        scratch_shapes=[pltpu.VMEM((seq, w), BF16)],
        compiler_params=_cparams(("parallel", "parallel", "arbitrary")),
        name="diff_attention",
    )(lam_rows, gain.reshape(1, HEAD_DIM), rot, c, s, c, s, proj, proj, proj)


def _mla_prep_kernel(cq_ref, ckv_ref, kr_ref, dqn_ref, dkvn_ref, wuq_ref, wuk_ref, wuv_ref,
                     rot_ref, c_ref, s_ref, q_out, k_out, v_out):
    rot = rot_ref[...]
    c = c_ref[...]
    s = s_ref[...]
    hq = _rms(cq_ref[...].astype(F32), dqn_ref[...]).astype(BF16)
    hkv = _rms(ckv_ref[...].astype(F32), dkvn_ref[...]).astype(BF16)
    q = _dot(hq, wuq_ref[...])
    kn = _dot(hkv, wuk_ref[...])
    v_out[...] = _dot(hkv, wuv_ref[...]).astype(BF16)
    k_rope = _rope(kr_ref[...], rot, c, s).astype(BF16)
    for h in range(N_HEADS):
        lo = h * MLA_QK
        q_out[:, lo:lo + NOPE_DIM] = q[:, lo:lo + NOPE_DIM].astype(BF16)
        q_out[:, lo + NOPE_DIM:lo + MLA_QK] = _rope(
            q[:, lo + NOPE_DIM:lo + MLA_QK].astype(BF16), rot, c, s).astype(BF16)
        k_out[:, lo:lo + NOPE_DIM] = kn[:, h * NOPE_DIM:(h + 1) * NOPE_DIM].astype(BF16)
        k_out[:, lo + NOPE_DIM:lo + MLA_QK] = k_rope


def _mla_prep(proj, dq_norm, dkv_norm, wuq, wuk, wuv, tables, *, seq, tm=512):
    c, s, rot = tables
    t = proj.shape[0]
    nb = seq // tm
    cb = B_OFF // Q_LORA
    full = lambda shape: pl.BlockSpec(shape, lambda i: (0, 0))
    return pl.pallas_call(
        _mla_prep_kernel,
        out_shape=(jax.ShapeDtypeStruct((t, N_HEADS * MLA_QK), BF16),
                   jax.ShapeDtypeStruct((t, N_HEADS * MLA_QK), BF16),
                   jax.ShapeDtypeStruct((t, N_HEADS * HEAD_DIM), BF16)),
        grid=(t // tm,),
        in_specs=[pl.BlockSpec((tm, Q_LORA), lambda i: (i, cb)),
                  pl.BlockSpec((tm, KV_LORA), lambda i: (i, cb + 1)),
                  pl.BlockSpec((tm, LANES), lambda i: (i, (B_OFF + Q_LORA + KV_LORA) // LANES)),
                  full((1, Q_LORA)), full((1, KV_LORA)),
                  full(wuq.shape), full(wuk.shape), full(wuv.shape),
                  full((LANES, LANES)),
                  pl.BlockSpec((tm, LANES), lambda i: (i % nb, 0)),
                  pl.BlockSpec((tm, LANES), lambda i: (i % nb, 0))],
        out_specs=(pl.BlockSpec((tm, N_HEADS * MLA_QK), lambda i: (i, 0)),
                   pl.BlockSpec((tm, N_HEADS * MLA_QK), lambda i: (i, 0)),
                   pl.BlockSpec((tm, N_HEADS * HEAD_DIM), lambda i: (i, 0))),
        compiler_params=_cparams(("parallel",)),
        name="mla_prep",
    )(proj, proj, proj, dq_norm.reshape(1, Q_LORA), dkv_norm.reshape(1, KV_LORA),
      wuq, wuk, wuv, rot, c, s)


def _flash_kernel(q_ref, k_ref, v_ref, o_ref, *, tq, hp, dqk, scale):
    qi = pl.program_id(2)
    qk = [slice(h * dqk, (h + 1) * dqk) for h in range(hp)]
    hv = [slice(h * HEAD_DIM, (h + 1) * HEAD_DIM) for h in range(hp)]
    qs = [q_ref[:, sl] for sl in qk]

    def step(j, carry, masked):
        off = pl.multiple_of(j * tq, tq)
        scores = [_dot_nt(q, k_ref[pl.ds(off, tq), ks]) * (scale * LOG2_E) for q, ks in zip(qs, qk)]
        if masked:
            keep = _causal_keep(tq)
            scores = [jnp.where(keep, s, -jnp.inf) for s in scores]
        return _osm(scores, [v_ref[pl.ds(off, tq), vs] for vs in hv], carry)

    init = (jnp.full((tq, 1), -jnp.inf, F32), jnp.zeros((tq, 1), F32),
            jnp.zeros((tq, HEAD_DIM), F32))
    carry = lax.fori_loop(0, qi, lambda j, c: step(j, c, False), tuple(init for _ in qs))
    carry = step(qi, carry, True)
    for vs, (_, l, acc) in zip(hv, carry):
        o_ref[:, vs] = (acc / l).astype(o_ref.dtype)


def _mla_attention(q, k, v, *, batch, seq, tq=256, hp=4):
    nq = seq // tq
    ng = N_HEADS // hp
    return pl.pallas_call(
        functools.partial(_flash_kernel, tq=tq, hp=hp, dqk=MLA_QK,
                          scale=1.0 / math.sqrt(NOPE_DIM + ROPE_DIM_B)),
        out_shape=jax.ShapeDtypeStruct((batch * seq, BRANCH_WIDTH), BF16),
        grid=(batch, ng, nq),
        in_specs=[pl.BlockSpec((tq, hp * MLA_QK), lambda b, h, i: (b * nq + i, h)),
                  pl.BlockSpec((seq, hp * MLA_QK), lambda b, h, i: (b, h)),
                  pl.BlockSpec((seq, hp * HEAD_DIM), lambda b, h, i: (b, h))],
        out_specs=pl.BlockSpec((tq, hp * HEAD_DIM), lambda b, h, i: (b * nq + i, h)),
        compiler_params=_cparams(("parallel", "parallel", "arbitrary")),
        name="mla_attention",
    )(q, k, v)


DIL_BLOCK = 128


def _dilated_kernel(rot_ref, c_ref, s_ref, *refs, seq):
    in_refs = refs[:9]
    o_ref = refs[9]
    qf, kf, vf, og, lse = refs[10:]
    rot = rot_ref[...]
    c = c_ref[...]
    s = s_ref[...]
    for g in range(N_DIL_GROUPS):
        qf[g] = _rope(in_refs[g][...], rot, c, s)
        kf[g] = _rope(in_refs[3 + g][...], rot, c, s)
        vf[g] = in_refs[6 + g][...].astype(F32)
    scale = 1.0 / math.sqrt(HEAD_DIM)
    blk = DIL_BLOCK

    def rows(start, size, stride):
        return pl.ds(start, size) if stride == 1 else pl.ds(start, size, stride=stride)

    for g, (window, dil) in enumerate(DIL_PAIRS):
        assert window == blk * dil
        span = blk * dil
        n_sub = seq // span
        nk = 2 * blk if n_sub > 1 else blk
        assert n_sub & (n_sub - 1) == 0
        q_rows, k_rows = [], []
        for r in range(dil):
            for cb in range(n_sub):
                q_rows.append(rows(r + cb * span, blk, dil))
                k_rows.append(rows(r + max(cb - 1, 0) * span, nk, dil))
        nb = len(q_rows)
        q = jnp.stack([qf[g, qr, :] for qr in q_rows]).astype(BF16)
        k = jnp.stack([kf[g, kr, :] for kr in k_rows]).astype(BF16)
        v = jnp.stack([vf[g, kr, :] for kr in k_rows]).astype(BF16)
        sc = jnp.einsum("bqd,bkd->bqk", q, k, preferred_element_type=F32) * (scale * LOG2_E)
        first = (lax.broadcasted_iota(jnp.int32, (nb, blk, nk), 0) & (n_sub - 1)) == 0
        dist = (lax.broadcasted_iota(jnp.int32, (nb, blk, nk), 1)
                - lax.broadcasted_iota(jnp.int32, (nb, blk, nk), 2)
                + jnp.where(first, 0, nk - blk))
        sc = jnp.where(dist >= 0, jnp.where(dist <= blk, sc, -jnp.inf), -jnp.inf)
        m = jnp.max(sc, axis=-1, keepdims=True)
        e = jnp.exp2(sc - m)
        den = jnp.sum(e, axis=-1, keepdims=True)
        o = jnp.einsum("bqk,bkd->bqd", e.astype(BF16), v, preferred_element_type=F32) / den
        lg = jnp.broadcast_to(m + jnp.log2(den), (nb, blk, HEAD_DIM))
        for i, qr in enumerate(q_rows):
            og[g, qr, :] = o[i]
            lse[g, qr, :] = lg[i]

    l0, l1, l2 = lse[0], lse[1], lse[2]
    mx = jnp.maximum(jnp.maximum(l0, l1), l2)
    w0, w1, w2 = jnp.exp2(l0 - mx), jnp.exp2(l1 - mx), jnp.exp2(l2 - mx)
    o_ref[...] = ((w0 * og[0] + w1 * og[1] + w2 * og[2]) / (w0 + w1 + w2)).astype(o_ref.dtype)


def _dilated_attention(proj, tables, *, batch, seq):
    c, s, rot = tables
    cb = C_OFF // HEAD_DIM
    nh = N_DIL_GROUPS * N_HEADS

    def col(kind, g):
        return lambda b, h: (b, cb + kind * nh + g * N_HEADS + h)

    in_specs = [pl.BlockSpec((LANES, LANES), lambda b, h: (0, 0)),
                pl.BlockSpec((seq, LANES), lambda b, h: (0, 0)),
                pl.BlockSpec((seq, LANES), lambda b, h: (0, 0))]
    for kind in range(3):
        for g in range(N_DIL_GROUPS):
            in_specs.append(pl.BlockSpec((seq, HEAD_DIM), col(kind, g)))
    return pl.pallas_call(
        functools.partial(_dilated_kernel, seq=seq),
        out_shape=jax.ShapeDtypeStruct((batch * seq, BRANCH_WIDTH), BF16),
        grid=(batch, N_HEADS),
        in_specs=in_specs,
        out_specs=pl.BlockSpec((seq, HEAD_DIM), lambda b, h: (b, h)),
        scratch_shapes=[pltpu.VMEM((N_DIL_GROUPS, seq, HEAD_DIM), F32) for _ in range(5)],
        compiler_params=_cparams(("parallel", "parallel")),
        name="dilated_attention",
    )(rot, c, s, *([proj] * 9))


def _stick_kernel(q_ref, k_ref, v_ref, o_ref, *, tq, hp, scale):
    qi = pl.program_id(2)
    heads = [slice(h * HEAD_DIM, (h + 1) * HEAD_DIM) for h in range(hp)]
    qs = [q_ref[:, hs] for hs in heads]
    r = lax.broadcasted_iota(jnp.int32, (tq, tq), 0)
    c = lax.broadcasted_iota(jnp.int32, (tq, tq), 1)
    later_keys = jnp.where(r > c, 1.0, 0.0).astype(BF16)

    def step(j, carry, diag):
        off = pl.multiple_of(j * tq, tq)
        strict = c < r
        z2 = [_dot_nt(q, k_ref[pl.ds(off, tq), hs]) * (scale * LOG2_E) for q, hs in zip(qs, heads)]
        sp2 = [jnp.maximum(z, 0.0) + jnp.log2(1.0 + jnp.exp2(-jnp.abs(z))) for z in z2]
        log_not = [jnp.where(strict, -sp, 0.0) if diag else -sp for sp in sp2]
        later = []
        for ln, (tail, _) in zip(log_not, carry):
            hi = ln.astype(BF16)
            lo = (ln - hi.astype(F32)).astype(BF16)
            later.append(_dot(hi, later_keys) + _dot(lo, later_keys) + tail)
        a = [jnp.exp2((z - sp) + lt) for z, sp, lt in zip(z2, sp2, later)]
        if diag:
            a = [jnp.where(strict, x, 0.0) for x in a]
        out = []
        for x, ln, hs, (tail, acc) in zip(a, log_not, heads, carry):
            acc = acc + _dot(x.astype(BF16), v_ref[pl.ds(off, tq), hs])
            out.append((tail + jnp.sum(ln, axis=-1, keepdims=True), acc))
        return tuple(out)

    init = (jnp.zeros((tq, 1), F32), jnp.zeros((tq, HEAD_DIM), F32))
    carry = step(qi, tuple(init for _ in heads), True)
    carry = lax.fori_loop(0, qi, lambda t, cr: step(qi - 1 - t, cr, False), carry)
    for hs, (_, acc) in zip(heads, carry):
        o_ref[:, hs] = acc.astype(o_ref.dtype)


def _stick_attention(proj, *, batch, seq, tq=256, hp=4):
    nq = seq // tq
    w = hp * HEAD_DIM
    cb = D_OFF // w
    ng = N_HEADS // hp
    return pl.pallas_call(
        functools.partial(_stick_kernel, tq=tq, hp=hp, scale=1.0 / math.sqrt(HEAD_DIM)),
        out_shape=jax.ShapeDtypeStruct((batch * seq, BRANCH_WIDTH), BF16),
        grid=(batch, ng, nq),
        in_specs=[pl.BlockSpec((tq, w), lambda b, h, i: (b * nq + i, cb + h)),
                  pl.BlockSpec((seq, w), lambda b, h, i: (b, cb + ng + h)),
                  pl.BlockSpec((seq, w), lambda b, h, i: (b, cb + 2 * ng + h))],
        out_specs=pl.BlockSpec((tq, w), lambda b, h, i: (b * nq + i, h)),
        compiler_params=_cparams(("parallel", "parallel", "arbitrary")),
        name="stick_attention",
    )(proj, proj, proj)


def _merge_kernel(ya, yb, yc, yd, g0, g1, g2, g3, wb_ref, o_ref):
    acc = None
    for i, (y, g) in enumerate(((ya, g0), (yb, g1), (yc, g2), (yd, g3))):
        t = g[...].astype(F32) * _dot(y[...], wb_ref[i])
        acc = t if acc is None else acc + t
    o_ref[...] = acc.astype(o_ref.dtype)


def _merge(ys, proj, wb, *, tm=1024, tn=1024):
    t = proj.shape[0]
    d = wb.shape[2]
    nn = d // tn
    g0 = QKV_COLS // tn
    y_spec = pl.BlockSpec((tm, BRANCH_WIDTH), lambda i, j: (i, 0))
    g_specs = [pl.BlockSpec((tm, tn), functools.partial(lambda i, j, b: (i, g0 + b * nn + j), b=b))
               for b in range(N_BRANCH)]
    return pl.pallas_call(
        _merge_kernel,
        out_shape=jax.ShapeDtypeStruct((t, d), BF16),
        grid=(t // tm, nn),
        in_specs=[y_spec] * N_BRANCH + g_specs
                 + [pl.BlockSpec((N_BRANCH, BRANCH_WIDTH, tn), lambda i, j: (0, 0, j))],
        out_specs=pl.BlockSpec((tm, tn), lambda i, j: (i, j)),
        compiler_params=_cparams(("parallel", "arbitrary")),
        name="branch_merge",
    )(*ys, proj, proj, proj, proj, wb)


def _outproj_kernel(x_ref, m_ref, w_ref, o_ref):
    o_ref[...] = x_ref[...] + _dot(m_ref[...], w_ref[...])


def _outproj(x, merged, w, *, tm=1024, tn=1024):
    t, d = x.shape
    return pl.pallas_call(
        _outproj_kernel,
        out_shape=jax.ShapeDtypeStruct((t, d), F32),
        grid=(t // tm, d // tn),
        in_specs=[pl.BlockSpec((tm, tn), lambda i, j: (i, j)),
                  pl.BlockSpec((tm, d), lambda i, j: (i, 0)),
                  pl.BlockSpec((d, tn), lambda i, j: (0, j))],
        out_specs=pl.BlockSpec((tm, tn), lambda i, j: (i, j)),
        compiler_params=_cparams(("parallel", "arbitrary")),
        name="out_proj",
    )(x, merged, w)


def _router_kernel(x_ref, gain_ref, wr_ref, idx_ref, gate_ref):
    h = _rms(x_ref[...], gain_ref[...])
    logits = jnp.dot(h, wr_ref[...], preferred_element_type=F32, precision=lax.Precision.HIGHEST)
    lane = lax.broadcasted_iota(jnp.int32, logits.shape, 1)
    lanef = lane.astype(F32)
    lg = jnp.where(lane < N_EXPERTS, logits, -jnp.inf)
    v1 = jnp.max(lg, axis=-1, keepdims=True)
    i1 = jnp.min(jnp.where(lg == v1, lanef, float(LANES)), axis=-1, keepdims=True)
    lg2 = jnp.where(lanef == i1, -jnp.inf, lg)
    v2 = jnp.max(lg2, axis=-1, keepdims=True)
    i2 = jnp.min(jnp.where(lg2 == v2, lanef, float(LANES)), axis=-1, keepdims=True)
    e2 = jnp.exp(v2 - v1)
    g1 = 1.0 / (1.0 + e2)
    g2 = e2 / (1.0 + e2)
    idx_ref[...] = jnp.where(lane == 0, i1, jnp.where(lane == 1, i2, 0.0)).astype(jnp.int32)
    gate_ref[...] = jnp.where(lane == 0, g1, jnp.where(lane == 1, g2, 0.0))


def _router(x, gain, w_router, *, tm=512):
    t, d = x.shape
    wr = jnp.zeros((d, LANES), F32).at[:, :N_EXPERTS].set(w_router)
    return pl.pallas_call(
        _router_kernel,
        out_shape=(jax.ShapeDtypeStruct((t, LANES), jnp.int32),
                   jax.ShapeDtypeStruct((t, LANES), F32)),
        grid=(t // tm,),
        in_specs=[pl.BlockSpec((tm, d), lambda i: (i, 0)),
                  pl.BlockSpec((1, d), lambda i: (0, 0)),
                  pl.BlockSpec((d, LANES), lambda i: (0, 0))],
        out_specs=(pl.BlockSpec((tm, LANES), lambda i: (i, 0)),
                   pl.BlockSpec((tm, LANES), lambda i: (i, 0))),
        compiler_params=_cparams(("parallel",)),
        name="router",
    )(x, gain.reshape(1, d), wr)


def _start_row_gather(idx_ref, base, n, src_hbm, dst, sem):
    def start(r, _):
        row = idx_ref[base + r]
        pltpu.make_async_copy(src_hbm.at[pl.ds(row, 1), :], dst.at[pl.ds(r, 1), :], sem).start()
        return 0

    lax.fori_loop(0, n, start, 0, unroll=8)


def _wait_row_gather(n, src_hbm, dst, sem):
    pltpu.make_async_copy(src_hbm.at[pl.ds(0, n), :], dst.at[pl.ds(0, n), :], sem).wait()


MOE_CHUNK = 256
MOE_VISIT_CHUNKS = 5
DENSE_CHUNK = 512
DENSE_VISIT_CHUNKS = 3
WEIGHT_SPLIT = 4


def _moe_kernel(vis_e_ref, vis_ok_ref, vis_row_ref, vis_nch_ref, tail_ref, row_tok_ref,
                x_hbm, gain_ref, *refs, ch, dense):
    del vis_e_ref, vis_ok_ref
    ns = WEIGHT_SPLIT
    wg_refs, wu_refs, wo_refs = refs[:ns], refs[ns:2 * ns], refs[2 * ns:3 * ns]
    y_hbm, stage, h_ref, acc_ref, gsem, osem = refs[3 * ns:]
    v = pl.program_id(0)
    f = pl.program_id(1)
    nv = pl.num_programs(0)
    nf = pl.num_programs(1)
    nch = vis_nch_ref[v]
    row0 = vis_row_ref[v]
    active = nch > 0
    kq = h_ref.shape[1] // ns

    def rows(c):
        return pl.ds(pl.multiple_of(c * ch, ch), ch)

    def step(h):
        parts = [h[:, q * kq:(q + 1) * kq] for q in range(ns)]
        g = sum(_dot(hq, w[...].astype(BF16)) for hq, w in zip(parts, wg_refs))
        u = sum(_dot(hq, w[...].astype(BF16)) for hq, w in zip(parts, wu_refs))
        a = (g * _sigmoid(g) * u).astype(BF16)
        return jnp.concatenate([_dot(a, w[...].astype(BF16)) for w in wo_refs], axis=1)

    def out_copy(c):
        dst = y_hbm.at[pl.ds(pl.multiple_of(row0 + c * ch, ch), ch), :]
        return pltpu.make_async_copy(acc_ref.at[rows(c), :], dst, osem)

    @pl.when(jnp.logical_and(active, f == 0))
    def _():
        def fetch(c, slot):
            if dense:
                src = x_hbm.at[pl.ds(pl.multiple_of(row0 + c * ch, ch), ch), :]
                pltpu.make_async_copy(src, stage.at[slot], gsem.at[slot]).start()
            else:
                _start_row_gather(row_tok_ref, row0 + c * ch, ch, x_hbm, stage.at[slot], gsem.at[slot])

        fetch(0, 0)

        def body(c, _):
            slot = lax.rem(c, 2)

            @pl.when(c + 1 < nch)
            def _():
                fetch(c + 1, 1 - slot)

            _wait_row_gather(ch, x_hbm, stage.at[slot], gsem.at[slot])
            x = stage[slot]
            h = _rms(x, gain_ref[...]).astype(BF16)
            h_ref[rows(c), :] = h
            acc_ref[rows(c), :] = x + step(h) if dense else step(h)
            return 0

        lax.fori_loop(0, nch, body, 0)

    @pl.when(jnp.logical_and(active, jnp.logical_and(f > 0, f < nf - 1)))
    def _():
        def body(c, _):
            acc_ref[rows(c), :] += step(h_ref[rows(c), :])
            return 0

        lax.fori_loop(0, nch, body, 0)

    @pl.when(jnp.logical_and(active, f == nf - 1))
    def _():
        def body(c, _):
            acc_ref[rows(c), :] += step(h_ref[rows(c), :])
            out_copy(c).start()
            return 0

        def drain(c, _):
            out_copy(c).wait()
            return 0

        lax.fori_loop(0, nch, body, 0)
        lax.fori_loop(0, nch, drain, 0)

    @pl.when(jnp.logical_and(v == nv - 1, f == nf - 1))
    def _():
        first = tail_ref[0]
        n_tail = y_hbm.shape[0] // ch - first
        stage[0] = jnp.zeros(stage.shape[1:], stage.dtype)

        def tail_copy(c):
            dst = y_hbm.at[pl.ds(pl.multiple_of((first + c) * ch, ch), ch), :]
            return pltpu.make_async_copy(stage.at[0], dst, osem)

        def fill(c, _):
            tail_copy(c).start()
            return 0

        def drain(c, _):
            tail_copy(c).wait()
            return 0

        lax.fori_loop(0, n_tail, fill, 0)
        lax.fori_loop(0, n_tail, drain, 0)


def _dense_visits(t, ch, max_chunks):
    span = ch * max_chunks
    row0 = np.arange(0, t, span, dtype=np.int32)
    n_ch = np.minimum(max_chunks, (t - row0) // ch).astype(np.int32)
    zeros = np.zeros_like(row0)
    return tuple(jnp.asarray(a) for a in (zeros, zeros + 1, row0, n_ch,
                                          np.array([t // ch], np.int32), np.zeros((1,), np.int32)))


def _swiglu_rows(x, gain, tables, w_in, w_out, *, dense, ch, max_chunks, tf=512):
    vis_e, vis_ok, vis_row, vis_nch, tail, row_tok = tables
    t, d = x.shape
    ff = w_out.shape[1]
    nf = ff // tf
    p = t if dense else row_tok.shape[0]
    ns = WEIGHT_SPLIT

    def wspec(shape, index):
        def index_map(v, f, e, ok, *_):
            return index(e[v], jnp.where(ok[v] > 0, f, nf - 1))
        return pl.BlockSpec(shape, index_map)

    wg_specs = [wspec((None, d // ns, tf), functools.partial(lambda e, f, q: (e, q, f), q=q))
                for q in range(ns)]
    wu_specs = [wspec((None, d // ns, tf), functools.partial(lambda e, f, q: (e, q, nf + f), q=q))
                for q in range(ns)]
    wo_specs = [wspec((None, tf, d // ns), functools.partial(lambda e, f, q: (e, f, q), q=q))
                for q in range(ns)]
    grid_spec = pltpu.PrefetchScalarGridSpec(
        num_scalar_prefetch=6,
        grid=(vis_e.shape[0], nf),
        in_specs=[pl.BlockSpec(memory_space=pl.ANY),
                  pl.BlockSpec((1, d), lambda v, f, *_: (0, 0))] + wg_specs + wu_specs + wo_specs,
        out_specs=pl.BlockSpec(memory_space=pl.ANY),
        scratch_shapes=[pltpu.VMEM((2, ch, d), F32),
                        pltpu.VMEM((max_chunks * ch, d), BF16),
                        pltpu.VMEM((max_chunks * ch, d), F32),
                        pltpu.SemaphoreType.DMA((2,)), pltpu.SemaphoreType.DMA],
    )
    return pl.pallas_call(
        functools.partial(_moe_kernel, ch=ch, dense=dense),
        out_shape=jax.ShapeDtypeStruct((p, d), F32),
        grid_spec=grid_spec,
        compiler_params=_cparams(("arbitrary", "arbitrary"), vmem=60 * 1024 * 1024),
        name="dense_ffn" if dense else "moe_experts",
    )(vis_e, vis_ok, vis_row, vis_nch, tail, row_tok, x, gain.reshape(1, d),
      *([w_in] * (2 * ns)), *([w_out] * ns))


def _combine_kernel(pos_ref, x_ref, gate_ref, y_hbm, gain_ref, o_ref, buf, sem, *, tm):
    i = pl.program_id(0)
    n = pl.num_programs(0)
    t = pos_ref.shape[0] // TOP_K
    slot = lax.rem(i, 2)

    def start(step, sl):
        for k in range(TOP_K):
            _start_row_gather(pos_ref, k * t + step * tm, tm, y_hbm, buf.at[sl, k], sem.at[sl, k])

    @pl.when(i == 0)
    def _():
        start(0, 0)

    @pl.when(i + 1 < n)
    def _():
        start(i + 1, 1 - slot)

    for k in range(TOP_K):
        _wait_row_gather(tm, y_hbm, buf.at[slot, k], sem.at[slot, k])
    g = gate_ref[...]
    x = x_ref[...] + (g[:, 0:1] * buf[slot, 0] + g[:, 1:2] * buf[slot, 1])
    if gain_ref is None:
        o_ref[...] = x
    else:
        o_ref[...] = _rms(x, gain_ref[...])


def _moe_combine(x, gate, y_rows, pos, final_gain, *, tm=256):
    t, d = x.shape
    in_specs = [pl.BlockSpec((tm, d), lambda i, ps: (i, 0)),
                pl.BlockSpec((tm, LANES), lambda i, ps: (i, 0)),
                pl.BlockSpec(memory_space=pl.ANY)]
    args = [x, gate, y_rows]
    if final_gain is not None:
        in_specs.append(pl.BlockSpec((1, d), lambda i, ps: (0, 0)))
        args.append(final_gain.reshape(1, d))
        body = functools.partial(_combine_kernel, tm=tm)
    else:
        def body(pos_ref, x_ref, gate_ref, y_hbm, o_ref, *scratch):
            _combine_kernel(pos_ref, x_ref, gate_ref, y_hbm, None, o_ref, *scratch, tm=tm)
    grid_spec = pltpu.PrefetchScalarGridSpec(
        num_scalar_prefetch=1,
        grid=(t // tm,),
        in_specs=in_specs,
        out_specs=pl.BlockSpec((tm, d), lambda i, ps: (i, 0)),
        scratch_shapes=[pltpu.VMEM((2, TOP_K, tm, d), F32), pltpu.SemaphoreType.DMA((2, TOP_K))],
    )
    return pl.pallas_call(
        body,
        out_shape=jax.ShapeDtypeStruct((t, d), F32),
        grid_spec=grid_spec,
        compiler_params=_cparams(("arbitrary",)),
        name="moe_combine",
    )(pos, *args)


def _final_norm_kernel(x_ref, g_ref, o_ref):
    o_ref[...] = _rms(x_ref[...], g_ref[...])


def _final_norm(x, gain, *, tm=512):
    t, d = x.shape
    return pl.pallas_call(
        _final_norm_kernel,
        out_shape=jax.ShapeDtypeStruct((t, d), F32),
        grid=(t // tm,),
        in_specs=[pl.BlockSpec((tm, d), lambda i: (i, 0)), pl.BlockSpec((1, d), lambda i: (0, 0))],
        out_specs=pl.BlockSpec((tm, d), lambda i: (i, 0)),
        compiler_params=_cparams(("parallel",)),
        name="final_norm",
    )(x, gain.reshape(1, d))


def _routing_tables(idx, ch, max_chunks):
    t = idx.shape[0]
    e_flat = idx[:, :TOP_K].T.reshape(-1)
    tok = jnp.tile(jnp.arange(t, dtype=jnp.int32), TOP_K)
    experts = jnp.arange(N_EXPERTS, dtype=jnp.int32)
    onehot = (e_flat[:, None] == experts[None, :]).astype(jnp.int32)
    rank = jnp.sum((jnp.cumsum(onehot, axis=0) - onehot) * onehot, axis=1)
    counts = jnp.sum(onehot, axis=0)
    n_chunk = (counts + ch - 1) // ch
    ends = jnp.cumsum(n_chunk * ch)
    starts = ends - n_chunk * ch
    dest = (starts[e_flat] + rank).astype(jnp.int32)
    p = TOP_K * t + N_EXPERTS * ch
    row_tok = jnp.zeros((p,), jnp.int32).at[dest].set(tok)

    n_vis = (n_chunk + max_chunks - 1) // max_chunks
    per_vis = (n_chunk + jnp.maximum(n_vis, 1) - 1) // jnp.maximum(n_vis, 1)
    v_end = jnp.cumsum(n_vis)
    v_start = v_end - n_vis
    nv_max = -(-p // (ch * max_chunks)) + N_EXPERTS
    slot = jnp.arange(nv_max, dtype=jnp.int32)
    ok = slot < v_end[-1]
    e_of = jnp.minimum(jnp.sum((slot[:, None] >= v_end[None, :]).astype(jnp.int32), axis=1),
                       N_EXPERTS - 1)
    e_of = jnp.where(ok, e_of, e_of[v_end[-1] - 1])
    k = slot - v_start[e_of]
    n_ch = jnp.where(ok, jnp.clip(n_chunk[e_of] - k * per_vis[e_of], 0, per_vis[e_of]), 0)
    row0 = jnp.where(ok, starts[e_of] + k * per_vis[e_of] * ch, 0)
    tail = (ends[-1:] // ch).astype(jnp.int32)
    tables = (e_of.astype(jnp.int32), ok.astype(jnp.int32), row0.astype(jnp.int32),
              n_ch.astype(jnp.int32), tail, row_tok)
    return tables, dest


B_END = B_OFF + Q_LORA + KV_LORA + ROPE_DIM_B


def _relayout_kernel(prev_ref, cur_ref, o_ref, *, tn):
    j = pl.program_id(1)
    n_copy = B_END // tn
    shift = C_OFF - B_END

    @pl.when(j < n_copy)
    def _():
        o_ref[...] = cur_ref[...].astype(o_ref.dtype)

    @pl.when(j == n_copy)
    def _():
        col = lax.broadcasted_iota(jnp.int32, cur_ref.shape, 1)
        o_ref[...] = jnp.where(col < B_END - n_copy * tn, cur_ref[...], 0.0).astype(o_ref.dtype)

    @pl.when(j > n_copy)
    def _():
        both = jnp.concatenate([prev_ref[...], cur_ref[...]], axis=1)
        o_ref[...] = both[:, tn - shift:2 * tn - shift].astype(o_ref.dtype)


def _relayout_w_in(w3, layer, *, tr=512, tn=1024):
    _, d, n_in = w3.shape
    n_out = n_in + C_OFF - B_END
    assert C_OFF - B_END < tn and (B_END // tn + 1) * tn == C_OFF and n_out % tn == 0
    return pl.pallas_call(
        functools.partial(_relayout_kernel, tn=tn),
        out_shape=jax.ShapeDtypeStruct((d, n_out), BF16),
        grid=(d // tr, n_out // tn),
        in_specs=[pl.BlockSpec((None, tr, tn), lambda r, j: (layer, r, jnp.maximum(j - 1, 0))),
                  pl.BlockSpec((None, tr, tn), lambda r, j: (layer, r, j))],
        out_specs=pl.BlockSpec((tr, tn), lambda r, j: (r, j)),
        compiler_params=_cparams(("parallel", "arbitrary")),
        name="w_in_relayout",
    )(w3, w3)


def _relayout_mla(w_uq, w_ukv):
    dq = NOPE_DIM + ROPE_DIM_B
    wq = w_uq.reshape(Q_LORA, N_HEADS, dq)
    wq = jnp.concatenate([wq, jnp.zeros((Q_LORA, N_HEADS, MLA_QK - dq), w_uq.dtype)], axis=2)
    wkv = w_ukv.reshape(KV_LORA, N_HEADS, NOPE_DIM + HEAD_DIM)
    wk = wkv[:, :, :NOPE_DIM].reshape(KV_LORA, N_HEADS * NOPE_DIM)
    wv = wkv[:, :, NOPE_DIM:].reshape(KV_LORA, N_HEADS * HEAD_DIM)
    return wq.reshape(Q_LORA, N_HEADS * MLA_QK).astype(BF16), wk.astype(BF16), wv.astype(BF16)


def kernel(x, norm_mix, w_in, dq_norm, dkv_norm, w_uq, w_ukv, lam_q1, lam_k1, lam_q2, lam_k2,
           diff_norm, w_branch, w_out, norm_ffn, w_dense_in, w_dense_out, w_router,
           w_moe_in, w_moe_out, norm_final):
    batch, seq, d = x.shape
    depth = w_in.shape[0]
    t = batch * seq
    xt = x.reshape(t, d)

    rot_a = DIFF_DIM // 4
    tab_a = _rope_tables(seq, ((0, rot_a), (DIFF_DIM, rot_a)))
    tab_b = _rope_tables(seq, ((0, ROPE_DIM_B),))
    tab_c = _rope_tables(seq, ((0, HEAD_DIM // 4),))

    for layer in range(depth):
        proj = _inproj(xt, norm_mix[layer], _relayout_w_in(w_in, layer))

        lam_init = 0.8 - 0.6 * math.exp(-0.3 * layer)
        lam_rows = jnp.stack([lam_q1[layer], lam_k1[layer], lam_q2[layer], lam_k2[layer]])
        y_a = _diff_attention(proj, lam_rows, diff_norm[layer], tab_a,
                              batch=batch, seq=seq, lam_init=lam_init)

        wuq, wuk, wuv = _relayout_mla(w_uq[layer], w_ukv[layer])
        q_b, k_b, v_b = _mla_prep(proj, dq_norm[layer], dkv_norm[layer], wuq, wuk, wuv, tab_b, seq=seq)
        y_b = _mla_attention(q_b, k_b, v_b, batch=batch, seq=seq)

        y_c = _dilated_attention(proj, tab_c, batch=batch, seq=seq)
        y_d = _stick_attention(proj, batch=batch, seq=seq)

        merged = _merge((y_a, y_b, y_c, y_d), proj, w_branch[layer].astype(BF16))
        xt = _outproj(xt, merged, w_out[layer].astype(BF16))

        last = layer == depth - 1
        if layer % 2 == 0:
            m = layer // 2
            xt = _swiglu_rows(xt, norm_ffn[layer], _dense_visits(t, DENSE_CHUNK, DENSE_VISIT_CHUNKS),
                              w_dense_in[m:m + 1], w_dense_out[m:m + 1], dense=True,
                              ch=DENSE_CHUNK, max_chunks=DENSE_VISIT_CHUNKS)
            if last:
                xt = _final_norm(xt, norm_final)
        else:
            m = layer // 2
            idx, gate = _router(xt, norm_ffn[layer], w_router[m])
            tables, dest = _routing_tables(idx, MOE_CHUNK, MOE_VISIT_CHUNKS)
            y_rows = _swiglu_rows(xt, norm_ffn[layer], tables, w_moe_in[m], w_moe_out[m], dense=False,
                                  ch=MOE_CHUNK, max_chunks=MOE_VISIT_CHUNKS)
            xt = _moe_combine(xt, gate, y_rows, dest, norm_final if last else None)
    return xt.reshape(batch, seq, d)
```

```python
import functools
import math

import numpy as np
import jax
import jax.numpy as jnp
from jax import lax
from jax.experimental import pallas as pl
from jax.experimental.pallas import tpu as pltpu

F32 = jnp.float32
BF16 = jnp.bfloat16

HEAD_DIM = 128
ROPE_THETA = 500000.0
NORM_EPS = 1e-6
N_BRANCH = 4
BRANCH_WIDTH = 512
N_HEADS = 4
DIFF_DIM = 64
Q_LORA = 512
KV_LORA = 512
NOPE_DIM = 128
ROPE_DIM_B = 64
MLA_QK = 256
DIL_PAIRS = ((128, 1), (512, 4), (2048, 16))
N_DIL_GROUPS = 3
N_EXPERTS = 8
TOP_K = 2
LANES = 128
LOG2_E = math.log2(math.e)

A_OFF = 0
B_OFF = 1536
C_OFF = 3072
D_OFF = 7680
QKV_COLS = 9216

VMEM_LIMIT = 56 * 1024 * 1024


def _cparams(sem, vmem=VMEM_LIMIT):
    return pltpu.CompilerParams(dimension_semantics=sem, vmem_limit_bytes=vmem)


def _dot(a, b):
    return jnp.dot(a, b, preferred_element_type=F32)


def _dot_nt(a, b):
    return lax.dot_general(a, b, (((1,), (1,)), ((), ())), preferred_element_type=F32)


def _rms(x, gain):
    return x * lax.rsqrt(jnp.mean(x * x, axis=-1, keepdims=True) + NORM_EPS) * gain


def _sigmoid(x):
    return 1.0 / (1.0 + jnp.exp(-x))


def _rope_tables(seq, segments):
    pos = np.arange(seq, dtype=np.float64)
    c = np.ones((seq, LANES), np.float64)
    s = np.zeros((seq, LANES), np.float64)
    r = np.zeros((LANES, LANES), np.float32)
    for start, rot in segments:
        half = rot // 2
        inv_freq = ROPE_THETA ** (-np.arange(0, rot, 2, dtype=np.float64) / rot)
        ang = pos[:, None] * inv_freq[None, :]
        cos, sin = np.cos(ang), np.sin(ang)
        c[:, start:start + half] = cos
        c[:, start + half:start + rot] = cos
        s[:, start:start + half] = -sin
        s[:, start + half:start + rot] = sin
        for i in range(half):
            r[start + half + i, start + i] = 1.0
            r[start + i, start + half + i] = 1.0
    return jnp.asarray(c, F32), jnp.asarray(s, F32), jnp.asarray(r, BF16)


def _rope(x_bf16, rot, c, s):
    return x_bf16.astype(F32) * c + _dot(x_bf16, rot) * s


def _norm_kernel(x_ref, g_ref, o_ref):
    o_ref[...] = _rms(x_ref[...], g_ref[...]).astype(o_ref.dtype)


def _rms_norm_bf16(x, gain, *, tm=512):
    t, d = x.shape
    return pl.pallas_call(
        _norm_kernel,
        out_shape=jax.ShapeDtypeStruct((t, d), BF16),
        grid=(t // tm,),
        in_specs=[pl.BlockSpec((tm, d), lambda i: (i, 0)), pl.BlockSpec((1, d), lambda i: (0, 0))],
        out_specs=pl.BlockSpec((tm, d), lambda i: (i, 0)),
        compiler_params=_cparams(("parallel",)),
        name="mix_norm",
    )(x, gain.reshape(1, d))


B_END = B_OFF + Q_LORA + KV_LORA + ROPE_DIM_B


def _inproj_kernel(h_ref, *refs, gate_block):
    w_refs, o_ref = refs[:-1], refs[-1]
    j = pl.program_id(1)
    h = h_ref[...]
    for q, w_ref in enumerate(w_refs):
        n = w_ref.shape[1]
        acc = _dot_nt(h, w_ref[0].astype(BF16))
        o_ref[:, q * n:(q + 1) * n] = jnp.where(j >= gate_block, _sigmoid(acc), acc).astype(o_ref.dtype)


def _inproj(h, w_t, layer, *, tm=2048, tn=1024):
    t, d = h.shape
    n_out = w_t.shape[1] + C_OFF - B_END
    assert C_OFF % tn == 0 and n_out % tn == 0
    ns = WEIGHT_SPLIT

    unit = math.gcd(tn, C_OFF - B_END)

    def first_col(j, q):
        k = j * (tn // unit) + q * (tn // ns // unit)
        return unit * jnp.where(j * tn < C_OFF, k, k - (C_OFF - B_END) // unit)

    return pl.pallas_call(
        functools.partial(_inproj_kernel, gate_block=QKV_COLS // tn),
        out_shape=jax.ShapeDtypeStruct((t, n_out), BF16),
        grid=(t // tm, n_out // tn),
        in_specs=[pl.BlockSpec((tm, d), lambda i, j: (i, 0))]
                 + [pl.BlockSpec((pl.Element(1), pl.Element(tn // ns), pl.Element(d)),
                                 functools.partial(
                                     lambda i, j, q: (layer, first_col(j, q), 0), q=q))
                    for q in range(ns)],
        out_specs=pl.BlockSpec((tm, tn), lambda i, j: (i, j)),
        compiler_params=_cparams(("parallel", "arbitrary")),
        name="inproj",
    )(h, *([w_t] * ns))


def _osm(scores, values, carries):
    stats = []
    for s, (m, l, _) in zip(scores, carries):
        m_new = jnp.maximum(m, jnp.max(s, axis=-1, keepdims=True))
        alpha = jnp.exp2(m - m_new)
        p = jnp.exp2(s - m_new)
        stats.append((m_new, alpha * l + jnp.sum(p, axis=-1, keepdims=True), alpha, p))
    return tuple((m_new, l, alpha * acc + _dot(p.astype(BF16), v))
                 for (m_new, l, alpha, p), v, (_, _, acc) in zip(stats, values, carries))


def _causal_keep(t):
    r = lax.broadcasted_iota(jnp.int32, (t, t), 0)
    c = lax.broadcasted_iota(jnp.int32, (t, t), 1)
    return c <= r


def _diff_kernel(lam_ref, gain_ref, rot_ref, cq_ref, sq_ref, ck_ref, sk_ref,
                 q_ref, k_ref, v_ref, o_ref, kr_ref, *, tq, hp, lam_init):
    qi = pl.program_id(2)
    rot = rot_ref[...]
    heads = [slice(h * HEAD_DIM, (h + 1) * HEAD_DIM) for h in range(hp)]

    @pl.when(qi == 0)
    def _():
        for hs in heads:
            kr_ref[:, hs] = _rope(k_ref[:, hs], rot, ck_ref[...], sk_ref[...]).astype(BF16)

    scale2 = LOG2_E / math.sqrt(DIFF_DIM)
    lane = lax.broadcasted_iota(jnp.int32, (tq, HEAD_DIM), 1)
    chains = []
    for hs in heads:
        qf = _rope(q_ref[:, hs], rot, cq_ref[...], sq_ref[...])
        chains.append((jnp.where(lane < DIFF_DIM, qf, 0.0).astype(BF16), hs))
        chains.append((jnp.where(lane >= DIFF_DIM, qf, 0.0).astype(BF16), hs))

    def step(j, carry, masked):
        off = pl.multiple_of(j * tq, tq)
        scores = [_dot_nt(q, kr_ref[pl.ds(off, tq), hs]) * scale2 for q, hs in chains]
        if masked:
            keep = _causal_keep(tq)
            scores = [jnp.where(keep, s, -jnp.inf) for s in scores]
        return _osm(scores, [v_ref[pl.ds(off, tq), hs] for _, hs in chains], carry)

    init = (jnp.full((tq, 1), -jnp.inf, F32), jnp.zeros((tq, 1), F32),
            jnp.zeros((tq, HEAD_DIM), F32))
    carry = lax.fori_loop(0, qi, lambda j, c: step(j, c, False), tuple(init for _ in chains))
    carry = step(qi, carry, True)

    lam_rows = lam_ref[...]
    lam = (jnp.exp(jnp.sum(lam_rows[0:1] * lam_rows[1:2], axis=-1, keepdims=True))
           - jnp.exp(jnp.sum(lam_rows[2:3] * lam_rows[3:4], axis=-1, keepdims=True))
           + lam_init)
    for i, hs in enumerate(heads):
        (_, l1, a1), (_, l2, a2) = carry[2 * i], carry[2 * i + 1]
        out = a1 / l1 - lam * (a2 / l2)
        o_ref[:, hs] = (_rms(out, gain_ref[...]) * (1.0 - lam_init)).astype(o_ref.dtype)


def _diff_attention(proj, lam_rows, gain, tables, *, batch, seq, lam_init, tq=256, hp=4):
    c, s, rot = tables
    nq = seq // tq
    w = hp * HEAD_DIM
    cb = A_OFF // w
    ng = N_HEADS // hp
    return pl.pallas_call(
        functools.partial(_diff_kernel, tq=tq, hp=hp, lam_init=lam_init),
        out_shape=jax.ShapeDtypeStruct((batch * seq, BRANCH_WIDTH), BF16),
        grid=(batch, ng, nq),
        in_specs=[pl.BlockSpec((4, DIFF_DIM), lambda b, h, i: (0, 0)),
                  pl.BlockSpec((1, HEAD_DIM), lambda b, h, i: (0, 0)),
                  pl.BlockSpec((LANES, LANES), lambda b, h, i: (0, 0)),
                  pl.BlockSpec((tq, LANES), lambda b, h, i: (i, 0)),
                  pl.BlockSpec((tq, LANES), lambda b, h, i: (i, 0)),
                  pl.BlockSpec((seq, LANES), lambda b, h, i: (0, 0)),
                  pl.BlockSpec((seq, LANES), lambda b, h, i: (0, 0)),
                  pl.BlockSpec((tq, w), lambda b, h, i: (b * nq + i, cb + h)),
                  pl.BlockSpec((seq, w), lambda b, h, i: (b, cb + ng + h)),
                  pl.BlockSpec((seq, w), lambda b, h, i: (b, cb + 2 * ng + h))],
        out_specs=pl.BlockSpec((tq, w), lambda b, h, i: (b * nq + i, h)),
        scratch_shapes=[pltpu.VMEM((seq, w), BF16)],
        compiler_params=_cparams(("parallel", "parallel", "arbitrary")),
        name="diff_attention",
    )(lam_rows, gain.reshape(1, HEAD_DIM), rot, c, s, c, s, proj, proj, proj)


def _mla_prep_kernel(cq_ref, ckv_ref, kr_ref, dqn_ref, dkvn_ref, wuq_ref, wuk_ref, wuv_ref,
                     rot_ref, c_ref, s_ref, q_out, k_out, v_out):
    rot = rot_ref[...]
    c = c_ref[...]
    s = s_ref[...]
    hq = _rms(cq_ref[...].astype(F32), dqn_ref[...]).astype(BF16)
    hkv = _rms(ckv_ref[...].astype(F32), dkvn_ref[...]).astype(BF16)
    q = _dot(hq, wuq_ref[...])
    kn = _dot(hkv, wuk_ref[...])
    v_out[...] = _dot(hkv, wuv_ref[...]).astype(BF16)
    kr_lane = lax.broadcasted_iota(jnp.int32, kr_ref.shape, 1)
    kr = jnp.where(kr_lane < ROPE_DIM_B, kr_ref[...], jnp.zeros_like(kr_ref[...]))
    k_rope = _rope(kr, rot, c, s).astype(BF16)
    for h in range(N_HEADS):
        lo = h * MLA_QK
        q_out[:, lo:lo + NOPE_DIM] = q[:, lo:lo + NOPE_DIM].astype(BF16)
        q_out[:, lo + NOPE_DIM:lo + MLA_QK] = _rope(
            q[:, lo + NOPE_DIM:lo + MLA_QK].astype(BF16), rot, c, s).astype(BF16)
        k_out[:, lo:lo + NOPE_DIM] = kn[:, h * NOPE_DIM:(h + 1) * NOPE_DIM].astype(BF16)
        k_out[:, lo + NOPE_DIM:lo + MLA_QK] = k_rope


def _mla_prep(proj, dq_norm, dkv_norm, wuq, wuk, wuv, tables, *, seq, tm=512):
    c, s, rot = tables
    t = proj.shape[0]
    nb = seq // tm
    cb = B_OFF // Q_LORA
    full = lambda shape: pl.BlockSpec(shape, lambda i: (0, 0))
    return pl.pallas_call(
        _mla_prep_kernel,
        out_shape=(jax.ShapeDtypeStruct((t, N_HEADS * MLA_QK), BF16),
                   jax.ShapeDtypeStruct((t, N_HEADS * MLA_QK), BF16),
                   jax.ShapeDtypeStruct((t, N_HEADS * HEAD_DIM), BF16)),
        grid=(t // tm,),
        in_specs=[pl.BlockSpec((tm, Q_LORA), lambda i: (i, cb)),
                  pl.BlockSpec((tm, KV_LORA), lambda i: (i, cb + 1)),
                  pl.BlockSpec((tm, LANES), lambda i: (i, (B_OFF + Q_LORA + KV_LORA) // LANES)),
                  full((1, Q_LORA)), full((1, KV_LORA)),
                  full(wuq.shape), full(wuk.shape), full(wuv.shape),
                  full((LANES, LANES)),
                  pl.BlockSpec((tm, LANES), lambda i: (i % nb, 0)),
                  pl.BlockSpec((tm, LANES), lambda i: (i % nb, 0))],
        out_specs=(pl.BlockSpec((tm, N_HEADS * MLA_QK), lambda i: (i, 0)),
                   pl.BlockSpec((tm, N_HEADS * MLA_QK), lambda i: (i, 0)),
                   pl.BlockSpec((tm, N_HEADS * HEAD_DIM), lambda i: (i, 0))),
        compiler_params=_cparams(("parallel",)),
        name="mla_prep",
    )(proj, proj, proj, dq_norm.reshape(1, Q_LORA), dkv_norm.reshape(1, KV_LORA),
      wuq, wuk, wuv, rot, c, s)


def _flash_kernel(q_ref, k_ref, v_ref, o_ref, *, tq, hp, dqk, scale):
    qi = pl.program_id(2)
    qk = [slice(h * dqk, (h + 1) * dqk) for h in range(hp)]
    hv = [slice(h * HEAD_DIM, (h + 1) * HEAD_DIM) for h in range(hp)]
    qs = [q_ref[:, sl] for sl in qk]

    def step(j, carry, masked):
        off = pl.multiple_of(j * tq, tq)
        scores = [_dot_nt(q, k_ref[pl.ds(off, tq), ks]) * (scale * LOG2_E) for q, ks in zip(qs, qk)]
        if masked:
            keep = _causal_keep(tq)
            scores = [jnp.where(keep, s, -jnp.inf) for s in scores]
        return _osm(scores, [v_ref[pl.ds(off, tq), vs] for vs in hv], carry)

    init = (jnp.full((tq, 1), -jnp.inf, F32), jnp.zeros((tq, 1), F32),
            jnp.zeros((tq, HEAD_DIM), F32))
    carry = lax.fori_loop(0, qi, lambda j, c: step(j, c, False), tuple(init for _ in qs))
    carry = step(qi, carry, True)
    for vs, (_, l, acc) in zip(hv, carry):
        o_ref[:, vs] = (acc / l).astype(o_ref.dtype)


def _mla_attention(q, k, v, *, batch, seq, tq=256, hp=4):
    nq = seq // tq
    ng = N_HEADS // hp
    return pl.pallas_call(
        functools.partial(_flash_kernel, tq=tq, hp=hp, dqk=MLA_QK,
                          scale=1.0 / math.sqrt(NOPE_DIM + ROPE_DIM_B)),
        out_shape=jax.ShapeDtypeStruct((batch * seq, BRANCH_WIDTH), BF16),
        grid=(batch, ng, nq),
        in_specs=[pl.BlockSpec((tq, hp * MLA_QK), lambda b, h, i: (b * nq + i, h)),
                  pl.BlockSpec((seq, hp * MLA_QK), lambda b, h, i: (b, h)),
                  pl.BlockSpec((seq, hp * HEAD_DIM), lambda b, h, i: (b, h))],
        out_specs=pl.BlockSpec((tq, hp * HEAD_DIM), lambda b, h, i: (b * nq + i, h)),
        compiler_params=_cparams(("parallel", "parallel", "arbitrary")),
        name="mla_attention",
    )(q, k, v)


DIL_BLOCK = 128


def _dilated_kernel(rot_ref, c_ref, s_ref, *refs, seq):
    in_refs = refs[:9]
    o_ref = refs[9]
    qf, kf, vf, og, lse = refs[10:]
    rot = rot_ref[...]
    c = c_ref[...]
    s = s_ref[...]
    for g in range(N_DIL_GROUPS):
        qf[g] = _rope(in_refs[g][...], rot, c, s)
        kf[g] = _rope(in_refs[3 + g][...], rot, c, s)
        vf[g] = in_refs[6 + g][...].astype(F32)
    scale = 1.0 / math.sqrt(HEAD_DIM)
    blk = DIL_BLOCK

    def rows(start, size, stride):
        return pl.ds(start, size) if stride == 1 else pl.ds(start, size, stride=stride)

    for g, (window, dil) in enumerate(DIL_PAIRS):
        assert window == blk * dil
        span = blk * dil
        n_sub = seq // span
        nk = 2 * blk if n_sub > 1 else blk
        assert n_sub & (n_sub - 1) == 0
        q_rows, k_rows = [], []
        for r in range(dil):
            for cb in range(n_sub):
                q_rows.append(rows(r + cb * span, blk, dil))
                k_rows.append(rows(r + max(cb - 1, 0) * span, nk, dil))
        nb = len(q_rows)
        q = jnp.stack([qf[g, qr, :] for qr in q_rows]).astype(BF16)
        k = jnp.stack([kf[g, kr, :] for kr in k_rows]).astype(BF16)
        v = jnp.stack([vf[g, kr, :] for kr in k_rows]).astype(BF16)
        sc = jnp.einsum("bqd,bkd->bqk", q, k, preferred_element_type=F32) * (scale * LOG2_E)
        first = (lax.broadcasted_iota(jnp.int32, (nb, blk, nk), 0) & (n_sub - 1)) == 0
        dist = (lax.broadcasted_iota(jnp.int32, (nb, blk, nk), 1)
                - lax.broadcasted_iota(jnp.int32, (nb, blk, nk), 2)
                + jnp.where(first, 0, nk - blk))
        sc = jnp.where(dist >= 0, jnp.where(dist <= blk, sc, -jnp.inf), -jnp.inf)
        m = jnp.max(sc, axis=-1, keepdims=True)
        e = jnp.exp2(sc - m)
        den = jnp.sum(e, axis=-1, keepdims=True)
        o = jnp.einsum("bqk,bkd->bqd", e.astype(BF16), v, preferred_element_type=F32) / den
        lg = jnp.broadcast_to(m + jnp.log2(den), (nb, blk, HEAD_DIM))
        for i, qr in enumerate(q_rows):
            og[g, qr, :] = o[i]
            lse[g, qr, :] = lg[i]

    l0, l1, l2 = lse[0], lse[1], lse[2]
    mx = jnp.maximum(jnp.maximum(l0, l1), l2)
    w0, w1, w2 = jnp.exp2(l0 - mx), jnp.exp2(l1 - mx), jnp.exp2(l2 - mx)
    o_ref[...] = ((w0 * og[0] + w1 * og[1] + w2 * og[2]) / (w0 + w1 + w2)).astype(o_ref.dtype)


def _dilated_attention(proj, tables, *, batch, seq):
    c, s, rot = tables
    cb = C_OFF // HEAD_DIM
    nh = N_DIL_GROUPS * N_HEADS

    def col(kind, g):
        return lambda b, h: (b, cb + kind * nh + g * N_HEADS + h)

    in_specs = [pl.BlockSpec((LANES, LANES), lambda b, h: (0, 0)),
                pl.BlockSpec((seq, LANES), lambda b, h: (0, 0)),
                pl.BlockSpec((seq, LANES), lambda b, h: (0, 0))]
    for kind in range(3):
        for g in range(N_DIL_GROUPS):
            in_specs.append(pl.BlockSpec((seq, HEAD_DIM), col(kind, g)))
    return pl.pallas_call(
        functools.partial(_dilated_kernel, seq=seq),
        out_shape=jax.ShapeDtypeStruct((batch * seq, BRANCH_WIDTH), BF16),
        grid=(batch, N_HEADS),
        in_specs=in_specs,
        out_specs=pl.BlockSpec((seq, HEAD_DIM), lambda b, h: (b, h)),
        scratch_shapes=[pltpu.VMEM((N_DIL_GROUPS, seq, HEAD_DIM), F32) for _ in range(5)],
        compiler_params=_cparams(("parallel", "parallel")),
        name="dilated_attention",
    )(rot, c, s, *([proj] * 9))


def _stick_kernel(q_ref, k_ref, v_ref, o_ref, *, tq, hp, scale):
    qi = pl.program_id(2)
    heads = [slice(h * HEAD_DIM, (h + 1) * HEAD_DIM) for h in range(hp)]
    qs = [q_ref[:, hs] for hs in heads]
    r = lax.broadcasted_iota(jnp.int32, (tq, tq), 0)
    c = lax.broadcasted_iota(jnp.int32, (tq, tq), 1)
    later_keys = jnp.where(r > c, 1.0, 0.0).astype(BF16)

    def step(j, carry, diag):
        off = pl.multiple_of(j * tq, tq)
        strict = c < r
        z2 = [_dot_nt(q, k_ref[pl.ds(off, tq), hs]) * (scale * LOG2_E) for q, hs in zip(qs, heads)]
        sp2 = [jnp.maximum(z, 0.0) + jnp.log2(1.0 + jnp.exp2(-jnp.abs(z))) for z in z2]
        log_not = [jnp.where(strict, -sp, 0.0) if diag else -sp for sp in sp2]
        later = []
        for ln, (tail, _) in zip(log_not, carry):
            hi = ln.astype(BF16)
            lo = (ln - hi.astype(F32)).astype(BF16)
            later.append(_dot(hi, later_keys) + _dot(lo, later_keys) + tail)
        a = [jnp.exp2((z - sp) + lt) for z, sp, lt in zip(z2, sp2, later)]
        if diag:
            a = [jnp.where(strict, x, 0.0) for x in a]
        out = []
        for x, ln, hs, (tail, acc) in zip(a, log_not, heads, carry):
            acc = acc + _dot(x.astype(BF16), v_ref[pl.ds(off, tq), hs])
            out.append((tail + jnp.sum(ln, axis=-1, keepdims=True), acc))
        return tuple(out)

    init = (jnp.zeros((tq, 1), F32), jnp.zeros((tq, HEAD_DIM), F32))
    carry = step(qi, tuple(init for _ in heads), True)
    carry = lax.fori_loop(0, qi, lambda t, cr: step(qi - 1 - t, cr, False), carry)
    for hs, (_, acc) in zip(heads, carry):
        o_ref[:, hs] = acc.astype(o_ref.dtype)


def _stick_attention(proj, *, batch, seq, tq=256, hp=4):
    nq = seq // tq
    w = hp * HEAD_DIM
    cb = D_OFF // w
    ng = N_HEADS // hp
    return pl.pallas_call(
        functools.partial(_stick_kernel, tq=tq, hp=hp, scale=1.0 / math.sqrt(HEAD_DIM)),
        out_shape=jax.ShapeDtypeStruct((batch * seq, BRANCH_WIDTH), BF16),
        grid=(batch, ng, nq),
        in_specs=[pl.BlockSpec((tq, w), lambda b, h, i: (b * nq + i, cb + h)),
                  pl.BlockSpec((seq, w), lambda b, h, i: (b, cb + ng + h)),
                  pl.BlockSpec((seq, w), lambda b, h, i: (b, cb + 2 * ng + h))],
        out_specs=pl.BlockSpec((tq, w), lambda b, h, i: (b * nq + i, h)),
        compiler_params=_cparams(("parallel", "parallel", "arbitrary")),
        name="stick_attention",
    )(proj, proj, proj)


def _merge_kernel(ya, yb, yc, yd, g0, g1, g2, g3, wb_ref, o_ref):
    acc = None
    for i, (y, g) in enumerate(((ya, g0), (yb, g1), (yc, g2), (yd, g3))):
        t = g[...].astype(F32) * _dot(y[...], wb_ref[i])
        acc = t if acc is None else acc + t
    o_ref[...] = acc.astype(o_ref.dtype)


def _merge(ys, proj, wb, *, tm=1024, tn=1024):
    t = proj.shape[0]
    d = wb.shape[2]
    nn = d // tn
    g0 = QKV_COLS // tn
    y_spec = pl.BlockSpec((tm, BRANCH_WIDTH), lambda i, j: (i, 0))
    g_specs = [pl.BlockSpec((tm, tn), functools.partial(lambda i, j, b: (i, g0 + b * nn + j), b=b))
               for b in range(N_BRANCH)]
    return pl.pallas_call(
        _merge_kernel,
        out_shape=jax.ShapeDtypeStruct((t, d), BF16),
        grid=(t // tm, nn),
        in_specs=[y_spec] * N_BRANCH + g_specs
                 + [pl.BlockSpec((N_BRANCH, BRANCH_WIDTH, tn), lambda i, j: (0, 0, j))],
        out_specs=pl.BlockSpec((tm, tn), lambda i, j: (i, j)),
        compiler_params=_cparams(("parallel", "arbitrary")),
        name="branch_merge",
    )(*ys, proj, proj, proj, proj, wb)


def _outproj_kernel(x_ref, m_ref, w_ref, o_ref):
    o_ref[...] = x_ref[...] + _dot(m_ref[...], w_ref[...])


def _outproj(x, merged, w, *, tm=1024, tn=1024):
    t, d = x.shape
    return pl.pallas_call(
        _outproj_kernel,
        out_shape=jax.ShapeDtypeStruct((t, d), F32),
        grid=(t // tm, d // tn),
        in_specs=[pl.BlockSpec((tm, tn), lambda i, j: (i, j)),
                  pl.BlockSpec((tm, d), lambda i, j: (i, 0)),
                  pl.BlockSpec((d, tn), lambda i, j: (0, j))],
        out_specs=pl.BlockSpec((tm, tn), lambda i, j: (i, j)),
        compiler_params=_cparams(("parallel", "arbitrary")),
        name="out_proj",
    )(x, merged, w)


def _router_kernel(x_ref, gain_ref, wr_ref, idx_ref, gate_ref):
    h = _rms(x_ref[...], gain_ref[...])
    logits = jnp.dot(h, wr_ref[...], preferred_element_type=F32, precision=lax.Precision.HIGHEST)
    lane = lax.broadcasted_iota(jnp.int32, logits.shape, 1)
    lanef = lane.astype(F32)
    lg = jnp.where(lane < N_EXPERTS, logits, -jnp.inf)
    v1 = jnp.max(lg, axis=-1, keepdims=True)
    i1 = jnp.min(jnp.where(lg == v1, lanef, float(LANES)), axis=-1, keepdims=True)
    lg2 = jnp.where(lanef == i1, -jnp.inf, lg)
    v2 = jnp.max(lg2, axis=-1, keepdims=True)
    i2 = jnp.min(jnp.where(lg2 == v2, lanef, float(LANES)), axis=-1, keepdims=True)
    e2 = jnp.exp(v2 - v1)
    g1 = 1.0 / (1.0 + e2)
    g2 = e2 / (1.0 + e2)
    idx_ref[...] = jnp.where(lane == 0, i1, jnp.where(lane == 1, i2, 0.0)).astype(jnp.int32)
    gate_ref[...] = jnp.where(lane == 0, g1, jnp.where(lane == 1, g2, 0.0))


def _router(x, gain, w_router, *, tm=512):
    t, d = x.shape
    wr = jnp.zeros((d, LANES), F32).at[:, :N_EXPERTS].set(w_router)
    return pl.pallas_call(
        _router_kernel,
        out_shape=(jax.ShapeDtypeStruct((t, LANES), jnp.int32),
                   jax.ShapeDtypeStruct((t, LANES), F32)),
        grid=(t // tm,),
        in_specs=[pl.BlockSpec((tm, d), lambda i: (i, 0)),
                  pl.BlockSpec((1, d), lambda i: (0, 0)),
                  pl.BlockSpec((d, LANES), lambda i: (0, 0))],
        out_specs=(pl.BlockSpec((tm, LANES), lambda i: (i, 0)),
                   pl.BlockSpec((tm, LANES), lambda i: (i, 0))),
        compiler_params=_cparams(("parallel",)),
        name="router",
    )(x, gain.reshape(1, d), wr)


def _start_row_gather(idx_ref, base, n, src_hbm, dst, sem):
    def start(r, _):
        row = idx_ref[base + r]
        pltpu.make_async_copy(src_hbm.at[pl.ds(row, 1), :], dst.at[pl.ds(r, 1), :], sem).start()
        return 0

    lax.fori_loop(0, n, start, 0, unroll=8)


def _wait_row_gather(n, src_hbm, dst, sem):
    pltpu.make_async_copy(src_hbm.at[pl.ds(0, n), :], dst.at[pl.ds(0, n), :], sem).wait()


MOE_CHUNK = 512
MOE_VISIT_CHUNKS = 5
DENSE_CHUNK = 512
DENSE_VISIT_CHUNKS = 5
WEIGHT_SPLIT = 4


def _moe_kernel(vis_e_ref, vis_ok_ref, vis_row_ref, vis_nch_ref, tail_ref, row_tok_ref,
                x_hbm, gain_ref, *refs, ch, dense):
    del vis_e_ref, vis_ok_ref
    ns = WEIGHT_SPLIT
    wg_refs, wu_refs, wo_refs = refs[:ns], refs[ns:2 * ns], refs[2 * ns:3 * ns]
    y_hbm, stage, h_ref, acc_ref, gsem, osem = refs[3 * ns:]
    v = pl.program_id(0)
    f = pl.program_id(1)
    nv = pl.num_programs(0)
    nf = pl.num_programs(1)
    nch = vis_nch_ref[v]
    row0 = vis_row_ref[v]
    active = nch > 0
    kq = h_ref.shape[1] // ns

    def rows(c):
        return pl.ds(pl.multiple_of(c * ch, ch), ch)

    def step(h):
        parts = [h[:, q * kq:(q + 1) * kq] for q in range(ns)]
        g = sum(_dot(hq, w[...].astype(BF16)) for hq, w in zip(parts, wg_refs))
        u = sum(_dot(hq, w[...].astype(BF16)) for hq, w in zip(parts, wu_refs))
        a = (g * _sigmoid(g) * u).astype(BF16)
        return jnp.concatenate([_dot(a, w[...].astype(BF16)) for w in wo_refs], axis=1)

    def out_copy(c):
        dst = y_hbm.at[pl.ds(pl.multiple_of(row0 + c * ch, ch), ch), :]
        return pltpu.make_async_copy(acc_ref.at[rows(c), :], dst, osem)

    @pl.when(jnp.logical_and(active, f == 0))
    def _():
        def fetch(c, slot):
            if dense:
                src = x_hbm.at[pl.ds(pl.multiple_of(row0 + c * ch, ch), ch), :]
                pltpu.make_async_copy(src, stage.at[slot], gsem.at[slot]).start()
            else:
                _start_row_gather(row_tok_ref, row0 + c * ch, ch, x_hbm, stage.at[slot], gsem.at[slot])

        fetch(0, 0)

        def body(c, _):
            slot = lax.rem(c, 2)

            @pl.when(c + 1 < nch)
            def _():
                fetch(c + 1, 1 - slot)

            _wait_row_gather(ch, x_hbm, stage.at[slot], gsem.at[slot])
            x = stage[slot]
            h = _rms(x, gain_ref[...]).astype(BF16)
            h_ref[rows(c), :] = h
            acc_ref[rows(c), :] = x + step(h) if dense else step(h)
            return 0

        lax.fori_loop(0, nch, body, 0)

    @pl.when(jnp.logical_and(active, jnp.logical_and(f > 0, f < nf - 1)))
    def _():
        def body(c, _):
            acc_ref[rows(c), :] += step(h_ref[rows(c), :])
            return 0

        lax.fori_loop(0, nch, body, 0)

    @pl.when(jnp.logical_and(active, f == nf - 1))
    def _():
        def body(c, _):
            acc_ref[rows(c), :] += step(h_ref[rows(c), :])
            out_copy(c).start()
            return 0

        def drain(c, _):
            out_copy(c).wait()
            return 0

        lax.fori_loop(0, nch, body, 0)
        lax.fori_loop(0, nch, drain, 0)

    @pl.when(jnp.logical_and(v == nv - 1, f == nf - 1))
    def _():
        first = tail_ref[0]
        n_tail = y_hbm.shape[0] // ch - first
        stage[0] = jnp.zeros(stage.shape[1:], stage.dtype)

        def tail_copy(c):
            dst = y_hbm.at[pl.ds(pl.multiple_of((first + c) * ch, ch), ch), :]
            return pltpu.make_async_copy(stage.at[0], dst, osem)

        def fill(c, _):
            tail_copy(c).start()
            return 0

        def drain(c, _):
            tail_copy(c).wait()
            return 0

        lax.fori_loop(0, n_tail, fill, 0)
        lax.fori_loop(0, n_tail, drain, 0)


def _dense_visits(t, ch, max_chunks):
    span = ch * max_chunks
    row0 = np.arange(0, t, span, dtype=np.int32)
    n_ch = np.minimum(max_chunks, (t - row0) // ch).astype(np.int32)
    zeros = np.zeros_like(row0)
    return tuple(jnp.asarray(a) for a in (zeros, zeros + 1, row0, n_ch,
                                          np.array([t // ch], np.int32), np.zeros((1,), np.int32)))


def _swiglu_rows(x, gain, tables, w_in, w_out, *, dense, ch, max_chunks, tf=256):
    vis_e, vis_ok, vis_row, vis_nch, tail, row_tok = tables
    t, d = x.shape
    ff = w_out.shape[1]
    nf = ff // tf
    p = t if dense else row_tok.shape[0]
    ns = WEIGHT_SPLIT

    def wspec(shape, index):
        def index_map(v, f, e, ok, *_):
            return index(e[v], jnp.where(ok[v] > 0, f, nf - 1))
        return pl.BlockSpec(shape, index_map)

    wg_specs = [wspec((None, d // ns, tf), functools.partial(lambda e, f, q: (e, q, f), q=q))
                for q in range(ns)]
    wu_specs = [wspec((None, d // ns, tf), functools.partial(lambda e, f, q: (e, q, nf + f), q=q))
                for q in range(ns)]
    wo_specs = [wspec((None, tf, d // ns), functools.partial(lambda e, f, q: (e, f, q), q=q))
                for q in range(ns)]
    grid_spec = pltpu.PrefetchScalarGridSpec(
        num_scalar_prefetch=6,
        grid=(vis_e.shape[0], nf),
        in_specs=[pl.BlockSpec(memory_space=pl.ANY),
                  pl.BlockSpec((1, d), lambda v, f, *_: (0, 0))] + wg_specs + wu_specs + wo_specs,
        out_specs=pl.BlockSpec(memory_space=pl.ANY),
        scratch_shapes=[pltpu.VMEM((2, ch, d), F32),
                        pltpu.VMEM((max_chunks * ch, d), BF16),
                        pltpu.VMEM((max_chunks * ch, d), F32),
                        pltpu.SemaphoreType.DMA((2,)), pltpu.SemaphoreType.DMA],
    )
    return pl.pallas_call(
        functools.partial(_moe_kernel, ch=ch, dense=dense),
        out_shape=jax.ShapeDtypeStruct((p, d), F32),
        grid_spec=grid_spec,
        compiler_params=_cparams(("arbitrary", "arbitrary"), vmem=60 * 1024 * 1024),
        name="dense_ffn" if dense else "moe_experts",
    )(vis_e, vis_ok, vis_row, vis_nch, tail, row_tok, x, gain.reshape(1, d),
      *([w_in] * (2 * ns)), *([w_out] * ns))


def _combine_kernel(pos_ref, x_ref, gate_ref, y_hbm, gain_ref, o_ref, buf, sem, *, tm):
    i = pl.program_id(0)
    n = pl.num_programs(0)
    t = pos_ref.shape[0] // TOP_K
    slot = lax.rem(i, 2)

    def start(step, sl):
        for k in range(TOP_K):
            _start_row_gather(pos_ref, k * t + step * tm, tm, y_hbm, buf.at[sl, k], sem.at[sl, k])

    @pl.when(i == 0)
    def _():
        start(0, 0)

    @pl.when(i + 1 < n)
    def _():
        start(i + 1, 1 - slot)

    for k in range(TOP_K):
        _wait_row_gather(tm, y_hbm, buf.at[slot, k], sem.at[slot, k])
    g = gate_ref[...]
    x = x_ref[...] + (g[:, 0:1] * buf[slot, 0] + g[:, 1:2] * buf[slot, 1])
    if gain_ref is None:
        o_ref[...] = x
    else:
        o_ref[...] = _rms(x, gain_ref[...])


def _moe_combine(x, gate, y_rows, pos, final_gain, *, tm=256):
    t, d = x.shape
    in_specs = [pl.BlockSpec((tm, d), lambda i, ps: (i, 0)),
                pl.BlockSpec((tm, LANES), lambda i, ps: (i, 0)),
                pl.BlockSpec(memory_space=pl.ANY)]
    args = [x, gate, y_rows]
    if final_gain is not None:
        in_specs.append(pl.BlockSpec((1, d), lambda i, ps: (0, 0)))
        args.append(final_gain.reshape(1, d))
        body = functools.partial(_combine_kernel, tm=tm)
    else:
        def body(pos_ref, x_ref, gate_ref, y_hbm, o_ref, *scratch):
            _combine_kernel(pos_ref, x_ref, gate_ref, y_hbm, None, o_ref, *scratch, tm=tm)
    grid_spec = pltpu.PrefetchScalarGridSpec(
        num_scalar_prefetch=1,
        grid=(t // tm,),
        in_specs=in_specs,
        out_specs=pl.BlockSpec((tm, d), lambda i, ps: (i, 0)),
        scratch_shapes=[pltpu.VMEM((2, TOP_K, tm, d), F32), pltpu.SemaphoreType.DMA((2, TOP_K))],
    )
    return pl.pallas_call(
        body,
        out_shape=jax.ShapeDtypeStruct((t, d), F32),
        grid_spec=grid_spec,
        compiler_params=_cparams(("arbitrary",)),
        name="moe_combine",
    )(pos, *args)


def _final_norm_kernel(x_ref, g_ref, o_ref):
    o_ref[...] = _rms(x_ref[...], g_ref[...])


def _final_norm(x, gain, *, tm=512):
    t, d = x.shape
    return pl.pallas_call(
        _final_norm_kernel,
        out_shape=jax.ShapeDtypeStruct((t, d), F32),
        grid=(t // tm,),
        in_specs=[pl.BlockSpec((tm, d), lambda i: (i, 0)), pl.BlockSpec((1, d), lambda i: (0, 0))],
        out_specs=pl.BlockSpec((tm, d), lambda i: (i, 0)),
        compiler_params=_cparams(("parallel",)),
        name="final_norm",
    )(x, gain.reshape(1, d))


def _routing_tables(idx, ch, max_chunks):
    t = idx.shape[0]
    e_flat = idx[:, :TOP_K].T.reshape(-1)
    tok = jnp.tile(jnp.arange(t, dtype=jnp.int32), TOP_K)
    experts = jnp.arange(N_EXPERTS, dtype=jnp.int32)
    onehot = (e_flat[:, None] == experts[None, :]).astype(jnp.int32)
    rank = jnp.sum((jnp.cumsum(onehot, axis=0) - onehot) * onehot, axis=1)
    counts = jnp.sum(onehot, axis=0)
    n_chunk = (counts + ch - 1) // ch
    ends = jnp.cumsum(n_chunk * ch)
    starts = ends - n_chunk * ch
    dest = (starts[e_flat] + rank).astype(jnp.int32)
    p = TOP_K * t + N_EXPERTS * ch
    row_tok = jnp.zeros((p,), jnp.int32).at[dest].set(tok)

    n_vis = (n_chunk + max_chunks - 1) // max_chunks
    per_vis = (n_chunk + jnp.maximum(n_vis, 1) - 1) // jnp.maximum(n_vis, 1)
    v_end = jnp.cumsum(n_vis)
    v_start = v_end - n_vis
    nv_max = (p // ch + (max_chunks - 1) * N_EXPERTS) // max_chunks
    slot = jnp.arange(nv_max, dtype=jnp.int32)
    ok = slot < v_end[-1]
    e_of = jnp.minimum(jnp.sum((slot[:, None] >= v_end[None, :]).astype(jnp.int32), axis=1),
                       N_EXPERTS - 1)
    e_of = jnp.where(ok, e_of, e_of[v_end[-1] - 1])
    k = slot - v_start[e_of]
    n_ch = jnp.where(ok, jnp.clip(n_chunk[e_of] - k * per_vis[e_of], 0, per_vis[e_of]), 0)
    row0 = jnp.where(ok, starts[e_of] + k * per_vis[e_of] * ch, 0)
    tail = (ends[-1:] // ch).astype(jnp.int32)
    tables = (e_of.astype(jnp.int32), ok.astype(jnp.int32), row0.astype(jnp.int32),
              n_ch.astype(jnp.int32), tail, row_tok)
    return tables, dest


def _relayout_mla(w_uq, w_ukv):
    dq = NOPE_DIM + ROPE_DIM_B
    wq = w_uq.reshape(Q_LORA, N_HEADS, dq)
    wq = jnp.concatenate([wq, jnp.zeros((Q_LORA, N_HEADS, MLA_QK - dq), w_uq.dtype)], axis=2)
    wkv = w_ukv.reshape(KV_LORA, N_HEADS, NOPE_DIM + HEAD_DIM)
    wk = wkv[:, :, :NOPE_DIM].reshape(KV_LORA, N_HEADS * NOPE_DIM)
    wv = wkv[:, :, NOPE_DIM:].reshape(KV_LORA, N_HEADS * HEAD_DIM)
    return wq.reshape(Q_LORA, N_HEADS * MLA_QK).astype(BF16), wk.astype(BF16), wv.astype(BF16)


def kernel(x, norm_mix, w_in, dq_norm, dkv_norm, w_uq, w_ukv, lam_q1, lam_k1, lam_q2, lam_k2,
           diff_norm, w_branch, w_out, norm_ffn, w_dense_in, w_dense_out, w_router,
           w_moe_in, w_moe_out, norm_final):
    batch, seq, d = x.shape
    depth = w_in.shape[0]
    t = batch * seq
    xt = x.reshape(t, d)

    rot_a = DIFF_DIM // 4
    tab_a = _rope_tables(seq, ((0, rot_a), (DIFF_DIM, rot_a)))
    tab_b = _rope_tables(seq, ((0, ROPE_DIM_B),))
    tab_c = _rope_tables(seq, ((0, HEAD_DIM // 4),))
    w_in_t = jnp.swapaxes(w_in, 1, 2)

    for layer in range(depth):
        proj = _inproj(_rms_norm_bf16(xt, norm_mix[layer]), w_in_t, layer)

        lam_init = 0.8 - 0.6 * math.exp(-0.3 * layer)
        lam_rows = jnp.stack([lam_q1[layer], lam_k1[layer], lam_q2[layer], lam_k2[layer]])
        y_a = _diff_attention(proj, lam_rows, diff_norm[layer], tab_a,
                              batch=batch, seq=seq, lam_init=lam_init)

        wuq, wuk, wuv = _relayout_mla(w_uq[layer], w_ukv[layer])
        q_b, k_b, v_b = _mla_prep(proj, dq_norm[layer], dkv_norm[layer], wuq, wuk, wuv, tab_b, seq=seq)
        y_b = _mla_attention(q_b, k_b, v_b, batch=batch, seq=seq)

        y_c = _dilated_attention(proj, tab_c, batch=batch, seq=seq)
        y_d = _stick_attention(proj, batch=batch, seq=seq)

        merged = _merge((y_a, y_b, y_c, y_d), proj, w_branch[layer].astype(BF16))
        xt = _outproj(xt, merged, w_out[layer].astype(BF16))

        last = layer == depth - 1
        if layer % 2 == 0:
            m = layer // 2
            xt = _swiglu_rows(xt, norm_ffn[layer], _dense_visits(t, DENSE_CHUNK, DENSE_VISIT_CHUNKS),
                              w_dense_in[m:m + 1], w_dense_out[m:m + 1], dense=True,
                              ch=DENSE_CHUNK, max_chunks=DENSE_VISIT_CHUNKS)
            if last:
                xt = _final_norm(xt, norm_final)
        else:
            m = layer // 2
            idx, gate = _router(xt, norm_ffn[layer], w_router[m])
            tables, dest = _routing_tables(idx, MOE_CHUNK, MOE_VISIT_CHUNKS)
            y_rows = _swiglu_rows(xt, norm_ffn[layer], tables, w_moe_in[m], w_moe_out[m], dense=False,
                                  ch=MOE_CHUNK, max_chunks=MOE_VISIT_CHUNKS)
            xt = _moe_combine(xt, gate, y_rows, dest, norm_final if last else None)
    return xt.reshape(batch, seq, d)
```

```python
import functools
import math

import numpy as np
import jax
import jax.numpy as jnp
from jax import lax
from jax.experimental import pallas as pl
from jax.experimental.pallas import tpu as pltpu

F32 = jnp.float32
BF16 = jnp.bfloat16

HEAD_DIM = 128
ROPE_THETA = 500000.0
NORM_EPS = 1e-6
N_BRANCH = 4
BRANCH_WIDTH = 512
N_HEADS = 4
DIFF_DIM = 64
Q_LORA = 512
KV_LORA = 512
NOPE_DIM = 128
ROPE_DIM_B = 64
MLA_QK = 256
DIL_PAIRS = ((128, 1), (512, 4), (2048, 16))
N_DIL_GROUPS = 3
N_EXPERTS = 8
TOP_K = 2
LANES = 128
LOG2_E = math.log2(math.e)

A_OFF = 0
B_OFF = 1536
C_OFF = 3072
D_OFF = 7680
QKV_COLS = 9216

VMEM_LIMIT = 56 * 1024 * 1024


def _cparams(sem, vmem=VMEM_LIMIT):
    return pltpu.CompilerParams(dimension_semantics=sem, vmem_limit_bytes=vmem)


def _dot(a, b):
    return jnp.dot(a, b, preferred_element_type=F32)


def _dot_nt(a, b):
    return lax.dot_general(a, b, (((1,), (1,)), ((), ())), preferred_element_type=F32)


def _rms(x, gain):
    return x * lax.rsqrt(jnp.mean(x * x, axis=-1, keepdims=True) + NORM_EPS) * gain


def _sigmoid(x):
    return 1.0 / (1.0 + jnp.exp(-x))


def _rope_tables(seq, segments):
    pos = np.arange(seq, dtype=np.float64)
    c = np.ones((seq, LANES), np.float64)
    s = np.zeros((seq, LANES), np.float64)
    r = np.zeros((LANES, LANES), np.float32)
    for start, rot in segments:
        half = rot // 2
        inv_freq = ROPE_THETA ** (-np.arange(0, rot, 2, dtype=np.float64) / rot)
        ang = pos[:, None] * inv_freq[None, :]
        cos, sin = np.cos(ang), np.sin(ang)
        c[:, start:start + half] = cos
        c[:, start + half:start + rot] = cos
        s[:, start:start + half] = -sin
        s[:, start + half:start + rot] = sin
        for i in range(half):
            r[start + half + i, start + i] = 1.0
            r[start + i, start + half + i] = 1.0
    return jnp.asarray(c, F32), jnp.asarray(s, F32), jnp.asarray(r, BF16)


def _rope(x_bf16, rot, c, s):
    return x_bf16.astype(F32) * c + _dot(x_bf16, rot) * s


def _norm_kernel(x_ref, g_ref, o_ref):
    o_ref[...] = _rms(x_ref[...], g_ref[...]).astype(o_ref.dtype)


def _rms_norm_bf16(x, gain, *, tm=512):
    t, d = x.shape
    return pl.pallas_call(
        _norm_kernel,
        out_shape=jax.ShapeDtypeStruct((t, d), BF16),
        grid=(t // tm,),
        in_specs=[pl.BlockSpec((tm, d), lambda i: (i, 0)), pl.BlockSpec((1, d), lambda i: (0, 0))],
        out_specs=pl.BlockSpec((tm, d), lambda i: (i, 0)),
        compiler_params=_cparams(("parallel",)),
        name="mix_norm",
    )(x, gain.reshape(1, d))


B_END = B_OFF + Q_LORA + KV_LORA + ROPE_DIM_B


def _inproj_kernel(h_ref, *refs, gate_block):
    w_refs, o_ref = refs[:-1], refs[-1]
    j = pl.program_id(1)
    h = h_ref[...]
    for q, w_ref in enumerate(w_refs):
        n = w_ref.shape[1]
        acc = _dot_nt(h, w_ref[0].astype(BF16))
        o_ref[:, q * n:(q + 1) * n] = jnp.where(j >= gate_block, _sigmoid(acc), acc).astype(o_ref.dtype)


def _inproj(h, w_t, layer, *, tm=2048, tn=1024):
    t, d = h.shape
    n_out = w_t.shape[1] + C_OFF - B_END
    assert C_OFF % tn == 0 and n_out % tn == 0
    ns = WEIGHT_SPLIT

    unit = math.gcd(tn, C_OFF - B_END)

    def first_col(j, q):
        k = j * (tn // unit) + q * (tn // ns // unit)
        return unit * jnp.where(j * tn < C_OFF, k, k - (C_OFF - B_END) // unit)

    return pl.pallas_call(
        functools.partial(_inproj_kernel, gate_block=QKV_COLS // tn),
        out_shape=jax.ShapeDtypeStruct((t, n_out), BF16),
        grid=(t // tm, n_out // tn),
        in_specs=[pl.BlockSpec((tm, d), lambda i, j: (i, 0))]
                 + [pl.BlockSpec((pl.Element(1), pl.Element(tn // ns), pl.Element(d)),
                                 functools.partial(
                                     lambda i, j, q: (layer, first_col(j, q), 0), q=q))
                    for q in range(ns)],
        out_specs=pl.BlockSpec((tm, tn), lambda i, j: (i, j)),
        compiler_params=_cparams(("parallel", "arbitrary")),
        name="inproj",
    )(h, *([w_t] * ns))


def _osm(scores, values, carries):
    stats = []
    for s, (m, l, _) in zip(scores, carries):
        m_new = jnp.maximum(m, jnp.max(s, axis=-1, keepdims=True))
        alpha = jnp.exp2(m - m_new)
        p = jnp.exp2(s - m_new)
        stats.append((m_new, alpha * l + jnp.sum(p, axis=-1, keepdims=True), alpha, p))
    return tuple((m_new, l, alpha * acc + _dot(p.astype(BF16), v))
                 for (m_new, l, alpha, p), v, (_, _, acc) in zip(stats, values, carries))


def _causal_keep(t):
    r = lax.broadcasted_iota(jnp.int32, (t, t), 0)
    c = lax.broadcasted_iota(jnp.int32, (t, t), 1)
    return c <= r


def _diff_kernel(lam_ref, gain_ref, rot_ref, cq_ref, sq_ref, ck_ref, sk_ref,
                 q_ref, k_ref, v_ref, o_ref, kr_ref, *, tq, hp, lam_init):
    qi = pl.program_id(2)
    rot = rot_ref[...]
    heads = [slice(h * HEAD_DIM, (h + 1) * HEAD_DIM) for h in range(hp)]

    @pl.when(qi == 0)
    def _():
        for hs in heads:
            kr_ref[:, hs] = _rope(k_ref[:, hs], rot, ck_ref[...], sk_ref[...]).astype(BF16)

    scale2 = LOG2_E / math.sqrt(DIFF_DIM)
    lane = lax.broadcasted_iota(jnp.int32, (tq, HEAD_DIM), 1)
    chains = []
    for hs in heads:
        qf = _rope(q_ref[:, hs], rot, cq_ref[...], sq_ref[...])
        chains.append((jnp.where(lane < DIFF_DIM, qf, 0.0).astype(BF16), hs))
        chains.append((jnp.where(lane >= DIFF_DIM, qf, 0.0).astype(BF16), hs))

    def step(j, carry, masked):
        off = pl.multiple_of(j * tq, tq)
        scores = [_dot_nt(q, kr_ref[pl.ds(off, tq), hs]) * scale2 for q, hs in chains]
        if masked:
            keep = _causal_keep(tq)
            scores = [jnp.where(keep, s, -jnp.inf) for s in scores]
        return _osm(scores, [v_ref[pl.ds(off, tq), hs] for _, hs in chains], carry)

    init = (jnp.full((tq, 1), -jnp.inf, F32), jnp.zeros((tq, 1), F32),
            jnp.zeros((tq, HEAD_DIM), F32))
    carry = lax.fori_loop(0, qi, lambda j, c: step(j, c, False), tuple(init for _ in chains))
    carry = step(qi, carry, True)

    lam_rows = lam_ref[...]
    lam = (jnp.exp(jnp.sum(lam_rows[0:1] * lam_rows[1:2], axis=-1, keepdims=True))
           - jnp.exp(jnp.sum(lam_rows[2:3] * lam_rows[3:4], axis=-1, keepdims=True))
           + lam_init)
    for i, hs in enumerate(heads):
        (_, l1, a1), (_, l2, a2) = carry[2 * i], carry[2 * i + 1]
        out = a1 / l1 - lam * (a2 / l2)
        o_ref[:, hs] = (_rms(out, gain_ref[...]) * (1.0 - lam_init)).astype(o_ref.dtype)


def _diff_attention(proj, lam_rows, gain, tables, *, batch, seq, lam_init, tq=512, hp=4):
    c, s, rot = tables
    nq = seq // tq
    w = hp * HEAD_DIM
    cb = A_OFF // w
    ng = N_HEADS // hp
    return pl.pallas_call(
        functools.partial(_diff_kernel, tq=tq, hp=hp, lam_init=lam_init),
        out_shape=jax.ShapeDtypeStruct((batch * seq, BRANCH_WIDTH), BF16),
        grid=(batch, ng, nq),
        in_specs=[pl.BlockSpec((4, DIFF_DIM), lambda b, h, i: (0, 0)),
                  pl.BlockSpec((1, HEAD_DIM), lambda b, h, i: (0, 0)),
                  pl.BlockSpec((LANES, LANES), lambda b, h, i: (0, 0)),
                  pl.BlockSpec((tq, LANES), lambda b, h, i: (i, 0)),
                  pl.BlockSpec((tq, LANES), lambda b, h, i: (i, 0)),
                  pl.BlockSpec((seq, LANES), lambda b, h, i: (0, 0)),
                  pl.BlockSpec((seq, LANES), lambda b, h, i: (0, 0)),
                  pl.BlockSpec((tq, w), lambda b, h, i: (b * nq + i, cb + h)),
                  pl.BlockSpec((seq, w), lambda b, h, i: (b, cb + ng + h)),
                  pl.BlockSpec((seq, w), lambda b, h, i: (b, cb + 2 * ng + h))],
        out_specs=pl.BlockSpec((tq, w), lambda b, h, i: (b * nq + i, h)),
        scratch_shapes=[pltpu.VMEM((seq, w), BF16)],
        compiler_params=_cparams(("parallel", "parallel", "arbitrary")),
        name="diff_attention",
    )(lam_rows, gain.reshape(1, HEAD_DIM), rot, c, s, c, s, proj, proj, proj)


def _mla_prep_kernel(cq_ref, ckv_ref, kr_ref, dqn_ref, dkvn_ref, wuq_ref, wuk_ref, wuv_ref,
                     rot_ref, c_ref, s_ref, q_out, k_out, v_out):
    rot = rot_ref[...]
    c = c_ref[...]
    s = s_ref[...]
    hq = _rms(cq_ref[...].astype(F32), dqn_ref[...]).astype(BF16)
    hkv = _rms(ckv_ref[...].astype(F32), dkvn_ref[...]).astype(BF16)
    q = _dot(hq, wuq_ref[...])
    kn = _dot(hkv, wuk_ref[...])
    v_out[...] = _dot(hkv, wuv_ref[...]).astype(BF16)
    kr_lane = lax.broadcasted_iota(jnp.int32, kr_ref.shape, 1)
    kr = jnp.where(kr_lane < ROPE_DIM_B, kr_ref[...], jnp.zeros_like(kr_ref[...]))
    k_rope = _rope(kr, rot, c, s).astype(BF16)
    for h in range(N_HEADS):
        lo = h * MLA_QK
        q_out[:, lo:lo + NOPE_DIM] = q[:, lo:lo + NOPE_DIM].astype(BF16)
        q_out[:, lo + NOPE_DIM:lo + MLA_QK] = _rope(
            q[:, lo + NOPE_DIM:lo + MLA_QK].astype(BF16), rot, c, s).astype(BF16)
        k_out[:, lo:lo + NOPE_DIM] = kn[:, h * NOPE_DIM:(h + 1) * NOPE_DIM].astype(BF16)
        k_out[:, lo + NOPE_DIM:lo + MLA_QK] = k_rope


def _mla_prep(proj, dq_norm, dkv_norm, wuq, wuk, wuv, tables, *, seq, tm=512):
    c, s, rot = tables
    t = proj.shape[0]
    nb = seq // tm
    cb = B_OFF // Q_LORA
    full = lambda shape: pl.BlockSpec(shape, lambda i: (0, 0))
    return pl.pallas_call(
        _mla_prep_kernel,
        out_shape=(jax.ShapeDtypeStruct((t, N_HEADS * MLA_QK), BF16),
                   jax.ShapeDtypeStruct((t, N_HEADS * MLA_QK), BF16),
                   jax.ShapeDtypeStruct((t, N_HEADS * HEAD_DIM), BF16)),
        grid=(t // tm,),
        in_specs=[pl.BlockSpec((tm, Q_LORA), lambda i: (i, cb)),
                  pl.BlockSpec((tm, KV_LORA), lambda i: (i, cb + 1)),
                  pl.BlockSpec((tm, LANES), lambda i: (i, (B_OFF + Q_LORA + KV_LORA) // LANES)),
                  full((1, Q_LORA)), full((1, KV_LORA)),
                  full(wuq.shape), full(wuk.shape), full(wuv.shape),
                  full((LANES, LANES)),
                  pl.BlockSpec((tm, LANES), lambda i: (i % nb, 0)),
                  pl.BlockSpec((tm, LANES), lambda i: (i % nb, 0))],
        out_specs=(pl.BlockSpec((tm, N_HEADS * MLA_QK), lambda i: (i, 0)),
                   pl.BlockSpec((tm, N_HEADS * MLA_QK), lambda i: (i, 0)),
                   pl.BlockSpec((tm, N_HEADS * HEAD_DIM), lambda i: (i, 0))),
        compiler_params=_cparams(("parallel",)),
        name="mla_prep",
    )(proj, proj, proj, dq_norm.reshape(1, Q_LORA), dkv_norm.reshape(1, KV_LORA),
      wuq, wuk, wuv, rot, c, s)


def _flash_kernel(q_ref, k_ref, v_ref, o_ref, *, tq, hp, dqk, scale):
    qi = pl.program_id(2)
    qk = [slice(h * dqk, (h + 1) * dqk) for h in range(hp)]
    hv = [slice(h * HEAD_DIM, (h + 1) * HEAD_DIM) for h in range(hp)]
    qs = [q_ref[:, sl] for sl in qk]

    def step(j, carry, masked):
        off = pl.multiple_of(j * tq, tq)
        scores = [_dot_nt(q, k_ref[pl.ds(off, tq), ks]) * (scale * LOG2_E) for q, ks in zip(qs, qk)]
        if masked:
            keep = _causal_keep(tq)
            scores = [jnp.where(keep, s, -jnp.inf) for s in scores]
        return _osm(scores, [v_ref[pl.ds(off, tq), vs] for vs in hv], carry)

    init = (jnp.full((tq, 1), -jnp.inf, F32), jnp.zeros((tq, 1), F32),
            jnp.zeros((tq, HEAD_DIM), F32))
    carry = lax.fori_loop(0, qi, lambda j, c: step(j, c, False), tuple(init for _ in qs))
    carry = step(qi, carry, True)
    for vs, (_, l, acc) in zip(hv, carry):
        o_ref[:, vs] = (acc / l).astype(o_ref.dtype)


def _mla_attention(q, k, v, *, batch, seq, tq=512, hp=4):
    nq = seq // tq
    ng = N_HEADS // hp
    return pl.pallas_call(
        functools.partial(_flash_kernel, tq=tq, hp=hp, dqk=MLA_QK,
                          scale=1.0 / math.sqrt(NOPE_DIM + ROPE_DIM_B)),
        out_shape=jax.ShapeDtypeStruct((batch * seq, BRANCH_WIDTH), BF16),
        grid=(batch, ng, nq),
        in_specs=[pl.BlockSpec((tq, hp * MLA_QK), lambda b, h, i: (b * nq + i, h)),
                  pl.BlockSpec((seq, hp * MLA_QK), lambda b, h, i: (b, h)),
                  pl.BlockSpec((seq, hp * HEAD_DIM), lambda b, h, i: (b, h))],
        out_specs=pl.BlockSpec((tq, hp * HEAD_DIM), lambda b, h, i: (b * nq + i, h)),
        compiler_params=_cparams(("parallel", "parallel", "arbitrary")),
        name="mla_attention",
    )(q, k, v)


DIL_BLOCK = 128


def _dilated_kernel(rot_ref, c_ref, s_ref, *refs, seq):
    in_refs = refs[:9]
    o_ref = refs[9]
    qf, kf, vf, og, lse = refs[10:]
    rot = rot_ref[...]
    c = c_ref[...]
    s = s_ref[...]
    for g in range(N_DIL_GROUPS):
        qf[g] = _rope(in_refs[g][...], rot, c, s)
        kf[g] = _rope(in_refs[3 + g][...], rot, c, s)
        vf[g] = in_refs[6 + g][...].astype(F32)
    scale = 1.0 / math.sqrt(HEAD_DIM)
    blk = DIL_BLOCK

    def rows(start, size, stride):
        return pl.ds(start, size) if stride == 1 else pl.ds(start, size, stride=stride)

    for g, (window, dil) in enumerate(DIL_PAIRS):
        assert window == blk * dil
        span = blk * dil
        n_sub = seq // span
        nk = 2 * blk if n_sub > 1 else blk
        assert n_sub & (n_sub - 1) == 0
        q_rows, k_rows = [], []
        for r in range(dil):
            for cb in range(n_sub):
                q_rows.append(rows(r + cb * span, blk, dil))
                k_rows.append(rows(r + max(cb - 1, 0) * span, nk, dil))
        nb = len(q_rows)
        q = jnp.stack([qf[g, qr, :] for qr in q_rows]).astype(BF16)
        k = jnp.stack([kf[g, kr, :] for kr in k_rows]).astype(BF16)
        v = jnp.stack([vf[g, kr, :] for kr in k_rows]).astype(BF16)
        sc = jnp.einsum("bqd,bkd->bqk", q, k, preferred_element_type=F32) * (scale * LOG2_E)
        first = (lax.broadcasted_iota(jnp.int32, (nb, blk, nk), 0) & (n_sub - 1)) == 0
        dist = (lax.broadcasted_iota(jnp.int32, (nb, blk, nk), 1)
                - lax.broadcasted_iota(jnp.int32, (nb, blk, nk), 2)
                + jnp.where(first, 0, nk - blk))
        sc = jnp.where(dist >= 0, jnp.where(dist <= blk, sc, -jnp.inf), -jnp.inf)
        m = jnp.max(sc, axis=-1, keepdims=True)
        e = jnp.exp2(sc - m)
        den = jnp.sum(e, axis=-1, keepdims=True)
        o = jnp.einsum("bqk,bkd->bqd", e.astype(BF16), v, preferred_element_type=F32) / den
        lg = jnp.broadcast_to(m + jnp.log2(den), (nb, blk, HEAD_DIM))
        for i, qr in enumerate(q_rows):
            og[g, qr, :] = o[i]
            lse[g, qr, :] = lg[i]

    l0, l1, l2 = lse[0], lse[1], lse[2]
    mx = jnp.maximum(jnp.maximum(l0, l1), l2)
    w0, w1, w2 = jnp.exp2(l0 - mx), jnp.exp2(l1 - mx), jnp.exp2(l2 - mx)
    o_ref[...] = ((w0 * og[0] + w1 * og[1] + w2 * og[2]) / (w0 + w1 + w2)).astype(o_ref.dtype)


def _dilated_attention(proj, tables, *, batch, seq):
    c, s, rot = tables
    cb = C_OFF // HEAD_DIM
    nh = N_DIL_GROUPS * N_HEADS

    def col(kind, g):
        return lambda b, h: (b, cb + kind * nh + g * N_HEADS + h)

    in_specs = [pl.BlockSpec((LANES, LANES), lambda b, h: (0, 0)),
                pl.BlockSpec((seq, LANES), lambda b, h: (0, 0)),
                pl.BlockSpec((seq, LANES), lambda b, h: (0, 0))]
    for kind in range(3):
        for g in range(N_DIL_GROUPS):
            in_specs.append(pl.BlockSpec((seq, HEAD_DIM), col(kind, g)))
    return pl.pallas_call(
        functools.partial(_dilated_kernel, seq=seq),
        out_shape=jax.ShapeDtypeStruct((batch * seq, BRANCH_WIDTH), BF16),
        grid=(batch, N_HEADS),
        in_specs=in_specs,
        out_specs=pl.BlockSpec((seq, HEAD_DIM), lambda b, h: (b, h)),
        scratch_shapes=[pltpu.VMEM((N_DIL_GROUPS, seq, HEAD_DIM), F32) for _ in range(5)],
        compiler_params=_cparams(("parallel", "parallel")),
        name="dilated_attention",
    )(rot, c, s, *([proj] * 9))


def _stick_kernel(q_ref, k_ref, v_ref, o_ref, *, tq, hp, scale):
    qi = pl.program_id(2)
    heads = [slice(h * HEAD_DIM, (h + 1) * HEAD_DIM) for h in range(hp)]
    qs = [q_ref[:, hs] for hs in heads]
    r = lax.broadcasted_iota(jnp.int32, (tq, tq), 0)
    c = lax.broadcasted_iota(jnp.int32, (tq, tq), 1)
    later_keys = jnp.where(r > c, 1.0, 0.0).astype(BF16)

    def step(j, carry, diag):
        off = pl.multiple_of(j * tq, tq)
        strict = c < r
        z2 = [_dot_nt(q, k_ref[pl.ds(off, tq), hs]) * (scale * LOG2_E) for q, hs in zip(qs, heads)]
        sp2 = [jnp.maximum(z, 0.0) + jnp.log2(1.0 + jnp.exp2(-jnp.abs(z))) for z in z2]
        log_not = [jnp.where(strict, -sp, 0.0) if diag else -sp for sp in sp2]
        later = []
        for ln, (tail, _) in zip(log_not, carry):
            hi = ln.astype(BF16)
            lo = (ln - hi.astype(F32)).astype(BF16)
            later.append(_dot(hi, later_keys) + _dot(lo, later_keys) + tail)
        a = [jnp.exp2((z - sp) + lt) for z, sp, lt in zip(z2, sp2, later)]
        if diag:
            a = [jnp.where(strict, x, 0.0) for x in a]
        out = []
        for x, ln, hs, (tail, acc) in zip(a, log_not, heads, carry):
            acc = acc + _dot(x.astype(BF16), v_ref[pl.ds(off, tq), hs])
            out.append((tail + jnp.sum(ln, axis=-1, keepdims=True), acc))
        return tuple(out)

    init = (jnp.zeros((tq, 1), F32), jnp.zeros((tq, HEAD_DIM), F32))
    carry = step(qi, tuple(init for _ in heads), True)
    carry = lax.fori_loop(0, qi, lambda t, cr: step(qi - 1 - t, cr, False), carry)
    for hs, (_, acc) in zip(heads, carry):
        o_ref[:, hs] = acc.astype(o_ref.dtype)


def _stick_attention(proj, *, batch, seq, tq=256, hp=4):
    nq = seq // tq
    w = hp * HEAD_DIM
    cb = D_OFF // w
    ng = N_HEADS // hp
    return pl.pallas_call(
        functools.partial(_stick_kernel, tq=tq, hp=hp, scale=1.0 / math.sqrt(HEAD_DIM)),
        out_shape=jax.ShapeDtypeStruct((batch * seq, BRANCH_WIDTH), BF16),
        grid=(batch, ng, nq),
        in_specs=[pl.BlockSpec((tq, w), lambda b, h, i: (b * nq + i, cb + h)),
                  pl.BlockSpec((seq, w), lambda b, h, i: (b, cb + ng + h)),
                  pl.BlockSpec((seq, w), lambda b, h, i: (b, cb + 2 * ng + h))],
        out_specs=pl.BlockSpec((tq, w), lambda b, h, i: (b * nq + i, h)),
        compiler_params=_cparams(("parallel", "parallel", "arbitrary")),
        name="stick_attention",
    )(proj, proj, proj)


def _merge_kernel(ya, yb, yc, yd, g0, g1, g2, g3, wb_ref, o_ref):
    acc = None
    for i, (y, g) in enumerate(((ya, g0), (yb, g1), (yc, g2), (yd, g3))):
        t = g[...].astype(F32) * _dot(y[...], wb_ref[i])
        acc = t if acc is None else acc + t
    o_ref[...] = acc.astype(o_ref.dtype)


def _merge(ys, proj, wb, *, tm=1024, tn=1024):
    t = proj.shape[0]
    d = wb.shape[2]
    nn = d // tn
    g0 = QKV_COLS // tn
    y_spec = pl.BlockSpec((tm, BRANCH_WIDTH), lambda i, j: (i, 0))
    g_specs = [pl.BlockSpec((tm, tn), functools.partial(lambda i, j, b: (i, g0 + b * nn + j), b=b))
               for b in range(N_BRANCH)]
    return pl.pallas_call(
        _merge_kernel,
        out_shape=jax.ShapeDtypeStruct((t, d), BF16),
        grid=(t // tm, nn),
        in_specs=[y_spec] * N_BRANCH + g_specs
                 + [pl.BlockSpec((N_BRANCH, BRANCH_WIDTH, tn), lambda i, j: (0, 0, j))],
        out_specs=pl.BlockSpec((tm, tn), lambda i, j: (i, j)),
        compiler_params=_cparams(("parallel", "arbitrary")),
        name="branch_merge",
    )(*ys, proj, proj, proj, proj, wb)


def _outproj_kernel(x_ref, m_ref, w_ref, o_ref):
    o_ref[...] = x_ref[...] + _dot(m_ref[...], w_ref[...])


def _outproj(x, merged, w, *, tm=1024, tn=1024):
    t, d = x.shape
    return pl.pallas_call(
        _outproj_kernel,
        out_shape=jax.ShapeDtypeStruct((t, d), F32),
        grid=(t // tm, d // tn),
        in_specs=[pl.BlockSpec((tm, tn), lambda i, j: (i, j)),
                  pl.BlockSpec((tm, d), lambda i, j: (i, 0)),
                  pl.BlockSpec((d, tn), lambda i, j: (0, j))],
        out_specs=pl.BlockSpec((tm, tn), lambda i, j: (i, j)),
        compiler_params=_cparams(("parallel", "arbitrary")),
        name="out_proj",
    )(x, merged, w)


def _router_kernel(x_ref, gain_ref, wr_ref, idx_ref, gate_ref):
    h = _rms(x_ref[...], gain_ref[...])
    logits = jnp.dot(h, wr_ref[...], preferred_element_type=F32, precision=lax.Precision.HIGHEST)
    lane = lax.broadcasted_iota(jnp.int32, logits.shape, 1)
    lanef = lane.astype(F32)
    lg = jnp.where(lane < N_EXPERTS, logits, -jnp.inf)
    v1 = jnp.max(lg, axis=-1, keepdims=True)
    i1 = jnp.min(jnp.where(lg == v1, lanef, float(LANES)), axis=-1, keepdims=True)
    lg2 = jnp.where(lanef == i1, -jnp.inf, lg)
    v2 = jnp.max(lg2, axis=-1, keepdims=True)
    i2 = jnp.min(jnp.where(lg2 == v2, lanef, float(LANES)), axis=-1, keepdims=True)
    e2 = jnp.exp(v2 - v1)
    g1 = 1.0 / (1.0 + e2)
    g2 = e2 / (1.0 + e2)
    idx_ref[...] = jnp.where(lane == 0, i1, jnp.where(lane == 1, i2, 0.0)).astype(jnp.int32)
    gate_ref[...] = jnp.where(lane == 0, g1, jnp.where(lane == 1, g2, 0.0))


def _router(x, gain, w_router, *, tm=512):
    t, d = x.shape
    wr = jnp.zeros((d, LANES), F32).at[:, :N_EXPERTS].set(w_router)
    return pl.pallas_call(
        _router_kernel,
        out_shape=(jax.ShapeDtypeStruct((t, LANES), jnp.int32),
                   jax.ShapeDtypeStruct((t, LANES), F32)),
        grid=(t // tm,),
        in_specs=[pl.BlockSpec((tm, d), lambda i: (i, 0)),
                  pl.BlockSpec((1, d), lambda i: (0, 0)),
                  pl.BlockSpec((d, LANES), lambda i: (0, 0))],
        out_specs=(pl.BlockSpec((tm, LANES), lambda i: (i, 0)),
                   pl.BlockSpec((tm, LANES), lambda i: (i, 0))),
        compiler_params=_cparams(("parallel",)),
        name="router",
    )(x, gain.reshape(1, d), wr)


def _start_row_gather(idx_ref, base, n, src_hbm, dst, sem):
    def start(r, _):
        row = idx_ref[base + r]
        pltpu.make_async_copy(src_hbm.at[pl.ds(row, 1), :], dst.at[pl.ds(r, 1), :], sem).start()
        return 0

    lax.fori_loop(0, n, start, 0, unroll=8)


def _wait_row_gather(n, src_hbm, dst, sem):
    pltpu.make_async_copy(src_hbm.at[pl.ds(0, n), :], dst.at[pl.ds(0, n), :], sem).wait()


MOE_CHUNK = 512
MOE_VISIT_CHUNKS = 5
DENSE_CHUNK = 512
DENSE_VISIT_CHUNKS = 5
WEIGHT_SPLIT = 1


def _moe_kernel(vis_e_ref, vis_ok_ref, vis_row_ref, vis_nch_ref, tail_ref, row_tok_ref,
                x_hbm, gain_ref, *refs, ch, dense):
    del vis_e_ref, vis_ok_ref
    ns = WEIGHT_SPLIT
    wg_refs, wu_refs, wo_refs = refs[:ns], refs[ns:2 * ns], refs[2 * ns:3 * ns]
    y_hbm, stage, h_ref, acc_ref, gsem, osem = refs[3 * ns:]
    v = pl.program_id(0)
    f = pl.program_id(1)
    nv = pl.num_programs(0)
    nf = pl.num_programs(1)
    nch = vis_nch_ref[v]
    row0 = vis_row_ref[v]
    active = nch > 0
    kq = h_ref.shape[1] // ns

    def rows(c):
        return pl.ds(pl.multiple_of(c * ch, ch), ch)

    def step(h):
        parts = [h[:, q * kq:(q + 1) * kq] for q in range(ns)]
        g = sum(_dot(hq, w[...].astype(BF16)) for hq, w in zip(parts, wg_refs))
        u = sum(_dot(hq, w[...].astype(BF16)) for hq, w in zip(parts, wu_refs))
        a = (g * _sigmoid(g) * u).astype(BF16)
        return jnp.concatenate([_dot(a, w[...].astype(BF16)) for w in wo_refs], axis=1)

    def out_copy(c):
        dst = y_hbm.at[pl.ds(pl.multiple_of(row0 + c * ch, ch), ch), :]
        return pltpu.make_async_copy(acc_ref.at[rows(c), :], dst, osem)

    @pl.when(jnp.logical_and(active, f == 0))
    def _():
        def fetch(c, slot):
            if dense:
                src = x_hbm.at[pl.ds(pl.multiple_of(row0 + c * ch, ch), ch), :]
                pltpu.make_async_copy(src, stage.at[slot], gsem.at[slot]).start()
            else:
                _start_row_gather(row_tok_ref, row0 + c * ch, ch, x_hbm, stage.at[slot], gsem.at[slot])

        fetch(0, 0)

        def body(c, _):
            slot = lax.rem(c, 2)

            @pl.when(c + 1 < nch)
            def _():
                fetch(c + 1, 1 - slot)

            _wait_row_gather(ch, x_hbm, stage.at[slot], gsem.at[slot])
            x = stage[slot]
            h = _rms(x, gain_ref[...]).astype(BF16)
            h_ref[rows(c), :] = h
            acc_ref[rows(c), :] = x + step(h) if dense else step(h)
            return 0

        lax.fori_loop(0, nch, body, 0)

    @pl.when(jnp.logical_and(active, jnp.logical_and(f > 0, f < nf - 1)))
    def _():
        def body(c, _):
            acc_ref[rows(c), :] += step(h_ref[rows(c), :])
            return 0

        lax.fori_loop(0, nch, body, 0)

    @pl.when(jnp.logical_and(active, f == nf - 1))
    def _():
        def body(c, _):
            acc_ref[rows(c), :] += step(h_ref[rows(c), :])
            out_copy(c).start()
            return 0

        def drain(c, _):
            out_copy(c).wait()
            return 0

        lax.fori_loop(0, nch, body, 0)
        lax.fori_loop(0, nch, drain, 0)

    @pl.when(jnp.logical_and(v == nv - 1, f == nf - 1))
    def _():
        first = tail_ref[0]
        n_tail = y_hbm.shape[0] // ch - first
        stage[0] = jnp.zeros(stage.shape[1:], stage.dtype)

        def tail_copy(c):
            dst = y_hbm.at[pl.ds(pl.multiple_of((first + c) * ch, ch), ch), :]
            return pltpu.make_async_copy(stage.at[0], dst, osem)

        def fill(c, _):
            tail_copy(c).start()
            return 0

        def drain(c, _):
            tail_copy(c).wait()
            return 0

        lax.fori_loop(0, n_tail, fill, 0)
        lax.fori_loop(0, n_tail, drain, 0)


def _dense_visits(t, ch, max_chunks):
    span = ch * max_chunks
    row0 = np.arange(0, t, span, dtype=np.int32)
    n_ch = np.minimum(max_chunks, (t - row0) // ch).astype(np.int32)
    zeros = np.zeros_like(row0)
    return tuple(jnp.asarray(a) for a in (zeros, zeros + 1, row0, n_ch,
                                          np.array([t // ch], np.int32), np.zeros((1,), np.int32)))


def _swiglu_rows(x, gain, tables, w_in, w_out, *, dense, ch, max_chunks, tf=256, n_visits=None):
    vis_e, vis_ok, vis_row, vis_nch, tail, row_tok = tables
    t, d = x.shape
    ff = w_out.shape[1]
    nf = ff // tf
    p = t if dense else row_tok.shape[0]
    ns = WEIGHT_SPLIT

    def wspec(shape, index):
        def index_map(v, f, e, ok, *_):
            return index(e[v], jnp.where(ok[v] > 0, f, nf - 1))
        return pl.BlockSpec(shape, index_map)

    wg_specs = [wspec((None, d // ns, tf), functools.partial(lambda e, f, q: (e, q, f), q=q))
                for q in range(ns)]
    wu_specs = [wspec((None, d // ns, tf), functools.partial(lambda e, f, q: (e, q, nf + f), q=q))
                for q in range(ns)]
    wo_specs = [wspec((None, tf, d // ns), functools.partial(lambda e, f, q: (e, f, q), q=q))
                for q in range(ns)]
    grid_spec = pltpu.PrefetchScalarGridSpec(
        num_scalar_prefetch=6,
        grid=(vis_e.shape[0] if n_visits is None else n_visits, nf),
        in_specs=[pl.BlockSpec(memory_space=pl.ANY),
                  pl.BlockSpec((1, d), lambda v, f, *_: (0, 0))] + wg_specs + wu_specs + wo_specs,
        out_specs=pl.BlockSpec(memory_space=pl.ANY),
        scratch_shapes=[pltpu.VMEM((2, ch, d), F32),
                        pltpu.VMEM((max_chunks * ch, d), BF16),
                        pltpu.VMEM((max_chunks * ch, d), F32),
                        pltpu.SemaphoreType.DMA((2,)), pltpu.SemaphoreType.DMA],
    )
    return pl.pallas_call(
        functools.partial(_moe_kernel, ch=ch, dense=dense),
        out_shape=jax.ShapeDtypeStruct((p, d), F32),
        grid_spec=grid_spec,
        compiler_params=_cparams(("arbitrary", "arbitrary"), vmem=60 * 1024 * 1024),
        name="dense_ffn" if dense else "moe_experts",
    )(vis_e, vis_ok, vis_row, vis_nch, tail, row_tok, x, gain.reshape(1, d),
      *([w_in] * (2 * ns)), *([w_out] * ns))


def _combine_kernel(pos_ref, x_ref, gate_ref, y_hbm, gain_ref, o_ref, buf, sem, *, tm):
    i = pl.program_id(0)
    n = pl.num_programs(0)
    t = pos_ref.shape[0] // TOP_K
    slot = lax.rem(i, 2)

    def start(step, sl):
        for k in range(TOP_K):
            _start_row_gather(pos_ref, k * t + step * tm, tm, y_hbm, buf.at[sl, k], sem.at[sl, k])

    @pl.when(i == 0)
    def _():
        start(0, 0)

    @pl.when(i + 1 < n)
    def _():
        start(i + 1, 1 - slot)

    for k in range(TOP_K):
        _wait_row_gather(tm, y_hbm, buf.at[slot, k], sem.at[slot, k])
    g = gate_ref[...]
    x = x_ref[...] + (g[:, 0:1] * buf[slot, 0] + g[:, 1:2] * buf[slot, 1])
    if gain_ref is None:
        o_ref[...] = x
    else:
        o_ref[...] = _rms(x, gain_ref[...])


def _moe_combine(x, gate, y_rows, pos, final_gain, *, tm=256):
    t, d = x.shape
    in_specs = [pl.BlockSpec((tm, d), lambda i, ps: (i, 0)),
                pl.BlockSpec((tm, LANES), lambda i, ps: (i, 0)),
                pl.BlockSpec(memory_space=pl.ANY)]
    args = [x, gate, y_rows]
    if final_gain is not None:
        in_specs.append(pl.BlockSpec((1, d), lambda i, ps: (0, 0)))
        args.append(final_gain.reshape(1, d))
        body = functools.partial(_combine_kernel, tm=tm)
    else:
        def body(pos_ref, x_ref, gate_ref, y_hbm, o_ref, *scratch):
            _combine_kernel(pos_ref, x_ref, gate_ref, y_hbm, None, o_ref, *scratch, tm=tm)
    grid_spec = pltpu.PrefetchScalarGridSpec(
        num_scalar_prefetch=1,
        grid=(t // tm,),
        in_specs=in_specs,
        out_specs=pl.BlockSpec((tm, d), lambda i, ps: (i, 0)),
        scratch_shapes=[pltpu.VMEM((2, TOP_K, tm, d), F32), pltpu.SemaphoreType.DMA((2, TOP_K))],
    )
    return pl.pallas_call(
        body,
        out_shape=jax.ShapeDtypeStruct((t, d), F32),
        grid_spec=grid_spec,
        compiler_params=_cparams(("arbitrary",)),
        name="moe_combine",
    )(pos, *args)


def _final_norm_kernel(x_ref, g_ref, o_ref):
    o_ref[...] = _rms(x_ref[...], g_ref[...])


def _final_norm(x, gain, *, tm=512):
    t, d = x.shape
    return pl.pallas_call(
        _final_norm_kernel,
        out_shape=jax.ShapeDtypeStruct((t, d), F32),
        grid=(t // tm,),
        in_specs=[pl.BlockSpec((tm, d), lambda i: (i, 0)), pl.BlockSpec((1, d), lambda i: (0, 0))],
        out_specs=pl.BlockSpec((tm, d), lambda i: (i, 0)),
        compiler_params=_cparams(("parallel",)),
        name="final_norm",
    )(x, gain.reshape(1, d))


def _routing_tables(idx, ch, max_chunks):
    t = idx.shape[0]
    e_flat = idx[:, :TOP_K].T.reshape(-1)
    tok = jnp.tile(jnp.arange(t, dtype=jnp.int32), TOP_K)
    experts = jnp.arange(N_EXPERTS, dtype=jnp.int32)
    onehot = (e_flat[:, None] == experts[None, :]).astype(jnp.int32)
    rank = jnp.sum((jnp.cumsum(onehot, axis=0) - onehot) * onehot, axis=1)
    counts = jnp.sum(onehot, axis=0)
    n_chunk = (counts + ch - 1) // ch
    ends = jnp.cumsum(n_chunk * ch)
    starts = ends - n_chunk * ch
    dest = (starts[e_flat] + rank).astype(jnp.int32)
    p = TOP_K * t + N_EXPERTS * ch
    row_tok = jnp.zeros((p,), jnp.int32).at[dest].set(tok)

    n_vis = (n_chunk + max_chunks - 1) // max_chunks
    per_vis = (n_chunk + jnp.maximum(n_vis, 1) - 1) // jnp.maximum(n_vis, 1)
    v_end = jnp.cumsum(n_vis)
    v_start = v_end - n_vis
    nv_max = (p // ch + (max_chunks - 1) * N_EXPERTS) // max_chunks
    slot = jnp.arange(nv_max, dtype=jnp.int32)
    ok = slot < v_end[-1]
    e_of = jnp.minimum(jnp.sum((slot[:, None] >= v_end[None, :]).astype(jnp.int32), axis=1),
                       N_EXPERTS - 1)
    e_of = jnp.where(ok, e_of, e_of[v_end[-1] - 1])
    k = slot - v_start[e_of]
    n_ch = jnp.where(ok, jnp.clip(n_chunk[e_of] - k * per_vis[e_of], 0, per_vis[e_of]), 0)
    row0 = jnp.where(ok, starts[e_of] + k * per_vis[e_of] * ch, 0)
    tail = (ends[-1:] // ch).astype(jnp.int32)
    tables = (e_of.astype(jnp.int32), ok.astype(jnp.int32), row0.astype(jnp.int32),
              n_ch.astype(jnp.int32), tail, row_tok)
    return tables, dest, v_end[-1].astype(jnp.int32)


def _relayout_mla(w_uq, w_ukv):
    dq = NOPE_DIM + ROPE_DIM_B
    wq = w_uq.reshape(Q_LORA, N_HEADS, dq)
    wq = jnp.concatenate([wq, jnp.zeros((Q_LORA, N_HEADS, MLA_QK - dq), w_uq.dtype)], axis=2)
    wkv = w_ukv.reshape(KV_LORA, N_HEADS, NOPE_DIM + HEAD_DIM)
    wk = wkv[:, :, :NOPE_DIM].reshape(KV_LORA, N_HEADS * NOPE_DIM)
    wv = wkv[:, :, NOPE_DIM:].reshape(KV_LORA, N_HEADS * HEAD_DIM)
    return wq.reshape(Q_LORA, N_HEADS * MLA_QK).astype(BF16), wk.astype(BF16), wv.astype(BF16)


def kernel(x, norm_mix, w_in, dq_norm, dkv_norm, w_uq, w_ukv, lam_q1, lam_k1, lam_q2, lam_k2,
           diff_norm, w_branch, w_out, norm_ffn, w_dense_in, w_dense_out, w_router,
           w_moe_in, w_moe_out, norm_final):
    batch, seq, d = x.shape
    depth = w_in.shape[0]
    t = batch * seq
    xt = x.reshape(t, d)

    rot_a = DIFF_DIM // 4
    tab_a = _rope_tables(seq, ((0, rot_a), (DIFF_DIM, rot_a)))
    tab_b = _rope_tables(seq, ((0, ROPE_DIM_B),))
    tab_c = _rope_tables(seq, ((0, HEAD_DIM // 4),))
    w_in_t = jnp.swapaxes(w_in, 1, 2)

    for layer in range(depth):
        proj = _inproj(_rms_norm_bf16(xt, norm_mix[layer]), w_in_t, layer)

        lam_init = 0.8 - 0.6 * math.exp(-0.3 * layer)
        lam_rows = jnp.stack([lam_q1[layer], lam_k1[layer], lam_q2[layer], lam_k2[layer]])
        y_a = _diff_attention(proj, lam_rows, diff_norm[layer], tab_a,
                              batch=batch, seq=seq, lam_init=lam_init)

        wuq, wuk, wuv = _relayout_mla(w_uq[layer], w_ukv[layer])
        q_b, k_b, v_b = _mla_prep(proj, dq_norm[layer], dkv_norm[layer], wuq, wuk, wuv, tab_b, seq=seq)
        y_b = _mla_attention(q_b, k_b, v_b, batch=batch, seq=seq)

        y_c = _dilated_attention(proj, tab_c, batch=batch, seq=seq)
        y_d = _stick_attention(proj, batch=batch, seq=seq)

        merged = _merge((y_a, y_b, y_c, y_d), proj, w_branch[layer].astype(BF16))
        xt = _outproj(xt, merged, w_out[layer].astype(BF16))

        last = layer == depth - 1
        if layer % 2 == 0:
            m = layer // 2
            xt = _swiglu_rows(xt, norm_ffn[layer], _dense_visits(t, DENSE_CHUNK, DENSE_VISIT_CHUNKS),
                              w_dense_in[m:m + 1], w_dense_out[m:m + 1], dense=True,
                              ch=DENSE_CHUNK, max_chunks=DENSE_VISIT_CHUNKS)
            if last:
                xt = _final_norm(xt, norm_final)
        else:
            m = layer // 2
            idx, gate = _router(xt, norm_ffn[layer], w_router[m])
            tables, dest, n_vis = _routing_tables(idx, MOE_CHUNK, MOE_VISIT_CHUNKS)
            y_rows = _swiglu_rows(xt, norm_ffn[layer], tables, w_moe_in[m], w_moe_out[m], dense=False,
                                  ch=MOE_CHUNK, max_chunks=MOE_VISIT_CHUNKS, n_visits=n_vis)
            xt = _moe_combine(xt, gate, y_rows, dest, norm_final if last else None)
    return xt.reshape(batch, seq, d)
```

```python
import functools
import math

import numpy as np
import jax
import jax.numpy as jnp
from jax import lax
from jax.experimental import pallas as pl
from jax.experimental.pallas import tpu as pltpu

F32 = jnp.float32
BF16 = jnp.bfloat16

HEAD_DIM = 128
ROPE_THETA = 500000.0
NORM_EPS = 1e-6
N_BRANCH = 4
BRANCH_WIDTH = 512
N_HEADS = 4
DIFF_DIM = 64
Q_LORA = 512
KV_LORA = 512
NOPE_DIM = 128
ROPE_DIM_B = 64
MLA_QK = 256
DIL_PAIRS = ((128, 1), (512, 4), (2048, 16))
N_DIL_GROUPS = 3
N_EXPERTS = 8
TOP_K = 2
LANES = 128
LOG2_E = math.log2(math.e)

A_OFF = 0
B_OFF = 1536
C_OFF = 3072
D_OFF = 7680
QKV_COLS = 9216

VMEM_LIMIT = 56 * 1024 * 1024


def _cparams(sem, vmem=VMEM_LIMIT):
    return pltpu.CompilerParams(dimension_semantics=sem, vmem_limit_bytes=vmem)


def _dot(a, b):
    return jnp.dot(a, b, preferred_element_type=F32)


def _dot_nt(a, b):
    return lax.dot_general(a, b, (((1,), (1,)), ((), ())), preferred_element_type=F32)


def _rms(x, gain):
    return x * lax.rsqrt(jnp.mean(x * x, axis=-1, keepdims=True) + NORM_EPS) * gain


def _sigmoid(x):
    return 1.0 / (1.0 + jnp.exp(-x))


def _rope_tables(seq, segments):
    pos = np.arange(seq, dtype=np.float64)
    c = np.ones((seq, LANES), np.float64)
    s = np.zeros((seq, LANES), np.float64)
    r = np.zeros((LANES, LANES), np.float32)
    for start, rot in segments:
        half = rot // 2
        inv_freq = ROPE_THETA ** (-np.arange(0, rot, 2, dtype=np.float64) / rot)
        ang = pos[:, None] * inv_freq[None, :]
        cos, sin = np.cos(ang), np.sin(ang)
        c[:, start:start + half] = cos
        c[:, start + half:start + rot] = cos
        s[:, start:start + half] = -sin
        s[:, start + half:start + rot] = sin
        for i in range(half):
            r[start + half + i, start + i] = 1.0
            r[start + i, start + half + i] = 1.0
    return jnp.asarray(c, F32), jnp.asarray(s, F32), jnp.asarray(r, BF16)


def _rope(x_bf16, rot, c, s):
    return x_bf16.astype(F32) * c + _dot(x_bf16, rot) * s


def _norm_kernel(x_ref, g_ref, o_ref):
    o_ref[...] = _rms(x_ref[...], g_ref[...]).astype(o_ref.dtype)


def _rms_norm_bf16(x, gain, *, tm=512):
    t, d = x.shape
    return pl.pallas_call(
        _norm_kernel,
        out_shape=jax.ShapeDtypeStruct((t, d), BF16),
        grid=(t // tm,),
        in_specs=[pl.BlockSpec((tm, d), lambda i: (i, 0)), pl.BlockSpec((1, d), lambda i: (0, 0))],
        out_specs=pl.BlockSpec((tm, d), lambda i: (i, 0)),
        compiler_params=_cparams(("parallel",)),
        name="mix_norm",
    )(x, gain.reshape(1, d))


B_END = B_OFF + Q_LORA + KV_LORA + ROPE_DIM_B


def _inproj_kernel(h_ref, *refs, gate_block):
    w_refs, o_ref = refs[:-1], refs[-1]
    j = pl.program_id(1)
    h = h_ref[...]
    for q, w_ref in enumerate(w_refs):
        n = w_ref.shape[1]
        acc = _dot_nt(h, w_ref[0].astype(BF16))
        o_ref[:, q * n:(q + 1) * n] = jnp.where(j >= gate_block, _sigmoid(acc), acc).astype(o_ref.dtype)


def _inproj(h, w_t, layer, *, tm=2048, tn=1024):
    t, d = h.shape
    n_out = w_t.shape[1] + C_OFF - B_END
    assert C_OFF % tn == 0 and n_out % tn == 0
    ns = WEIGHT_SPLIT

    unit = math.gcd(tn, C_OFF - B_END)

    def first_col(j, q):
        k = j * (tn // unit) + q * (tn // ns // unit)
        return unit * jnp.where(j * tn < C_OFF, k, k - (C_OFF - B_END) // unit)

    return pl.pallas_call(
        functools.partial(_inproj_kernel, gate_block=QKV_COLS // tn),
        out_shape=jax.ShapeDtypeStruct((t, n_out), BF16),
        grid=(t // tm, n_out // tn),
        in_specs=[pl.BlockSpec((tm, d), lambda i, j: (i, 0))]
                 + [pl.BlockSpec((pl.Element(1), pl.Element(tn // ns), pl.Element(d)),
                                 functools.partial(
                                     lambda i, j, q: (layer, first_col(j, q), 0), q=q))
                    for q in range(ns)],
        out_specs=pl.BlockSpec((tm, tn), lambda i, j: (i, j)),
        compiler_params=_cparams(("parallel", "arbitrary")),
        name="inproj",
    )(h, *([w_t] * ns))


def _osm(scores, values, carries):
    stats = []
    for s, (m, l, _) in zip(scores, carries):
        m_new = jnp.maximum(m, jnp.max(s, axis=-1, keepdims=True))
        alpha = jnp.exp2(m - m_new)
        p = jnp.exp2(s - m_new)
        stats.append((m_new, alpha * l + jnp.sum(p, axis=-1, keepdims=True), alpha, p))
    return tuple((m_new, l, alpha * acc + _dot(p.astype(BF16), v))
                 for (m_new, l, alpha, p), v, (_, _, acc) in zip(stats, values, carries))


def _causal_keep(t):
    r = lax.broadcasted_iota(jnp.int32, (t, t), 0)
    c = lax.broadcasted_iota(jnp.int32, (t, t), 1)
    return c <= r


def _diff_kernel(lam_ref, gain_ref, rot_ref, cq_ref, sq_ref, ck_ref, sk_ref,
                 q_ref, k_ref, v_ref, o_ref, kr_ref, *, tq, hp, lam_init):
    qi = pl.program_id(2)
    rot = rot_ref[...]
    heads = [slice(h * HEAD_DIM, (h + 1) * HEAD_DIM) for h in range(hp)]

    @pl.when(qi == 0)
    def _():
        for hs in heads:
            kr_ref[:, hs] = _rope(k_ref[:, hs], rot, ck_ref[...], sk_ref[...]).astype(BF16)

    scale2 = LOG2_E / math.sqrt(DIFF_DIM)
    lane = lax.broadcasted_iota(jnp.int32, (tq, HEAD_DIM), 1)
    chains = []
    for hs in heads:
        qf = _rope(q_ref[:, hs], rot, cq_ref[...], sq_ref[...])
        chains.append((jnp.where(lane < DIFF_DIM, qf, 0.0).astype(BF16), hs))
        chains.append((jnp.where(lane >= DIFF_DIM, qf, 0.0).astype(BF16), hs))

    def step(j, carry, masked):
        off = pl.multiple_of(j * tq, tq)
        scores = [_dot_nt(q, kr_ref[pl.ds(off, tq), hs]) * scale2 for q, hs in chains]
        if masked:
            keep = _causal_keep(tq)
            scores = [jnp.where(keep, s, -jnp.inf) for s in scores]
        return _osm(scores, [v_ref[pl.ds(off, tq), hs] for _, hs in chains], carry)

    init = (jnp.full((tq, 1), -jnp.inf, F32), jnp.zeros((tq, 1), F32),
            jnp.zeros((tq, HEAD_DIM), F32))
    carry = lax.fori_loop(0, qi, lambda j, c: step(j, c, False), tuple(init for _ in chains))
    carry = step(qi, carry, True)

    lam_rows = lam_ref[...]
    lam = (jnp.exp(jnp.sum(lam_rows[0:1] * lam_rows[1:2], axis=-1, keepdims=True))
           - jnp.exp(jnp.sum(lam_rows[2:3] * lam_rows[3:4], axis=-1, keepdims=True))
           + lam_init)
    for i, hs in enumerate(heads):
        (_, l1, a1), (_, l2, a2) = carry[2 * i], carry[2 * i + 1]
        out = a1 / l1 - lam * (a2 / l2)
        o_ref[:, hs] = (_rms(out, gain_ref[...]) * (1.0 - lam_init)).astype(o_ref.dtype)


def _diff_attention(proj, lam_rows, gain, tables, *, batch, seq, lam_init, tq=512, hp=4):
    c, s, rot = tables
    nq = seq // tq
    w = hp * HEAD_DIM
    cb = A_OFF // w
    ng = N_HEADS // hp
    return pl.pallas_call(
        functools.partial(_diff_kernel, tq=tq, hp=hp, lam_init=lam_init),
        out_shape=jax.ShapeDtypeStruct((batch * seq, BRANCH_WIDTH), BF16),
        grid=(batch, ng, nq),
        in_specs=[pl.BlockSpec((4, DIFF_DIM), lambda b, h, i: (0, 0)),
                  pl.BlockSpec((1, HEAD_DIM), lambda b, h, i: (0, 0)),
                  pl.BlockSpec((LANES, LANES), lambda b, h, i: (0, 0)),
                  pl.BlockSpec((tq, LANES), lambda b, h, i: (i, 0)),
                  pl.BlockSpec((tq, LANES), lambda b, h, i: (i, 0)),
                  pl.BlockSpec((seq, LANES), lambda b, h, i: (0, 0)),
                  pl.BlockSpec((seq, LANES), lambda b, h, i: (0, 0)),
                  pl.BlockSpec((tq, w), lambda b, h, i: (b * nq + i, cb + h)),
                  pl.BlockSpec((seq, w), lambda b, h, i: (b, cb + ng + h)),
                  pl.BlockSpec((seq, w), lambda b, h, i: (b, cb + 2 * ng + h))],
        out_specs=pl.BlockSpec((tq, w), lambda b, h, i: (b * nq + i, h)),
        scratch_shapes=[pltpu.VMEM((seq, w), BF16)],
        compiler_params=_cparams(("parallel", "parallel", "arbitrary")),
        name="diff_attention",
    )(lam_rows, gain.reshape(1, HEAD_DIM), rot, c, s, c, s, proj, proj, proj)


def _mla_prep_kernel(cq_ref, ckv_ref, kr_ref, dqn_ref, dkvn_ref, wuq_ref, wuk_ref, wuv_ref,
                     rot_ref, c_ref, s_ref, q_out, k_out, v_out):
    rot = rot_ref[...]
    c = c_ref[...]
    s = s_ref[...]
    hq = _rms(cq_ref[...].astype(F32), dqn_ref[...]).astype(BF16)
    hkv = _rms(ckv_ref[...].astype(F32), dkvn_ref[...]).astype(BF16)
    q = _dot(hq, wuq_ref[...])
    kn = _dot(hkv, wuk_ref[...])
    v_out[...] = _dot(hkv, wuv_ref[...]).astype(BF16)
    kr_lane = lax.broadcasted_iota(jnp.int32, kr_ref.shape, 1)
    kr = jnp.where(kr_lane < ROPE_DIM_B, kr_ref[...], jnp.zeros_like(kr_ref[...]))
    k_rope = _rope(kr, rot, c, s).astype(BF16)
    for h in range(N_HEADS):
        lo = h * MLA_QK
        q_out[:, lo:lo + NOPE_DIM] = q[:, lo:lo + NOPE_DIM].astype(BF16)
        q_out[:, lo + NOPE_DIM:lo + MLA_QK] = _rope(
            q[:, lo + NOPE_DIM:lo + MLA_QK].astype(BF16), rot, c, s).astype(BF16)
        k_out[:, lo:lo + NOPE_DIM] = kn[:, h * NOPE_DIM:(h + 1) * NOPE_DIM].astype(BF16)
        k_out[:, lo + NOPE_DIM:lo + MLA_QK] = k_rope


def _mla_prep(proj, dq_norm, dkv_norm, wuq, wuk, wuv, tables, *, seq, tm=512):
    c, s, rot = tables
    t = proj.shape[0]
    nb = seq // tm
    cb = B_OFF // Q_LORA
    full = lambda shape: pl.BlockSpec(shape, lambda i: (0, 0))
    return pl.pallas_call(
        _mla_prep_kernel,
        out_shape=(jax.ShapeDtypeStruct((t, N_HEADS * MLA_QK), BF16),
                   jax.ShapeDtypeStruct((t, N_HEADS * MLA_QK), BF16),
                   jax.ShapeDtypeStruct((t, N_HEADS * HEAD_DIM), BF16)),
        grid=(t // tm,),
        in_specs=[pl.BlockSpec((tm, Q_LORA), lambda i: (i, cb)),
                  pl.BlockSpec((tm, KV_LORA), lambda i: (i, cb + 1)),
                  pl.BlockSpec((tm, LANES), lambda i: (i, (B_OFF + Q_LORA + KV_LORA) // LANES)),
                  full((1, Q_LORA)), full((1, KV_LORA)),
                  full(wuq.shape), full(wuk.shape), full(wuv.shape),
                  full((LANES, LANES)),
                  pl.BlockSpec((tm, LANES), lambda i: (i % nb, 0)),
                  pl.BlockSpec((tm, LANES), lambda i: (i % nb, 0))],
        out_specs=(pl.BlockSpec((tm, N_HEADS * MLA_QK), lambda i: (i, 0)),
                   pl.BlockSpec((tm, N_HEADS * MLA_QK), lambda i: (i, 0)),
                   pl.BlockSpec((tm, N_HEADS * HEAD_DIM), lambda i: (i, 0))),
        compiler_params=_cparams(("parallel",)),
        name="mla_prep",
    )(proj, proj, proj, dq_norm.reshape(1, Q_LORA), dkv_norm.reshape(1, KV_LORA),
      wuq, wuk, wuv, rot, c, s)


def _flash_kernel(q_ref, k_ref, v_ref, o_ref, *, tq, hp, dqk, scale):
    qi = pl.program_id(2)
    qk = [slice(h * dqk, (h + 1) * dqk) for h in range(hp)]
    hv = [slice(h * HEAD_DIM, (h + 1) * HEAD_DIM) for h in range(hp)]
    qs = [q_ref[:, sl] for sl in qk]

    def step(j, carry, masked):
        off = pl.multiple_of(j * tq, tq)
        scores = [_dot_nt(q, k_ref[pl.ds(off, tq), ks]) * (scale * LOG2_E) for q, ks in zip(qs, qk)]
        if masked:
            keep = _causal_keep(tq)
            scores = [jnp.where(keep, s, -jnp.inf) for s in scores]
        return _osm(scores, [v_ref[pl.ds(off, tq), vs] for vs in hv], carry)

    init = (jnp.full((tq, 1), -jnp.inf, F32), jnp.zeros((tq, 1), F32),
            jnp.zeros((tq, HEAD_DIM), F32))
    carry = lax.fori_loop(0, qi, lambda j, c: step(j, c, False), tuple(init for _ in qs))
    carry = step(qi, carry, True)
    for vs, (_, l, acc) in zip(hv, carry):
        o_ref[:, vs] = (acc / l).astype(o_ref.dtype)


def _mla_attention(q, k, v, *, batch, seq, tq=512, hp=4):
    nq = seq // tq
    ng = N_HEADS // hp
    return pl.pallas_call(
        functools.partial(_flash_kernel, tq=tq, hp=hp, dqk=MLA_QK,
                          scale=1.0 / math.sqrt(NOPE_DIM + ROPE_DIM_B)),
        out_shape=jax.ShapeDtypeStruct((batch * seq, BRANCH_WIDTH), BF16),
        grid=(batch, ng, nq),
        in_specs=[pl.BlockSpec((tq, hp * MLA_QK), lambda b, h, i: (b * nq + i, h)),
                  pl.BlockSpec((seq, hp * MLA_QK), lambda b, h, i: (b, h)),
                  pl.BlockSpec((seq, hp * HEAD_DIM), lambda b, h, i: (b, h))],
        out_specs=pl.BlockSpec((tq, hp * HEAD_DIM), lambda b, h, i: (b * nq + i, h)),
        compiler_params=_cparams(("parallel", "parallel", "arbitrary")),
        name="mla_attention",
    )(q, k, v)


DIL_BLOCK = 128


def _dilated_kernel(rot_ref, c_ref, s_ref, *refs, seq):
    in_refs = refs[:9]
    o_ref = refs[9]
    qf, kf, vf, og, lse = refs[10:]
    rot = rot_ref[...]
    c = c_ref[...]
    s = s_ref[...]
    for g in range(N_DIL_GROUPS):
        qf[g] = _rope(in_refs[g][...], rot, c, s)
        kf[g] = _rope(in_refs[3 + g][...], rot, c, s)
        vf[g] = in_refs[6 + g][...].astype(F32)
    scale = 1.0 / math.sqrt(HEAD_DIM)
    blk = DIL_BLOCK

    def rows(start, size, stride):
        return pl.ds(start, size) if stride == 1 else pl.ds(start, size, stride=stride)

    for g, (window, dil) in enumerate(DIL_PAIRS):
        assert window == blk * dil
        span = blk * dil
        n_sub = seq // span
        nk = 2 * blk if n_sub > 1 else blk
        assert n_sub & (n_sub - 1) == 0
        q_rows, k_rows = [], []
        for r in range(dil):
            for cb in range(n_sub):
                q_rows.append(rows(r + cb * span, blk, dil))
                k_rows.append(rows(r + max(cb - 1, 0) * span, nk, dil))
        nb = len(q_rows)
        q = jnp.stack([qf[g, qr, :] for qr in q_rows]).astype(BF16)
        k = jnp.stack([kf[g, kr, :] for kr in k_rows]).astype(BF16)
        v = jnp.stack([vf[g, kr, :] for kr in k_rows]).astype(BF16)
        sc = jnp.einsum("bqd,bkd->bqk", q, k, preferred_element_type=F32) * (scale * LOG2_E)
        first = (lax.broadcasted_iota(jnp.int32, (nb, blk, nk), 0) & (n_sub - 1)) == 0
        dist = (lax.broadcasted_iota(jnp.int32, (nb, blk, nk), 1)
                - lax.broadcasted_iota(jnp.int32, (nb, blk, nk), 2)
                + jnp.where(first, 0, nk - blk))
        sc = jnp.where(dist >= 0, jnp.where(dist <= blk, sc, -jnp.inf), -jnp.inf)
        m = jnp.max(sc, axis=-1, keepdims=True)
        e = jnp.exp2(sc - m)
        den = jnp.sum(e, axis=-1, keepdims=True)
        o = jnp.einsum("bqk,bkd->bqd", e.astype(BF16), v, preferred_element_type=F32) / den
        lg = jnp.broadcast_to(m + jnp.log2(den), (nb, blk, HEAD_DIM))
        for i, qr in enumerate(q_rows):
            og[g, qr, :] = o[i]
            lse[g, qr, :] = lg[i]

    l0, l1, l2 = lse[0], lse[1], lse[2]
    mx = jnp.maximum(jnp.maximum(l0, l1), l2)
    w0, w1, w2 = jnp.exp2(l0 - mx), jnp.exp2(l1 - mx), jnp.exp2(l2 - mx)
    o_ref[...] = ((w0 * og[0] + w1 * og[1] + w2 * og[2]) / (w0 + w1 + w2)).astype(o_ref.dtype)


def _dilated_attention(proj, tables, *, batch, seq):
    c, s, rot = tables
    cb = C_OFF // HEAD_DIM
    nh = N_DIL_GROUPS * N_HEADS

    def col(kind, g):
        return lambda b, h: (b, cb + kind * nh + g * N_HEADS + h)

    in_specs = [pl.BlockSpec((LANES, LANES), lambda b, h: (0, 0)),
                pl.BlockSpec((seq, LANES), lambda b, h: (0, 0)),
                pl.BlockSpec((seq, LANES), lambda b, h: (0, 0))]
    for kind in range(3):
        for g in range(N_DIL_GROUPS):
            in_specs.append(pl.BlockSpec((seq, HEAD_DIM), col(kind, g)))
    return pl.pallas_call(
        functools.partial(_dilated_kernel, seq=seq),
        out_shape=jax.ShapeDtypeStruct((batch * seq, BRANCH_WIDTH), BF16),
        grid=(batch, N_HEADS),
        in_specs=in_specs,
        out_specs=pl.BlockSpec((seq, HEAD_DIM), lambda b, h: (b, h)),
        scratch_shapes=[pltpu.VMEM((N_DIL_GROUPS, seq, HEAD_DIM), F32) for _ in range(5)],
        compiler_params=_cparams(("parallel", "parallel")),
        name="dilated_attention",
    )(rot, c, s, *([proj] * 9))


def _stick_kernel(q_ref, k_ref, v_ref, o_ref, *, tq, hp, scale):
    qi = pl.program_id(2)
    heads = [slice(h * HEAD_DIM, (h + 1) * HEAD_DIM) for h in range(hp)]
    qs = [q_ref[:, hs] for hs in heads]
    r = lax.broadcasted_iota(jnp.int32, (tq, tq), 0)
    c = lax.broadcasted_iota(jnp.int32, (tq, tq), 1)
    later_keys = jnp.where(r > c, 1.0, 0.0).astype(BF16)

    def step(j, carry, diag):
        off = pl.multiple_of(j * tq, tq)
        strict = c < r
        z2 = [_dot_nt(q, k_ref[pl.ds(off, tq), hs]) * (scale * LOG2_E) for q, hs in zip(qs, heads)]
        sp2 = [jnp.maximum(z, 0.0) + jnp.log2(1.0 + jnp.exp2(-jnp.abs(z))) for z in z2]
        log_not = [jnp.where(strict, -sp, 0.0) if diag else -sp for sp in sp2]
        later = []
        for ln, (tail, _) in zip(log_not, carry):
            hi = ln.astype(BF16)
            lo = (ln - hi.astype(F32)).astype(BF16)
            later.append(_dot(hi, later_keys) + _dot(lo, later_keys) + tail)
        a = [jnp.exp2((z - sp) + lt) for z, sp, lt in zip(z2, sp2, later)]
        if diag:
            a = [jnp.where(strict, x, 0.0) for x in a]
        out = []
        for x, ln, hs, (tail, acc) in zip(a, log_not, heads, carry):
            acc = acc + _dot(x.astype(BF16), v_ref[pl.ds(off, tq), hs])
            out.append((tail + jnp.sum(ln, axis=-1, keepdims=True), acc))
        return tuple(out)

    init = (jnp.zeros((tq, 1), F32), jnp.zeros((tq, HEAD_DIM), F32))
    carry = step(qi, tuple(init for _ in heads), True)
    carry = lax.fori_loop(0, qi, lambda t, cr: step(qi - 1 - t, cr, False), carry)
    for hs, (_, acc) in zip(heads, carry):
        o_ref[:, hs] = acc.astype(o_ref.dtype)


def _stick_attention(proj, *, batch, seq, tq=256, hp=4):
    nq = seq // tq
    w = hp * HEAD_DIM
    cb = D_OFF // w
    ng = N_HEADS // hp
    return pl.pallas_call(
        functools.partial(_stick_kernel, tq=tq, hp=hp, scale=1.0 / math.sqrt(HEAD_DIM)),
        out_shape=jax.ShapeDtypeStruct((batch * seq, BRANCH_WIDTH), BF16),
        grid=(batch, ng, nq),
        in_specs=[pl.BlockSpec((tq, w), lambda b, h, i: (b * nq + i, cb + h)),
                  pl.BlockSpec((seq, w), lambda b, h, i: (b, cb + ng + h)),
                  pl.BlockSpec((seq, w), lambda b, h, i: (b, cb + 2 * ng + h))],
        out_specs=pl.BlockSpec((tq, w), lambda b, h, i: (b * nq + i, h)),
        compiler_params=_cparams(("parallel", "parallel", "arbitrary")),
        name="stick_attention",
    )(proj, proj, proj)


def _merge_kernel(ya, yb, yc, yd, g0, g1, g2, g3, wb_ref, o_ref):
    acc = None
    for i, (y, g) in enumerate(((ya, g0), (yb, g1), (yc, g2), (yd, g3))):
        t = g[...].astype(F32) * _dot(y[...], wb_ref[i])
        acc = t if acc is None else acc + t
    o_ref[...] = acc.astype(o_ref.dtype)


def _merge(ys, proj, wb, *, tm=1024, tn=1024):
    t = proj.shape[0]
    d = wb.shape[2]
    nn = d // tn
    g0 = QKV_COLS // tn
    y_spec = pl.BlockSpec((tm, BRANCH_WIDTH), lambda i, j: (i, 0))
    g_specs = [pl.BlockSpec((tm, tn), functools.partial(lambda i, j, b: (i, g0 + b * nn + j), b=b))
               for b in range(N_BRANCH)]
    return pl.pallas_call(
        _merge_kernel,
        out_shape=jax.ShapeDtypeStruct((t, d), BF16),
        grid=(t // tm, nn),
        in_specs=[y_spec] * N_BRANCH + g_specs
                 + [pl.BlockSpec((N_BRANCH, BRANCH_WIDTH, tn), lambda i, j: (0, 0, j))],
        out_specs=pl.BlockSpec((tm, tn), lambda i, j: (i, j)),
        compiler_params=_cparams(("parallel", "arbitrary")),
        name="branch_merge",
    )(*ys, proj, proj, proj, proj, wb)


def _outproj_kernel(x_ref, m_ref, w_ref, o_ref):
    o_ref[...] = x_ref[...] + _dot(m_ref[...], w_ref[...])


def _outproj(x, merged, w, *, tm=1024, tn=1024):
    t, d = x.shape
    return pl.pallas_call(
        _outproj_kernel,
        out_shape=jax.ShapeDtypeStruct((t, d), F32),
        grid=(t // tm, d // tn),
        in_specs=[pl.BlockSpec((tm, tn), lambda i, j: (i, j)),
                  pl.BlockSpec((tm, d), lambda i, j: (i, 0)),
                  pl.BlockSpec((d, tn), lambda i, j: (0, j))],
        out_specs=pl.BlockSpec((tm, tn), lambda i, j: (i, j)),
        compiler_params=_cparams(("parallel", "arbitrary")),
        name="out_proj",
    )(x, merged, w)


def _router_kernel(x_ref, gain_ref, wr_ref, idx_ref, gate_ref):
    h = _rms(x_ref[...], gain_ref[...])
    logits = jnp.dot(h, wr_ref[...], preferred_element_type=F32, precision=lax.Precision.HIGHEST)
    lane = lax.broadcasted_iota(jnp.int32, logits.shape, 1)
    lanef = lane.astype(F32)
    lg = jnp.where(lane < N_EXPERTS, logits, -jnp.inf)
    v1 = jnp.max(lg, axis=-1, keepdims=True)
    i1 = jnp.min(jnp.where(lg == v1, lanef, float(LANES)), axis=-1, keepdims=True)
    lg2 = jnp.where(lanef == i1, -jnp.inf, lg)
    v2 = jnp.max(lg2, axis=-1, keepdims=True)
    i2 = jnp.min(jnp.where(lg2 == v2, lanef, float(LANES)), axis=-1, keepdims=True)
    e2 = jnp.exp(v2 - v1)
    g1 = 1.0 / (1.0 + e2)
    g2 = e2 / (1.0 + e2)
    idx_ref[...] = jnp.where(lane == 0, i1, jnp.where(lane == 1, i2, 0.0)).astype(jnp.int32)
    gate_ref[...] = jnp.where(lane == 0, g1, jnp.where(lane == 1, g2, 0.0))


def _router(x, gain, w_router, *, tm=512):
    t, d = x.shape
    wr = jnp.zeros((d, LANES), F32).at[:, :N_EXPERTS].set(w_router)
    return pl.pallas_call(
        _router_kernel,
        out_shape=(jax.ShapeDtypeStruct((t, LANES), jnp.int32),
                   jax.ShapeDtypeStruct((t, LANES), F32)),
        grid=(t // tm,),
        in_specs=[pl.BlockSpec((tm, d), lambda i: (i, 0)),
                  pl.BlockSpec((1, d), lambda i: (0, 0)),
                  pl.BlockSpec((d, LANES), lambda i: (0, 0))],
        out_specs=(pl.BlockSpec((tm, LANES), lambda i: (i, 0)),
                   pl.BlockSpec((tm, LANES), lambda i: (i, 0))),
        compiler_params=_cparams(("parallel",)),
        name="router",
    )(x, gain.reshape(1, d), wr)


def _start_row_gather(idx_ref, base, n, src_hbm, dst, sem):
    def start(r, _):
        row = idx_ref[base + r]
        pltpu.make_async_copy(src_hbm.at[pl.ds(row, 1), :], dst.at[pl.ds(r, 1), :], sem).start()
        return 0

    lax.fori_loop(0, n, start, 0, unroll=8)


def _wait_row_gather(n, src_hbm, dst, sem):
    pltpu.make_async_copy(src_hbm.at[pl.ds(0, n), :], dst.at[pl.ds(0, n), :], sem).wait()


MOE_CHUNK = 512
MOE_VISIT_CHUNKS = 5
MOE_TF = 512
STAGE_ROWS = 256
SWIGLU_VMEM_LIMIT = 62 * 1024 * 1024
DENSE_CHUNK = 512
DENSE_VISIT_CHUNKS = 4
DENSE_TF = 512
WEIGHT_SPLIT = 1


def _moe_kernel(vis_e_ref, vis_ok_ref, vis_row_ref, vis_nch_ref, tail_ref, row_tok_ref,
                x_hbm, gain_ref, *refs, ch, dense):
    del vis_e_ref, vis_ok_ref
    ns = WEIGHT_SPLIT
    wg_refs, wu_refs, wo_refs = refs[:ns], refs[ns:2 * ns], refs[2 * ns:3 * ns]
    y_hbm, stage, h_ref, acc_ref, gsem, osem = refs[3 * ns:]
    v = pl.program_id(0)
    f = pl.program_id(1)
    nv = pl.num_programs(0)
    nf = pl.num_programs(1)
    nch = vis_nch_ref[v]
    row0 = vis_row_ref[v]
    active = nch > 0
    kq = h_ref.shape[1] // ns

    def rows(c):
        return pl.ds(pl.multiple_of(c * ch, ch), ch)

    def step(h):
        parts = [h[:, q * kq:(q + 1) * kq] for q in range(ns)]
        g = sum(_dot(hq, w[...].astype(BF16)) for hq, w in zip(parts, wg_refs))
        u = sum(_dot(hq, w[...].astype(BF16)) for hq, w in zip(parts, wu_refs))
        a = (g * _sigmoid(g) * u).astype(BF16)
        return jnp.concatenate([_dot(a, w[...].astype(BF16)) for w in wo_refs], axis=1)

    def out_copy(c):
        dst = y_hbm.at[pl.ds(pl.multiple_of(row0 + c * ch, ch), ch), :]
        return pltpu.make_async_copy(acc_ref.at[rows(c), :], dst, osem)

    @pl.when(jnp.logical_and(active, f == 0))
    def _():
        gch = stage.shape[1]
        n_piece = nch * (ch // gch)

        def fetch(g, slot):
            if dense:
                src = x_hbm.at[pl.ds(pl.multiple_of(row0 + g * gch, gch), gch), :]
                pltpu.make_async_copy(src, stage.at[slot], gsem.at[slot]).start()
            else:
                _start_row_gather(row_tok_ref, row0 + g * gch, gch, x_hbm, stage.at[slot], gsem.at[slot])

        fetch(0, 0)

        def piece(g, _):
            slot = lax.rem(g, 2)

            @pl.when(g + 1 < n_piece)
            def _():
                fetch(g + 1, 1 - slot)

            _wait_row_gather(gch, x_hbm, stage.at[slot], gsem.at[slot])
            x = stage[slot]
            piece_rows = pl.ds(pl.multiple_of(g * gch, gch), gch)
            h_ref[piece_rows, :] = _rms(x, gain_ref[...]).astype(BF16)
            if dense:
                acc_ref[piece_rows, :] = x
            return 0

        lax.fori_loop(0, n_piece, piece, 0)

        def body(c, _):
            if dense:
                acc_ref[rows(c), :] += step(h_ref[rows(c), :])
            else:
                acc_ref[rows(c), :] = step(h_ref[rows(c), :])
            return 0

        lax.fori_loop(0, nch, body, 0)

    @pl.when(jnp.logical_and(active, jnp.logical_and(f > 0, f < nf - 1)))
    def _():
        def body(c, _):
            acc_ref[rows(c), :] += step(h_ref[rows(c), :])
            return 0

        lax.fori_loop(0, nch, body, 0)

    @pl.when(jnp.logical_and(active, f == nf - 1))
    def _():
        def body(c, _):
            acc_ref[rows(c), :] += step(h_ref[rows(c), :])
            out_copy(c).start()
            return 0

        def drain(c, _):
            out_copy(c).wait()
            return 0

        lax.fori_loop(0, nch, body, 0)
        lax.fori_loop(0, nch, drain, 0)

    @pl.when(jnp.logical_and(v == nv - 1, f == nf - 1))
    def _():
        gch = stage.shape[1]
        first = tail_ref[0] * (ch // gch)
        n_tail = y_hbm.shape[0] // gch - first
        stage[0] = jnp.zeros(stage.shape[1:], stage.dtype)

        def tail_copy(c):
            dst = y_hbm.at[pl.ds(pl.multiple_of((first + c) * gch, gch), gch), :]
            return pltpu.make_async_copy(stage.at[0], dst, osem)

        def fill(c, _):
            tail_copy(c).start()
            return 0

        def drain(c, _):
            tail_copy(c).wait()
            return 0

        lax.fori_loop(0, n_tail, fill, 0)
        lax.fori_loop(0, n_tail, drain, 0)


def _dense_visits(t, ch, max_chunks):
    span = ch * max_chunks
    row0 = np.arange(0, t, span, dtype=np.int32)
    n_ch = np.minimum(max_chunks, (t - row0) // ch).astype(np.int32)
    zeros = np.zeros_like(row0)
    return tuple(jnp.asarray(a) for a in (zeros, zeros + 1, row0, n_ch,
                                          np.array([t // ch], np.int32), np.zeros((1,), np.int32)))


def _swiglu_rows(x, gain, tables, w_in, w_out, *, dense, ch, max_chunks, tf, n_visits=None):
    vis_e, vis_ok, vis_row, vis_nch, tail, row_tok = tables
    t, d = x.shape
    ff = w_out.shape[1]
    nf = ff // tf
    p = t if dense else row_tok.shape[0]
    ns = WEIGHT_SPLIT

    def wspec(shape, index):
        def index_map(v, f, e, ok, *_):
            return index(e[v], jnp.where(ok[v] > 0, f, nf - 1))
        return pl.BlockSpec(shape, index_map)

    wg_specs = [wspec((None, d // ns, tf), functools.partial(lambda e, f, q: (e, q, f), q=q))
                for q in range(ns)]
    wu_specs = [wspec((None, d // ns, tf), functools.partial(lambda e, f, q: (e, q, nf + f), q=q))
                for q in range(ns)]
    wo_specs = [wspec((None, tf, d // ns), functools.partial(lambda e, f, q: (e, f, q), q=q))
                for q in range(ns)]
    grid_spec = pltpu.PrefetchScalarGridSpec(
        num_scalar_prefetch=6,
        grid=(vis_e.shape[0] if n_visits is None else n_visits, nf),
        in_specs=[pl.BlockSpec(memory_space=pl.ANY),
                  pl.BlockSpec((1, d), lambda v, f, *_: (0, 0))] + wg_specs + wu_specs + wo_specs,
        out_specs=pl.BlockSpec(memory_space=pl.ANY),
        scratch_shapes=[pltpu.VMEM((2, STAGE_ROWS, d), F32),
                        pltpu.VMEM((max_chunks * ch, d), BF16),
                        pltpu.VMEM((max_chunks * ch, d), F32),
                        pltpu.SemaphoreType.DMA((2,)), pltpu.SemaphoreType.DMA],
    )
    return pl.pallas_call(
        functools.partial(_moe_kernel, ch=ch, dense=dense),
        out_shape=jax.ShapeDtypeStruct((p, d), F32),
        grid_spec=grid_spec,
        compiler_params=_cparams(("arbitrary", "arbitrary"), vmem=SWIGLU_VMEM_LIMIT),
        name="dense_ffn" if dense else "moe_experts",
    )(vis_e, vis_ok, vis_row, vis_nch, tail, row_tok, x, gain.reshape(1, d),
      *([w_in] * (2 * ns)), *([w_out] * ns))


def _combine_kernel(pos_ref, x_ref, gate_ref, y_hbm, gain_ref, o_ref, buf, sem, *, tm):
    i = pl.program_id(0)
    n = pl.num_programs(0)
    t = pos_ref.shape[0] // TOP_K
    slot = lax.rem(i, 2)

    def start(step, sl):
        for k in range(TOP_K):
            _start_row_gather(pos_ref, k * t + step * tm, tm, y_hbm, buf.at[sl, k], sem.at[sl, k])

    @pl.when(i == 0)
    def _():
        start(0, 0)

    @pl.when(i + 1 < n)
    def _():
        start(i + 1, 1 - slot)

    for k in range(TOP_K):
        _wait_row_gather(tm, y_hbm, buf.at[slot, k], sem.at[slot, k])
    g = gate_ref[...]
    x = x_ref[...] + (g[:, 0:1] * buf[slot, 0] + g[:, 1:2] * buf[slot, 1])
    if gain_ref is None:
        o_ref[...] = x
    else:
        o_ref[...] = _rms(x, gain_ref[...])


def _moe_combine(x, gate, y_rows, pos, final_gain, *, tm=256):
    t, d = x.shape
    in_specs = [pl.BlockSpec((tm, d), lambda i, ps: (i, 0)),
                pl.BlockSpec((tm, LANES), lambda i, ps: (i, 0)),
                pl.BlockSpec(memory_space=pl.ANY)]
    args = [x, gate, y_rows]
    if final_gain is not None:
        in_specs.append(pl.BlockSpec((1, d), lambda i, ps: (0, 0)))
        args.append(final_gain.reshape(1, d))
        body = functools.partial(_combine_kernel, tm=tm)
    else:
        def body(pos_ref, x_ref, gate_ref, y_hbm, o_ref, *scratch):
            _combine_kernel(pos_ref, x_ref, gate_ref, y_hbm, None, o_ref, *scratch, tm=tm)
    grid_spec = pltpu.PrefetchScalarGridSpec(
        num_scalar_prefetch=1,
        grid=(t // tm,),
        in_specs=in_specs,
        out_specs=pl.BlockSpec((tm, d), lambda i, ps: (i, 0)),
        scratch_shapes=[pltpu.VMEM((2, TOP_K, tm, d), F32), pltpu.SemaphoreType.DMA((2, TOP_K))],
    )
    return pl.pallas_call(
        body,
        out_shape=jax.ShapeDtypeStruct((t, d), F32),
        grid_spec=grid_spec,
        compiler_params=_cparams(("arbitrary",)),
        name="moe_combine",
    )(pos, *args)


def _final_norm_kernel(x_ref, g_ref, o_ref):
    o_ref[...] = _rms(x_ref[...], g_ref[...])


def _final_norm(x, gain, *, tm=512):
    t, d = x.shape
    return pl.pallas_call(
        _final_norm_kernel,
        out_shape=jax.ShapeDtypeStruct((t, d), F32),
        grid=(t // tm,),
        in_specs=[pl.BlockSpec((tm, d), lambda i: (i, 0)), pl.BlockSpec((1, d), lambda i: (0, 0))],
        out_specs=pl.BlockSpec((tm, d), lambda i: (i, 0)),
        compiler_params=_cparams(("parallel",)),
        name="final_norm",
    )(x, gain.reshape(1, d))


def _routing_tables(idx, ch, max_chunks):
    t = idx.shape[0]
    e_flat = idx[:, :TOP_K].T.reshape(-1)
    tok = jnp.tile(jnp.arange(t, dtype=jnp.int32), TOP_K)
    experts = jnp.arange(N_EXPERTS, dtype=jnp.int32)
    onehot = (e_flat[:, None] == experts[None, :]).astype(jnp.int32)
    rank = jnp.sum((jnp.cumsum(onehot, axis=0) - onehot) * onehot, axis=1)
    counts = jnp.sum(onehot, axis=0)
    n_chunk = (counts + ch - 1) // ch
    ends = jnp.cumsum(n_chunk * ch)
    starts = ends - n_chunk * ch
    dest = (starts[e_flat] + rank).astype(jnp.int32)
    p = TOP_K * t + N_EXPERTS * ch
    row_tok = jnp.zeros((p,), jnp.int32).at[dest].set(tok)

    n_vis = (n_chunk + max_chunks - 1) // max_chunks
    per_vis = (n_chunk + jnp.maximum(n_vis, 1) - 1) // jnp.maximum(n_vis, 1)
    v_end = jnp.cumsum(n_vis)
    v_start = v_end - n_vis
    nv_max = (p // ch + (max_chunks - 1) * N_EXPERTS) // max_chunks
    slot = jnp.arange(nv_max, dtype=jnp.int32)
    ok = slot < v_end[-1]
    e_of = jnp.minimum(jnp.sum((slot[:, None] >= v_end[None, :]).astype(jnp.int32), axis=1),
                       N_EXPERTS - 1)
    e_of = jnp.where(ok, e_of, e_of[v_end[-1] - 1])
    k = slot - v_start[e_of]
    n_ch = jnp.where(ok, jnp.clip(n_chunk[e_of] - k * per_vis[e_of], 0, per_vis[e_of]), 0)
    row0 = jnp.where(ok, starts[e_of] + k * per_vis[e_of] * ch, 0)
    tail = (ends[-1:] // ch).astype(jnp.int32)
    tables = (e_of.astype(jnp.int32), ok.astype(jnp.int32), row0.astype(jnp.int32),
              n_ch.astype(jnp.int32), tail, row_tok)
    return tables, dest, v_end[-1].astype(jnp.int32)


def _relayout_mla(w_uq, w_ukv):
    dq = NOPE_DIM + ROPE_DIM_B
    wq = w_uq.reshape(Q_LORA, N_HEADS, dq)
    wq = jnp.concatenate([wq, jnp.zeros((Q_LORA, N_HEADS, MLA_QK - dq), w_uq.dtype)], axis=2)
    wkv = w_ukv.reshape(KV_LORA, N_HEADS, NOPE_DIM + HEAD_DIM)
    wk = wkv[:, :, :NOPE_DIM].reshape(KV_LORA, N_HEADS * NOPE_DIM)
    wv = wkv[:, :, NOPE_DIM:].reshape(KV_LORA, N_HEADS * HEAD_DIM)
    return wq.reshape(Q_LORA, N_HEADS * MLA_QK).astype(BF16), wk.astype(BF16), wv.astype(BF16)


def kernel(x, norm_mix, w_in, dq_norm, dkv_norm, w_uq, w_ukv, lam_q1, lam_k1, lam_q2, lam_k2,
           diff_norm, w_branch, w_out, norm_ffn, w_dense_in, w_dense_out, w_router,
           w_moe_in, w_moe_out, norm_final):
    batch, seq, d = x.shape
    depth = w_in.shape[0]
    t = batch * seq
    xt = x.reshape(t, d)

    rot_a = DIFF_DIM // 4
    tab_a = _rope_tables(seq, ((0, rot_a), (DIFF_DIM, rot_a)))
    tab_b = _rope_tables(seq, ((0, ROPE_DIM_B),))
    tab_c = _rope_tables(seq, ((0, HEAD_DIM // 4),))
    w_in_t = jnp.swapaxes(w_in, 1, 2)

    for layer in range(depth):
        proj = _inproj(_rms_norm_bf16(xt, norm_mix[layer]), w_in_t, layer)

        lam_init = 0.8 - 0.6 * math.exp(-0.3 * layer)
        lam_rows = jnp.stack([lam_q1[layer], lam_k1[layer], lam_q2[layer], lam_k2[layer]])
        y_a = _diff_attention(proj, lam_rows, diff_norm[layer], tab_a,
                              batch=batch, seq=seq, lam_init=lam_init)

        wuq, wuk, wuv = _relayout_mla(w_uq[layer], w_ukv[layer])
        q_b, k_b, v_b = _mla_prep(proj, dq_norm[layer], dkv_norm[layer], wuq, wuk, wuv, tab_b, seq=seq)
        y_b = _mla_attention(q_b, k_b, v_b, batch=batch, seq=seq)

        y_c = _dilated_attention(proj, tab_c, batch=batch, seq=seq)
        y_d = _stick_attention(proj, batch=batch, seq=seq)

        merged = _merge((y_a, y_b, y_c, y_d), proj, w_branch[layer].astype(BF16))
        xt = _outproj(xt, merged, w_out[layer].astype(BF16))

        last = layer == depth - 1
        if layer % 2 == 0:
            m = layer // 2
            xt = _swiglu_rows(xt, norm_ffn[layer], _dense_visits(t, DENSE_CHUNK, DENSE_VISIT_CHUNKS),
                              w_dense_in[m:m + 1], w_dense_out[m:m + 1], dense=True,
                              ch=DENSE_CHUNK, max_chunks=DENSE_VISIT_CHUNKS, tf=DENSE_TF)
            if last:
                xt = _final_norm(xt, norm_final)
        else:
            m = layer // 2
            idx, gate = _router(xt, norm_ffn[layer], w_router[m])
            tables, dest, n_vis = _routing_tables(idx, MOE_CHUNK, MOE_VISIT_CHUNKS)
            y_rows = _swiglu_rows(xt, norm_ffn[layer], tables, w_moe_in[m], w_moe_out[m], dense=False,
                                  ch=MOE_CHUNK, max_chunks=MOE_VISIT_CHUNKS, tf=MOE_TF, n_visits=n_vis)
            xt = _moe_combine(xt, gate, y_rows, dest, norm_final if last else None)
    return xt.reshape(batch, seq, d)
```

```python
import functools
import math

import numpy as np
import jax
import jax.numpy as jnp
from jax import lax
from jax.experimental import pallas as pl
from jax.experimental.pallas import tpu as pltpu

F32 = jnp.float32
BF16 = jnp.bfloat16

HEAD_DIM = 128
ROPE_THETA = 500000.0
NORM_EPS = 1e-6
N_BRANCH = 4
BRANCH_WIDTH = 512
N_HEADS = 4
DIFF_DIM = 64
Q_LORA = 512
KV_LORA = 512
NOPE_DIM = 128
ROPE_DIM_B = 64
MLA_QK = 256
DIL_PAIRS = ((128, 1), (512, 4), (2048, 16))
N_DIL_GROUPS = 3
N_EXPERTS = 8
TOP_K = 2
LANES = 128
LOG2_E = math.log2(math.e)

A_OFF = 0
B_OFF = 1536
C_OFF = 3072
D_OFF = 7680
QKV_COLS = 9216

VMEM_LIMIT = 56 * 1024 * 1024


def _cparams(sem, vmem=VMEM_LIMIT):
    return pltpu.CompilerParams(dimension_semantics=sem, vmem_limit_bytes=vmem)


def _dot(a, b):
    return jnp.dot(a, b, preferred_element_type=F32)


def _dot_nt(a, b):
    return lax.dot_general(a, b, (((1,), (1,)), ((), ())), preferred_element_type=F32)


def _rms(x, gain):
    return x * lax.rsqrt(jnp.mean(x * x, axis=-1, keepdims=True) + NORM_EPS) * gain


def _sigmoid(x):
    return 1.0 / (1.0 + jnp.exp(-x))


def _rope_tables(seq, segments):
    pos = np.arange(seq, dtype=np.float64)
    c = np.ones((seq, LANES), np.float64)
    s = np.zeros((seq, LANES), np.float64)
    r = np.zeros((LANES, LANES), np.float32)
    for start, rot in segments:
        half = rot // 2
        inv_freq = ROPE_THETA ** (-np.arange(0, rot, 2, dtype=np.float64) / rot)
        ang = pos[:, None] * inv_freq[None, :]
        cos, sin = np.cos(ang), np.sin(ang)
        c[:, start:start + half] = cos
        c[:, start + half:start + rot] = cos
        s[:, start:start + half] = -sin
        s[:, start + half:start + rot] = sin
        for i in range(half):
            r[start + half + i, start + i] = 1.0
            r[start + i, start + half + i] = 1.0
    return jnp.asarray(c, F32), jnp.asarray(s, F32), jnp.asarray(r, BF16)


def _rope(x_bf16, rot, c, s):
    return x_bf16.astype(F32) * c + _dot(x_bf16, rot) * s


def _norm_kernel(x_ref, g_ref, o_ref):
    o_ref[...] = _rms(x_ref[...], g_ref[...]).astype(o_ref.dtype)


def _rms_norm_bf16(x, gain, *, tm=512):
    t, d = x.shape
    return pl.pallas_call(
        _norm_kernel,
        out_shape=jax.ShapeDtypeStruct((t, d), BF16),
        grid=(t // tm,),
        in_specs=[pl.BlockSpec((tm, d), lambda i: (i, 0)), pl.BlockSpec((1, d), lambda i: (0, 0))],
        out_specs=pl.BlockSpec((tm, d), lambda i: (i, 0)),
        compiler_params=_cparams(("parallel",)),
        name="mix_norm",
    )(x, gain.reshape(1, d))


B_END = B_OFF + Q_LORA + KV_LORA + ROPE_DIM_B


def _inproj_kernel(h_ref, *refs, gate_block):
    w_refs, o_ref = refs[:-1], refs[-1]
    j = pl.program_id(1)
    h = h_ref[...]
    for q, w_ref in enumerate(w_refs):
        n = w_ref.shape[1]
        acc = _dot_nt(h, w_ref[0].astype(BF16))
        o_ref[:, q * n:(q + 1) * n] = jnp.where(j >= gate_block, _sigmoid(acc), acc).astype(o_ref.dtype)


def _inproj(h, w_t, layer, *, tm=2048, tn=1024):
    t, d = h.shape
    n_out = w_t.shape[1] + C_OFF - B_END
    assert C_OFF % tn == 0 and n_out % tn == 0
    ns = WEIGHT_SPLIT

    unit = math.gcd(tn, C_OFF - B_END)

    def first_col(j, q):
        k = j * (tn // unit) + q * (tn // ns // unit)
        return unit * jnp.where(j * tn < C_OFF, k, k - (C_OFF - B_END) // unit)

    return pl.pallas_call(
        functools.partial(_inproj_kernel, gate_block=QKV_COLS // tn),
        out_shape=jax.ShapeDtypeStruct((t, n_out), BF16),
        grid=(t // tm, n_out // tn),
        in_specs=[pl.BlockSpec((tm, d), lambda i, j: (i, 0))]
                 + [pl.BlockSpec((pl.Element(1), pl.Element(tn // ns), pl.Element(d)),
                                 functools.partial(
                                     lambda i, j, q: (layer, first_col(j, q), 0), q=q))
                    for q in range(ns)],
        out_specs=pl.BlockSpec((tm, tn), lambda i, j: (i, j)),
        compiler_params=_cparams(("parallel", "arbitrary")),
        name="inproj",
    )(h, *([w_t] * ns))


def _osm(scores, values, carries):
    stats = []
    for s, (m, l, _) in zip(scores, carries):
        m_new = jnp.maximum(m, jnp.max(s, axis=-1, keepdims=True))
        alpha = jnp.exp2(m - m_new)
        p = jnp.exp2(s - m_new)
        stats.append((m_new, alpha * l + jnp.sum(p, axis=-1, keepdims=True), alpha, p))
    return tuple((m_new, l, alpha * acc + _dot(p.astype(BF16), v))
                 for (m_new, l, alpha, p), v, (_, _, acc) in zip(stats, values, carries))


def _causal_keep(t):
    r = lax.broadcasted_iota(jnp.int32, (t, t), 0)
    c = lax.broadcasted_iota(jnp.int32, (t, t), 1)
    return c <= r


def _diff_kernel(lam_ref, gain_ref, rot_ref, cq_ref, sq_ref, ck_ref, sk_ref,
                 q_ref, k_ref, v_ref, o_ref, kr_ref, *, tq, hp, lam_init):
    qi = pl.program_id(2)
    rot = rot_ref[...]
    heads = [slice(h * HEAD_DIM, (h + 1) * HEAD_DIM) for h in range(hp)]

    @pl.when(qi == 0)
    def _():
        for hs in heads:
            kr_ref[:, hs] = _rope(k_ref[:, hs], rot, ck_ref[...], sk_ref[...]).astype(BF16)

    scale2 = LOG2_E / math.sqrt(DIFF_DIM)
    lane = lax.broadcasted_iota(jnp.int32, (tq, HEAD_DIM), 1)
    chains = []
    for hs in heads:
        qf = _rope(q_ref[:, hs], rot, cq_ref[...], sq_ref[...])
        chains.append((jnp.where(lane < DIFF_DIM, qf, 0.0).astype(BF16), hs))
        chains.append((jnp.where(lane >= DIFF_DIM, qf, 0.0).astype(BF16), hs))

    def step(j, carry, masked):
        off = pl.multiple_of(j * tq, tq)
        scores = [_dot_nt(q, kr_ref[pl.ds(off, tq), hs]) * scale2 for q, hs in chains]
        if masked:
            keep = _causal_keep(tq)
            scores = [jnp.where(keep, s, -jnp.inf) for s in scores]
        return _osm(scores, [v_ref[pl.ds(off, tq), hs] for _, hs in chains], carry)

    init = (jnp.full((tq, 1), -jnp.inf, F32), jnp.zeros((tq, 1), F32),
            jnp.zeros((tq, HEAD_DIM), F32))
    carry = lax.fori_loop(0, qi, lambda j, c: step(j, c, False), tuple(init for _ in chains))
    carry = step(qi, carry, True)

    lam_rows = lam_ref[...]
    lam = (jnp.exp(jnp.sum(lam_rows[0:1] * lam_rows[1:2], axis=-1, keepdims=True))
           - jnp.exp(jnp.sum(lam_rows[2:3] * lam_rows[3:4], axis=-1, keepdims=True))
           + lam_init)
    for i, hs in enumerate(heads):
        (_, l1, a1), (_, l2, a2) = carry[2 * i], carry[2 * i + 1]
        out = a1 / l1 - lam * (a2 / l2)
        o_ref[:, hs] = (_rms(out, gain_ref[...]) * (1.0 - lam_init)).astype(o_ref.dtype)


def _diff_attention(proj, lam_rows, gain, tables, *, batch, seq, lam_init, tq=512, hp=4):
    c, s, rot = tables
    nq = seq // tq
    w = hp * HEAD_DIM
    cb = A_OFF // w
    ng = N_HEADS // hp
    return pl.pallas_call(
        functools.partial(_diff_kernel, tq=tq, hp=hp, lam_init=lam_init),
        out_shape=jax.ShapeDtypeStruct((batch * seq, BRANCH_WIDTH), BF16),
        grid=(batch, ng, nq),
        in_specs=[pl.BlockSpec((4, DIFF_DIM), lambda b, h, i: (0, 0)),
                  pl.BlockSpec((1, HEAD_DIM), lambda b, h, i: (0, 0)),
                  pl.BlockSpec((LANES, LANES), lambda b, h, i: (0, 0)),
                  pl.BlockSpec((tq, LANES), lambda b, h, i: (i, 0)),
                  pl.BlockSpec((tq, LANES), lambda b, h, i: (i, 0)),
                  pl.BlockSpec((seq, LANES), lambda b, h, i: (0, 0)),
                  pl.BlockSpec((seq, LANES), lambda b, h, i: (0, 0)),
                  pl.BlockSpec((tq, w), lambda b, h, i: (b * nq + i, cb + h)),
                  pl.BlockSpec((seq, w), lambda b, h, i: (b, cb + ng + h)),
                  pl.BlockSpec((seq, w), lambda b, h, i: (b, cb + 2 * ng + h))],
        out_specs=pl.BlockSpec((tq, w), lambda b, h, i: (b * nq + i, h)),
        scratch_shapes=[pltpu.VMEM((seq, w), BF16)],
        compiler_params=_cparams(("parallel", "parallel", "arbitrary")),
        name="diff_attention",
    )(lam_rows, gain.reshape(1, HEAD_DIM), rot, c, s, c, s, proj, proj, proj)


def _mla_prep_kernel(cq_ref, ckv_ref, kr_ref, dqn_ref, dkvn_ref, wuq_ref, wuk_ref, wuv_ref,
                     rot_ref, c_ref, s_ref, q_out, k_out, v_out):
    rot = rot_ref[...]
    c = c_ref[...]
    s = s_ref[...]
    hq = _rms(cq_ref[...].astype(F32), dqn_ref[...]).astype(BF16)
    hkv = _rms(ckv_ref[...].astype(F32), dkvn_ref[...]).astype(BF16)
    q = _dot(hq, wuq_ref[...])
    kn = _dot(hkv, wuk_ref[...])
    v_out[...] = _dot(hkv, wuv_ref[...]).astype(BF16)
    kr_lane = lax.broadcasted_iota(jnp.int32, kr_ref.shape, 1)
    kr = jnp.where(kr_lane < ROPE_DIM_B, kr_ref[...], jnp.zeros_like(kr_ref[...]))
    k_rope = _rope(kr, rot, c, s).astype(BF16)
    for h in range(N_HEADS):
        lo = h * MLA_QK
        q_out[:, lo:lo + NOPE_DIM] = q[:, lo:lo + NOPE_DIM].astype(BF16)
        q_out[:, lo + NOPE_DIM:lo + MLA_QK] = _rope(
            q[:, lo + NOPE_DIM:lo + MLA_QK].astype(BF16), rot, c, s).astype(BF16)
        k_out[:, lo:lo + NOPE_DIM] = kn[:, h * NOPE_DIM:(h + 1) * NOPE_DIM].astype(BF16)
        k_out[:, lo + NOPE_DIM:lo + MLA_QK] = k_rope


def _mla_prep(proj, dq_norm, dkv_norm, wuq, wuk, wuv, tables, *, seq, tm=512):
    c, s, rot = tables
    t = proj.shape[0]
    nb = seq // tm
    cb = B_OFF // Q_LORA
    full = lambda shape: pl.BlockSpec(shape, lambda i: (0, 0))
    return pl.pallas_call(
        _mla_prep_kernel,
        out_shape=(jax.ShapeDtypeStruct((t, N_HEADS * MLA_QK), BF16),
                   jax.ShapeDtypeStruct((t, N_HEADS * MLA_QK), BF16),
                   jax.ShapeDtypeStruct((t, N_HEADS * HEAD_DIM), BF16)),
        grid=(t // tm,),
        in_specs=[pl.BlockSpec((tm, Q_LORA), lambda i: (i, cb)),
                  pl.BlockSpec((tm, KV_LORA), lambda i: (i, cb + 1)),
                  pl.BlockSpec((tm, LANES), lambda i: (i, (B_OFF + Q_LORA + KV_LORA) // LANES)),
                  full((1, Q_LORA)), full((1, KV_LORA)),
                  full(wuq.shape), full(wuk.shape), full(wuv.shape),
                  full((LANES, LANES)),
                  pl.BlockSpec((tm, LANES), lambda i: (i % nb, 0)),
                  pl.BlockSpec((tm, LANES), lambda i: (i % nb, 0))],
        out_specs=(pl.BlockSpec((tm, N_HEADS * MLA_QK), lambda i: (i, 0)),
                   pl.BlockSpec((tm, N_HEADS * MLA_QK), lambda i: (i, 0)),
                   pl.BlockSpec((tm, N_HEADS * HEAD_DIM), lambda i: (i, 0))),
        compiler_params=_cparams(("parallel",)),
        name="mla_prep",
    )(proj, proj, proj, dq_norm.reshape(1, Q_LORA), dkv_norm.reshape(1, KV_LORA),
      wuq, wuk, wuv, rot, c, s)


def _flash_kernel(q_ref, k_ref, v_ref, o_ref, *, tq, hp, dqk, scale):
    qi = pl.program_id(2)
    qk = [slice(h * dqk, (h + 1) * dqk) for h in range(hp)]
    hv = [slice(h * HEAD_DIM, (h + 1) * HEAD_DIM) for h in range(hp)]
    qs = [q_ref[:, sl] for sl in qk]

    def step(j, carry, masked):
        off = pl.multiple_of(j * tq, tq)
        scores = [_dot_nt(q, k_ref[pl.ds(off, tq), ks]) * (scale * LOG2_E) for q, ks in zip(qs, qk)]
        if masked:
            keep = _causal_keep(tq)
            scores = [jnp.where(keep, s, -jnp.inf) for s in scores]
        return _osm(scores, [v_ref[pl.ds(off, tq), vs] for vs in hv], carry)

    init = (jnp.full((tq, 1), -jnp.inf, F32), jnp.zeros((tq, 1), F32),
            jnp.zeros((tq, HEAD_DIM), F32))
    carry = lax.fori_loop(0, qi, lambda j, c: step(j, c, False), tuple(init for _ in qs))
    carry = step(qi, carry, True)
    for vs, (_, l, acc) in zip(hv, carry):
        o_ref[:, vs] = (acc / l).astype(o_ref.dtype)


def _mla_attention(q, k, v, *, batch, seq, tq=512, hp=4):
    nq = seq // tq
    ng = N_HEADS // hp
    return pl.pallas_call(
        functools.partial(_flash_kernel, tq=tq, hp=hp, dqk=MLA_QK,
                          scale=1.0 / math.sqrt(NOPE_DIM + ROPE_DIM_B)),
        out_shape=jax.ShapeDtypeStruct((batch * seq, BRANCH_WIDTH), BF16),
        grid=(batch, ng, nq),
        in_specs=[pl.BlockSpec((tq, hp * MLA_QK), lambda b, h, i: (b * nq + i, h)),
                  pl.BlockSpec((seq, hp * MLA_QK), lambda b, h, i: (b, h)),
                  pl.BlockSpec((seq, hp * HEAD_DIM), lambda b, h, i: (b, h))],
        out_specs=pl.BlockSpec((tq, hp * HEAD_DIM), lambda b, h, i: (b * nq + i, h)),
        compiler_params=_cparams(("parallel", "parallel", "arbitrary")),
        name="mla_attention",
    )(q, k, v)


DIL_BLOCK = 128


def _dilated_kernel(rot_ref, c_ref, s_ref, *refs, seq):
    in_refs = refs[:9]
    o_ref = refs[9]
    qf, kf, vf, og, lse = refs[10:]
    rot = rot_ref[...]
    c = c_ref[...]
    s = s_ref[...]
    for g in range(N_DIL_GROUPS):
        qf[g] = _rope(in_refs[g][...], rot, c, s)
        kf[g] = _rope(in_refs[3 + g][...], rot, c, s)
        vf[g] = in_refs[6 + g][...].astype(F32)
    scale = 1.0 / math.sqrt(HEAD_DIM)
    blk = DIL_BLOCK

    def rows(start, size, stride):
        return pl.ds(start, size) if stride == 1 else pl.ds(start, size, stride=stride)

    for g, (window, dil) in enumerate(DIL_PAIRS):
        assert window == blk * dil
        span = blk * dil
        n_sub = seq // span
        nk = 2 * blk if n_sub > 1 else blk
        assert n_sub & (n_sub - 1) == 0
        q_rows, k_rows = [], []
        for r in range(dil):
            for cb in range(n_sub):
                q_rows.append(rows(r + cb * span, blk, dil))
                k_rows.append(rows(r + max(cb - 1, 0) * span, nk, dil))
        nb = len(q_rows)
        q = jnp.stack([qf[g, qr, :] for qr in q_rows]).astype(BF16)
        k = jnp.stack([kf[g, kr, :] for kr in k_rows]).astype(BF16)
        v = jnp.stack([vf[g, kr, :] for kr in k_rows]).astype(BF16)
        sc = jnp.einsum("bqd,bkd->bqk", q, k, preferred_element_type=F32) * (scale * LOG2_E)
        first = (lax.broadcasted_iota(jnp.int32, (nb, blk, nk), 0) & (n_sub - 1)) == 0
        dist = (lax.broadcasted_iota(jnp.int32, (nb, blk, nk), 1)
                - lax.broadcasted_iota(jnp.int32, (nb, blk, nk), 2)
                + jnp.where(first, 0, nk - blk))
        sc = jnp.where(dist >= 0, jnp.where(dist <= blk, sc, -jnp.inf), -jnp.inf)
        m = jnp.max(sc, axis=-1, keepdims=True)
        e = jnp.exp2(sc - m)
        den = jnp.sum(e, axis=-1, keepdims=True)
        o = jnp.einsum("bqk,bkd->bqd", e.astype(BF16), v, preferred_element_type=F32) / den
        lg = jnp.broadcast_to(m + jnp.log2(den), (nb, blk, HEAD_DIM))
        for i, qr in enumerate(q_rows):
            og[g, qr, :] = o[i]
            lse[g, qr, :] = lg[i]

    l0, l1, l2 = lse[0], lse[1], lse[2]
    mx = jnp.maximum(jnp.maximum(l0, l1), l2)
    w0, w1, w2 = jnp.exp2(l0 - mx), jnp.exp2(l1 - mx), jnp.exp2(l2 - mx)
    o_ref[...] = ((w0 * og[0] + w1 * og[1] + w2 * og[2]) / (w0 + w1 + w2)).astype(o_ref.dtype)


def _dilated_attention(proj, tables, *, batch, seq):
    c, s, rot = tables
    cb = C_OFF // HEAD_DIM
    nh = N_DIL_GROUPS * N_HEADS

    def col(kind, g):
        return lambda b, h: (b, cb + kind * nh + g * N_HEADS + h)

    in_specs = [pl.BlockSpec((LANES, LANES), lambda b, h: (0, 0)),
                pl.BlockSpec((seq, LANES), lambda b, h: (0, 0)),
                pl.BlockSpec((seq, LANES), lambda b, h: (0, 0))]
    for kind in range(3):
        for g in range(N_DIL_GROUPS):
            in_specs.append(pl.BlockSpec((seq, HEAD_DIM), col(kind, g)))
    return pl.pallas_call(
        functools.partial(_dilated_kernel, seq=seq),
        out_shape=jax.ShapeDtypeStruct((batch * seq, BRANCH_WIDTH), BF16),
        grid=(batch, N_HEADS),
        in_specs=in_specs,
        out_specs=pl.BlockSpec((seq, HEAD_DIM), lambda b, h: (b, h)),
        scratch_shapes=[pltpu.VMEM((N_DIL_GROUPS, seq, HEAD_DIM), F32) for _ in range(5)],
        compiler_params=_cparams(("parallel", "parallel")),
        name="dilated_attention",
    )(rot, c, s, *([proj] * 9))


def _stick_kernel(q_ref, k_ref, v_ref, o_ref, *, tq, hp, scale):
    qi = pl.program_id(2)
    heads = [slice(h * HEAD_DIM, (h + 1) * HEAD_DIM) for h in range(hp)]
    qs = [q_ref[:, hs] for hs in heads]
    r = lax.broadcasted_iota(jnp.int32, (tq, tq), 0)
    c = lax.broadcasted_iota(jnp.int32, (tq, tq), 1)
    later_keys = jnp.where(r > c, 1.0, 0.0).astype(BF16)

    def step(j, carry, diag):
        off = pl.multiple_of(j * tq, tq)
        strict = c < r
        z2 = [_dot_nt(q, k_ref[pl.ds(off, tq), hs]) * (scale * LOG2_E) for q, hs in zip(qs, heads)]
        sp2 = [jnp.maximum(z, 0.0) + jnp.log2(1.0 + jnp.exp2(-jnp.abs(z))) for z in z2]
        log_not = [jnp.where(strict, -sp, 0.0) if diag else -sp for sp in sp2]
        later = []
        for ln, (tail, _) in zip(log_not, carry):
            hi = ln.astype(BF16)
            lo = (ln - hi.astype(F32)).astype(BF16)
            later.append(_dot(hi, later_keys) + _dot(lo, later_keys) + tail)
        a = [jnp.exp2((z - sp) + lt) for z, sp, lt in zip(z2, sp2, later)]
        if diag:
            a = [jnp.where(strict, x, 0.0) for x in a]
        out = []
        for x, ln, hs, (tail, acc) in zip(a, log_not, heads, carry):
            acc = acc + _dot(x.astype(BF16), v_ref[pl.ds(off, tq), hs])
            out.append((tail + jnp.sum(ln, axis=-1, keepdims=True), acc))
        return tuple(out)

    init = (jnp.zeros((tq, 1), F32), jnp.zeros((tq, HEAD_DIM), F32))
    carry = step(qi, tuple(init for _ in heads), True)
    carry = lax.fori_loop(0, qi, lambda t, cr: step(qi - 1 - t, cr, False), carry)
    for hs, (_, acc) in zip(heads, carry):
        o_ref[:, hs] = acc.astype(o_ref.dtype)


def _stick_attention(proj, *, batch, seq, tq=256, hp=4):
    nq = seq // tq
    w = hp * HEAD_DIM
    cb = D_OFF // w
    ng = N_HEADS // hp
    return pl.pallas_call(
        functools.partial(_stick_kernel, tq=tq, hp=hp, scale=1.0 / math.sqrt(HEAD_DIM)),
        out_shape=jax.ShapeDtypeStruct((batch * seq, BRANCH_WIDTH), BF16),
        grid=(batch, ng, nq),
        in_specs=[pl.BlockSpec((tq, w), lambda b, h, i: (b * nq + i, cb + h)),
                  pl.BlockSpec((seq, w), lambda b, h, i: (b, cb + ng + h)),
                  pl.BlockSpec((seq, w), lambda b, h, i: (b, cb + 2 * ng + h))],
        out_specs=pl.BlockSpec((tq, w), lambda b, h, i: (b * nq + i, h)),
        compiler_params=_cparams(("parallel", "parallel", "arbitrary")),
        name="stick_attention",
    )(proj, proj, proj)


def _merge_kernel(ya, yb, yc, yd, g0, g1, g2, g3, wb_ref, o_ref):
    acc = None
    for i, (y, g) in enumerate(((ya, g0), (yb, g1), (yc, g2), (yd, g3))):
        t = g[...].astype(F32) * _dot(y[...], wb_ref[i])
        acc = t if acc is None else acc + t
    o_ref[...] = acc.astype(o_ref.dtype)


def _merge(ys, proj, wb, *, tm=1024, tn=1024):
    t = proj.shape[0]
    d = wb.shape[2]
    nn = d // tn
    g0 = QKV_COLS // tn
    y_spec = pl.BlockSpec((tm, BRANCH_WIDTH), lambda i, j: (i, 0))
    g_specs = [pl.BlockSpec((tm, tn), functools.partial(lambda i, j, b: (i, g0 + b * nn + j), b=b))
               for b in range(N_BRANCH)]
    return pl.pallas_call(
        _merge_kernel,
        out_shape=jax.ShapeDtypeStruct((t, d), BF16),
        grid=(t // tm, nn),
        in_specs=[y_spec] * N_BRANCH + g_specs
                 + [pl.BlockSpec((N_BRANCH, BRANCH_WIDTH, tn), lambda i, j: (0, 0, j))],
        out_specs=pl.BlockSpec((tm, tn), lambda i, j: (i, j)),
        compiler_params=_cparams(("parallel", "arbitrary")),
        name="branch_merge",
    )(*ys, proj, proj, proj, proj, wb)


def _outproj_kernel(x_ref, m_ref, w_ref, o_ref):
    o_ref[...] = x_ref[...] + _dot(m_ref[...], w_ref[...])


def _outproj(x, merged, w, *, tm=1024, tn=1024):
    t, d = x.shape
    return pl.pallas_call(
        _outproj_kernel,
        out_shape=jax.ShapeDtypeStruct((t, d), F32),
        grid=(t // tm, d // tn),
        in_specs=[pl.BlockSpec((tm, tn), lambda i, j: (i, j)),
                  pl.BlockSpec((tm, d), lambda i, j: (i, 0)),
                  pl.BlockSpec((d, tn), lambda i, j: (0, j))],
        out_specs=pl.BlockSpec((tm, tn), lambda i, j: (i, j)),
        compiler_params=_cparams(("parallel", "arbitrary")),
        name="out_proj",
    )(x, merged, w)


def _router_kernel(x_ref, gain_ref, wr_ref, idx_ref, gate_ref):
    h = _rms(x_ref[...], gain_ref[...])
    logits = jnp.dot(h, wr_ref[...], preferred_element_type=F32, precision=lax.Precision.HIGHEST)
    lane = lax.broadcasted_iota(jnp.int32, logits.shape, 1)
    lanef = lane.astype(F32)
    lg = jnp.where(lane < N_EXPERTS, logits, -jnp.inf)
    v1 = jnp.max(lg, axis=-1, keepdims=True)
    i1 = jnp.min(jnp.where(lg == v1, lanef, float(LANES)), axis=-1, keepdims=True)
    lg2 = jnp.where(lanef == i1, -jnp.inf, lg)
    v2 = jnp.max(lg2, axis=-1, keepdims=True)
    i2 = jnp.min(jnp.where(lg2 == v2, lanef, float(LANES)), axis=-1, keepdims=True)
    e2 = jnp.exp(v2 - v1)
    g1 = 1.0 / (1.0 + e2)
    g2 = e2 / (1.0 + e2)
    idx_ref[...] = jnp.where(lane == 0, i1, jnp.where(lane == 1, i2, 0.0)).astype(jnp.int32)
    gate_ref[...] = jnp.where(lane == 0, g1, jnp.where(lane == 1, g2, 0.0))


def _router(x, gain, w_router, *, tm=512):
    t, d = x.shape
    wr = jnp.zeros((d, LANES), F32).at[:, :N_EXPERTS].set(w_router)
    return pl.pallas_call(
        _router_kernel,
        out_shape=(jax.ShapeDtypeStruct((t, LANES), jnp.int32),
                   jax.ShapeDtypeStruct((t, LANES), F32)),
        grid=(t // tm,),
        in_specs=[pl.BlockSpec((tm, d), lambda i: (i, 0)),
                  pl.BlockSpec((1, d), lambda i: (0, 0)),
                  pl.BlockSpec((d, LANES), lambda i: (0, 0))],
        out_specs=(pl.BlockSpec((tm, LANES), lambda i: (i, 0)),
                   pl.BlockSpec((tm, LANES), lambda i: (i, 0))),
        compiler_params=_cparams(("parallel",)),
        name="router",
    )(x, gain.reshape(1, d), wr)


def _start_row_gather(idx_ref, base, n, src_hbm, dst, sem):
    def start(r, _):
        row = idx_ref[base + r]
        pltpu.make_async_copy(src_hbm.at[pl.ds(row, 1), :], dst.at[pl.ds(r, 1), :], sem).start()
        return 0

    lax.fori_loop(0, n, start, 0, unroll=8)


def _wait_row_gather(n, src_hbm, dst, sem):
    pltpu.make_async_copy(src_hbm.at[pl.ds(0, n), :], dst.at[pl.ds(0, n), :], sem).wait()


MOE_CHUNK = 512
MOE_VISIT_CHUNKS = 5
MOE_TF = 512
SWIGLU_VMEM_LIMIT = 62 * 1024 * 1024
DENSE_CHUNK = 512
DENSE_VISIT_CHUNKS = 4
DENSE_TF = 512
WEIGHT_SPLIT = 1


def _moe_kernel(vis_e_ref, vis_ok_ref, vis_row_ref, vis_nch_ref, tail_ref, row_tok_ref,
                x_hbm, gain_ref, *refs, ch, dense):
    del vis_e_ref, vis_ok_ref
    ns = WEIGHT_SPLIT
    wg_refs, wu_refs, wo_refs = refs[:ns], refs[ns:2 * ns], refs[2 * ns:3 * ns]
    y_hbm, h_ref, acc_ref, gsem, osem = refs[3 * ns:]
    v = pl.program_id(0)
    f = pl.program_id(1)
    nv = pl.num_programs(0)
    nf = pl.num_programs(1)
    nch = vis_nch_ref[v]
    row0 = vis_row_ref[v]
    active = nch > 0
    kq = h_ref.shape[1] // ns

    def rows(c):
        return pl.ds(pl.multiple_of(c * ch, ch), ch)

    def step(h):
        parts = [h[:, q * kq:(q + 1) * kq] for q in range(ns)]
        g = sum(_dot(hq, w[...].astype(BF16)) for hq, w in zip(parts, wg_refs))
        u = sum(_dot(hq, w[...].astype(BF16)) for hq, w in zip(parts, wu_refs))
        a = (g * _sigmoid(g) * u).astype(BF16)
        return jnp.concatenate([_dot(a, w[...].astype(BF16)) for w in wo_refs], axis=1)

    def out_copy(c):
        dst = y_hbm.at[pl.ds(pl.multiple_of(row0 + c * ch, ch), ch), :]
        return pltpu.make_async_copy(acc_ref.at[rows(c), :], dst, osem)

    @pl.when(jnp.logical_and(active, f == 0))
    def _():
        def fetch(c, _):
            dst = acc_ref.at[rows(c), :]
            if dense:
                src = x_hbm.at[pl.ds(pl.multiple_of(row0 + c * ch, ch), ch), :]
                pltpu.make_async_copy(src, dst, gsem).start()
            else:
                _start_row_gather(row_tok_ref, row0 + c * ch, ch, x_hbm, dst, gsem)
            return 0

        def arrived(c, _):
            _wait_row_gather(ch, x_hbm, acc_ref.at[rows(c), :], gsem)
            return 0

        lax.fori_loop(0, nch, fetch, 0)
        lax.fori_loop(0, nch, arrived, 0)

        def body(c, _):
            x = acc_ref[rows(c), :]
            h = _rms(x, gain_ref[...]).astype(BF16)
            h_ref[rows(c), :] = h
            acc_ref[rows(c), :] = x + step(h) if dense else step(h)
            return 0

        lax.fori_loop(0, nch, body, 0)

    @pl.when(jnp.logical_and(active, jnp.logical_and(f > 0, f < nf - 1)))
    def _():
        def body(c, _):
            acc_ref[rows(c), :] += step(h_ref[rows(c), :])
            return 0

        lax.fori_loop(0, nch, body, 0)

    @pl.when(jnp.logical_and(active, f == nf - 1))
    def _():
        def body(c, _):
            acc_ref[rows(c), :] += step(h_ref[rows(c), :])
            out_copy(c).start()
            return 0

        def drain(c, _):
            out_copy(c).wait()
            return 0

        lax.fori_loop(0, nch, body, 0)
        lax.fori_loop(0, nch, drain, 0)

    @pl.when(jnp.logical_and(v == nv - 1, f == nf - 1))
    def _():
        first = tail_ref[0]
        n_tail = y_hbm.shape[0] // ch - first
        acc_ref[rows(0), :] = jnp.zeros((ch, acc_ref.shape[1]), acc_ref.dtype)

        def tail_copy(c):
            dst = y_hbm.at[pl.ds(pl.multiple_of((first + c) * ch, ch), ch), :]
            return pltpu.make_async_copy(acc_ref.at[rows(0), :], dst, osem)

        def fill(c, _):
            tail_copy(c).start()
            return 0

        def drain(c, _):
            tail_copy(c).wait()
            return 0

        lax.fori_loop(0, n_tail, fill, 0)
        lax.fori_loop(0, n_tail, drain, 0)


def _dense_visits(t, ch, max_chunks):
    span = ch * max_chunks
    row0 = np.arange(0, t, span, dtype=np.int32)
    n_ch = np.minimum(max_chunks, (t - row0) // ch).astype(np.int32)
    zeros = np.zeros_like(row0)
    return tuple(jnp.asarray(a) for a in (zeros, zeros + 1, row0, n_ch,
                                          np.array([t // ch], np.int32), np.zeros((1,), np.int32)))


def _swiglu_rows(x, gain, tables, w_in, w_out, *, dense, ch, max_chunks, tf, n_visits=None):
    vis_e, vis_ok, vis_row, vis_nch, tail, row_tok = tables
    t, d = x.shape
    ff = w_out.shape[1]
    nf = ff // tf
    p = t if dense else row_tok.shape[0]
    ns = WEIGHT_SPLIT

    def wspec(shape, index):
        def index_map(v, f, e, ok, *_):
            return index(e[v], jnp.where(ok[v] > 0, f, nf - 1))
        return pl.BlockSpec(shape, index_map)

    wg_specs = [wspec((None, d // ns, tf), functools.partial(lambda e, f, q: (e, q, f), q=q))
                for q in range(ns)]
    wu_specs = [wspec((None, d // ns, tf), functools.partial(lambda e, f, q: (e, q, nf + f), q=q))
                for q in range(ns)]
    wo_specs = [wspec((None, tf, d // ns), functools.partial(lambda e, f, q: (e, f, q), q=q))
                for q in range(ns)]
    grid_spec = pltpu.PrefetchScalarGridSpec(
        num_scalar_prefetch=6,
        grid=(vis_e.shape[0] if n_visits is None else n_visits, nf),
        in_specs=[pl.BlockSpec(memory_space=pl.ANY),
                  pl.BlockSpec((1, d), lambda v, f, *_: (0, 0))] + wg_specs + wu_specs + wo_specs,
        out_specs=pl.BlockSpec(memory_space=pl.ANY),
        scratch_shapes=[pltpu.VMEM((max_chunks * ch, d), BF16),
                        pltpu.VMEM((max_chunks * ch, d), F32),
                        pltpu.SemaphoreType.DMA, pltpu.SemaphoreType.DMA],
    )
    return pl.pallas_call(
        functools.partial(_moe_kernel, ch=ch, dense=dense),
        out_shape=jax.ShapeDtypeStruct((p, d), F32),
        grid_spec=grid_spec,
        compiler_params=_cparams(("arbitrary", "arbitrary"), vmem=SWIGLU_VMEM_LIMIT),
        name="dense_ffn" if dense else "moe_experts",
    )(vis_e, vis_ok, vis_row, vis_nch, tail, row_tok, x, gain.reshape(1, d),
      *([w_in] * (2 * ns)), *([w_out] * ns))


def _combine_kernel(pos_ref, x_ref, gate_ref, y_hbm, gain_ref, o_ref, buf, sem, *, tm):
    i = pl.program_id(0)
    n = pl.num_programs(0)
    t = pos_ref.shape[0] // TOP_K
    slot = lax.rem(i, 2)

    def start(step, sl):
        for k in range(TOP_K):
            _start_row_gather(pos_ref, k * t + step * tm, tm, y_hbm, buf.at[sl, k], sem.at[sl, k])

    @pl.when(i == 0)
    def _():
        start(0, 0)

    @pl.when(i + 1 < n)
    def _():
        start(i + 1, 1 - slot)

    for k in range(TOP_K):
        _wait_row_gather(tm, y_hbm, buf.at[slot, k], sem.at[slot, k])
    g = gate_ref[...]
    x = x_ref[...] + (g[:, 0:1] * buf[slot, 0] + g[:, 1:2] * buf[slot, 1])
    if gain_ref is None:
        o_ref[...] = x
    else:
        o_ref[...] = _rms(x, gain_ref[...])


def _moe_combine(x, gate, y_rows, pos, final_gain, *, tm=256):
    t, d = x.shape
    in_specs = [pl.BlockSpec((tm, d), lambda i, ps: (i, 0)),
                pl.BlockSpec((tm, LANES), lambda i, ps: (i, 0)),
                pl.BlockSpec(memory_space=pl.ANY)]
    args = [x, gate, y_rows]
    if final_gain is not None:
        in_specs.append(pl.BlockSpec((1, d), lambda i, ps: (0, 0)))
        args.append(final_gain.reshape(1, d))
        body = functools.partial(_combine_kernel, tm=tm)
    else:
        def body(pos_ref, x_ref, gate_ref, y_hbm, o_ref, *scratch):
            _combine_kernel(pos_ref, x_ref, gate_ref, y_hbm, None, o_ref, *scratch, tm=tm)
    grid_spec = pltpu.PrefetchScalarGridSpec(
        num_scalar_prefetch=1,
        grid=(t // tm,),
        in_specs=in_specs,
        out_specs=pl.BlockSpec((tm, d), lambda i, ps: (i, 0)),
        scratch_shapes=[pltpu.VMEM((2, TOP_K, tm, d), F32), pltpu.SemaphoreType.DMA((2, TOP_K))],
    )
    return pl.pallas_call(
        body,
        out_shape=jax.ShapeDtypeStruct((t, d), F32),
        grid_spec=grid_spec,
        compiler_params=_cparams(("arbitrary",)),
        name="moe_combine",
    )(pos, *args)


def _final_norm_kernel(x_ref, g_ref, o_ref):
    o_ref[...] = _rms(x_ref[...], g_ref[...])


def _final_norm(x, gain, *, tm=512):
    t, d = x.shape
    return pl.pallas_call(
        _final_norm_kernel,
        out_shape=jax.ShapeDtypeStruct((t, d), F32),
        grid=(t // tm,),
        in_specs=[pl.BlockSpec((tm, d), lambda i: (i, 0)), pl.BlockSpec((1, d), lambda i: (0, 0))],
        out_specs=pl.BlockSpec((tm, d), lambda i: (i, 0)),
        compiler_params=_cparams(("parallel",)),
        name="final_norm",
    )(x, gain.reshape(1, d))


def _routing_tables(idx, ch, max_chunks):
    t = idx.shape[0]
    e_flat = idx[:, :TOP_K].T.reshape(-1)
    tok = jnp.tile(jnp.arange(t, dtype=jnp.int32), TOP_K)
    experts = jnp.arange(N_EXPERTS, dtype=jnp.int32)
    onehot = (e_flat[:, None] == experts[None, :]).astype(jnp.int32)
    rank = jnp.sum((jnp.cumsum(onehot, axis=0) - onehot) * onehot, axis=1)
    counts = jnp.sum(onehot, axis=0)
    n_chunk = (counts + ch - 1) // ch
    ends = jnp.cumsum(n_chunk * ch)
    starts = ends - n_chunk * ch
    dest = (starts[e_flat] + rank).astype(jnp.int32)
    p = TOP_K * t + N_EXPERTS * ch
    row_tok = jnp.zeros((p,), jnp.int32).at[dest].set(tok)

    n_vis = (n_chunk + max_chunks - 1) // max_chunks
    per_vis = (n_chunk + jnp.maximum(n_vis, 1) - 1) // jnp.maximum(n_vis, 1)
    v_end = jnp.cumsum(n_vis)
    v_start = v_end - n_vis
    nv_max = (p // ch + (max_chunks - 1) * N_EXPERTS) // max_chunks
    slot = jnp.arange(nv_max, dtype=jnp.int32)
    ok = slot < v_end[-1]
    e_of = jnp.minimum(jnp.sum((slot[:, None] >= v_end[None, :]).astype(jnp.int32), axis=1),
                       N_EXPERTS - 1)
    e_of = jnp.where(ok, e_of, e_of[v_end[-1] - 1])
    k = slot - v_start[e_of]
    n_ch = jnp.where(ok, jnp.clip(n_chunk[e_of] - k * per_vis[e_of], 0, per_vis[e_of]), 0)
    row0 = jnp.where(ok, starts[e_of] + k * per_vis[e_of] * ch, 0)
    tail = (ends[-1:] // ch).astype(jnp.int32)
    tables = (e_of.astype(jnp.int32), ok.astype(jnp.int32), row0.astype(jnp.int32),
              n_ch.astype(jnp.int32), tail, row_tok)
    return tables, dest, v_end[-1].astype(jnp.int32)


def _relayout_mla(w_uq, w_ukv):
    dq = NOPE_DIM + ROPE_DIM_B
    wq = w_uq.reshape(Q_LORA, N_HEADS, dq)
    wq = jnp.concatenate([wq, jnp.zeros((Q_LORA, N_HEADS, MLA_QK - dq), w_uq.dtype)], axis=2)
    wkv = w_ukv.reshape(KV_LORA, N_HEADS, NOPE_DIM + HEAD_DIM)
    wk = wkv[:, :, :NOPE_DIM].reshape(KV_LORA, N_HEADS * NOPE_DIM)
    wv = wkv[:, :, NOPE_DIM:].reshape(KV_LORA, N_HEADS * HEAD_DIM)
    return wq.reshape(Q_LORA, N_HEADS * MLA_QK).astype(BF16), wk.astype(BF16), wv.astype(BF16)


def kernel(x, norm_mix, w_in, dq_norm, dkv_norm, w_uq, w_ukv, lam_q1, lam_k1, lam_q2, lam_k2,
           diff_norm, w_branch, w_out, norm_ffn, w_dense_in, w_dense_out, w_router,
           w_moe_in, w_moe_out, norm_final):
    batch, seq, d = x.shape
    depth = w_in.shape[0]
    t = batch * seq
    xt = x.reshape(t, d)

    rot_a = DIFF_DIM // 4
    tab_a = _rope_tables(seq, ((0, rot_a), (DIFF_DIM, rot_a)))
    tab_b = _rope_tables(seq, ((0, ROPE_DIM_B),))
    tab_c = _rope_tables(seq, ((0, HEAD_DIM // 4),))
    w_in_t = jnp.swapaxes(w_in, 1, 2)

    for layer in range(depth):
        proj = _inproj(_rms_norm_bf16(xt, norm_mix[layer]), w_in_t, layer)

        lam_init = 0.8 - 0.6 * math.exp(-0.3 * layer)
        lam_rows = jnp.stack([lam_q1[layer], lam_k1[layer], lam_q2[layer], lam_k2[layer]])
        y_a = _diff_attention(proj, lam_rows, diff_norm[layer], tab_a,
                              batch=batch, seq=seq, lam_init=lam_init)

        wuq, wuk, wuv = _relayout_mla(w_uq[layer], w_ukv[layer])
        q_b, k_b, v_b = _mla_prep(proj, dq_norm[layer], dkv_norm[layer], wuq, wuk, wuv, tab_b, seq=seq)
        y_b = _mla_attention(q_b, k_b, v_b, batch=batch, seq=seq)

        y_c = _dilated_attention(proj, tab_c, batch=batch, seq=seq)
        y_d = _stick_attention(proj, batch=batch, seq=seq)

        merged = _merge((y_a, y_b, y_c, y_d), proj, w_branch[layer].astype(BF16))
        xt = _outproj(xt, merged, w_out[layer].astype(BF16))

        last = layer == depth - 1
        if layer % 2 == 0:
            m = layer // 2
            xt = _swiglu_rows(xt, norm_ffn[layer], _dense_visits(t, DENSE_CHUNK, DENSE_VISIT_CHUNKS),
                              w_dense_in[m:m + 1], w_dense_out[m:m + 1], dense=True,
                              ch=DENSE_CHUNK, max_chunks=DENSE_VISIT_CHUNKS, tf=DENSE_TF)
            if last:
                xt = _final_norm(xt, norm_final)
        else:
            m = layer // 2
            idx, gate = _router(xt, norm_ffn[layer], w_router[m])
            tables, dest, n_vis = _routing_tables(idx, MOE_CHUNK, MOE_VISIT_CHUNKS)
            y_rows = _swiglu_rows(xt, norm_ffn[layer], tables, w_moe_in[m], w_moe_out[m], dense=False,
                                  ch=MOE_CHUNK, max_chunks=MOE_VISIT_CHUNKS, tf=MOE_TF, n_visits=n_vis)
            xt = _moe_combine(xt, gate, y_rows, dest, norm_final if last else None)
    return xt.reshape(batch, seq, d)
```

```python
import functools
import math

import numpy as np
import jax
import jax.numpy as jnp
from jax import lax
from jax.experimental import pallas as pl
from jax.experimental.pallas import tpu as pltpu

F32 = jnp.float32
BF16 = jnp.bfloat16

HEAD_DIM = 128
ROPE_THETA = 500000.0
NORM_EPS = 1e-6
N_BRANCH = 4
BRANCH_WIDTH = 512
N_HEADS = 4
DIFF_DIM = 64
Q_LORA = 512
KV_LORA = 512
NOPE_DIM = 128
ROPE_DIM_B = 64
MLA_QK = 256
DIL_PAIRS = ((128, 1), (512, 4), (2048, 16))
N_DIL_GROUPS = 3
N_EXPERTS = 8
TOP_K = 2
LANES = 128
LOG2_E = math.log2(math.e)

A_OFF = 0
B_OFF = 1536
C_OFF = 3072
D_OFF = 7680
QKV_COLS = 9216

VMEM_LIMIT = 56 * 1024 * 1024


def _cparams(sem, vmem=VMEM_LIMIT):
    return pltpu.CompilerParams(dimension_semantics=sem, vmem_limit_bytes=vmem)


def _dot(a, b):
    return jnp.dot(a, b, preferred_element_type=F32)


def _dot_nt(a, b):
    return lax.dot_general(a, b, (((1,), (1,)), ((), ())), preferred_element_type=F32)


def _rms(x, gain):
    return x * lax.rsqrt(jnp.mean(x * x, axis=-1, keepdims=True) + NORM_EPS) * gain


def _sigmoid(x):
    return 1.0 / (1.0 + jnp.exp(-x))


def _rope_tables(seq, segments):
    pos = np.arange(seq, dtype=np.float64)
    c = np.ones((seq, LANES), np.float64)
    s = np.zeros((seq, LANES), np.float64)
    r = np.zeros((LANES, LANES), np.float32)
    for start, rot in segments:
        half = rot // 2
        inv_freq = ROPE_THETA ** (-np.arange(0, rot, 2, dtype=np.float64) / rot)
        ang = pos[:, None] * inv_freq[None, :]
        cos, sin = np.cos(ang), np.sin(ang)
        c[:, start:start + half] = cos
        c[:, start + half:start + rot] = cos
        s[:, start:start + half] = -sin
        s[:, start + half:start + rot] = sin
        for i in range(half):
            r[start + half + i, start + i] = 1.0
            r[start + i, start + half + i] = 1.0
    return jnp.asarray(c, F32), jnp.asarray(s, F32), jnp.asarray(r, BF16)


def _rope(x_bf16, rot, c, s):
    return x_bf16.astype(F32) * c + _dot(x_bf16, rot) * s


def _norm_kernel(x_ref, g_ref, o_ref):
    o_ref[...] = _rms(x_ref[...], g_ref[...]).astype(o_ref.dtype)


def _rms_norm_bf16(x, gain, *, tm=512):
    t, d = x.shape
    return pl.pallas_call(
        _norm_kernel,
        out_shape=jax.ShapeDtypeStruct((t, d), BF16),
        grid=(t // tm,),
        in_specs=[pl.BlockSpec((tm, d), lambda i: (i, 0)), pl.BlockSpec((1, d), lambda i: (0, 0))],
        out_specs=pl.BlockSpec((tm, d), lambda i: (i, 0)),
        compiler_params=_cparams(("parallel",)),
        name="mix_norm",
    )(x, gain.reshape(1, d))


B_END = B_OFF + Q_LORA + KV_LORA + ROPE_DIM_B


INPROJ_ROW_SPLIT = 8


def _inproj_kernel(h_ref, *refs, gate_block):
    w_refs, o_ref = refs[:-1], refs[-1]
    j = pl.program_id(1)
    rows = h_ref.shape[0] // INPROJ_ROW_SPLIT
    for q, w_ref in enumerate(w_refs):
        n = w_ref.shape[1]
        w = w_ref[0].astype(BF16)
        for r in range(INPROJ_ROW_SPLIT):
            acc = _dot_nt(h_ref[r * rows:(r + 1) * rows, :], w)
            o_ref[r * rows:(r + 1) * rows, q * n:(q + 1) * n] = jnp.where(
                j >= gate_block, _sigmoid(acc), acc).astype(o_ref.dtype)


def _inproj(h, w_t, layer, *, tm=2048, tn=1024):
    t, d = h.shape
    n_out = w_t.shape[1] + C_OFF - B_END
    assert C_OFF % tn == 0 and n_out % tn == 0
    ns = WEIGHT_SPLIT

    unit = math.gcd(tn, C_OFF - B_END)

    def first_col(j, q):
        k = j * (tn // unit) + q * (tn // ns // unit)
        return unit * jnp.where(j * tn < C_OFF, k, k - (C_OFF - B_END) // unit)

    return pl.pallas_call(
        functools.partial(_inproj_kernel, gate_block=QKV_COLS // tn),
        out_shape=jax.ShapeDtypeStruct((t, n_out), BF16),
        grid=(t // tm, n_out // tn),
        in_specs=[pl.BlockSpec((tm, d), lambda i, j: (i, 0))]
                 + [pl.BlockSpec((pl.Element(1), pl.Element(tn // ns), pl.Element(d)),
                                 functools.partial(
                                     lambda i, j, q: (layer, first_col(j, q), 0), q=q))
                    for q in range(ns)],
        out_specs=pl.BlockSpec((tm, tn), lambda i, j: (i, j)),
        compiler_params=_cparams(("parallel", "arbitrary")),
        name="inproj",
    )(h, *([w_t] * ns))


def _osm(scores, values, carries):
    stats = []
    for s, (m, l, _) in zip(scores, carries):
        m_new = jnp.maximum(m, jnp.max(s, axis=-1, keepdims=True))
        alpha = jnp.exp2(m - m_new)
        p = jnp.exp2(s - m_new)
        stats.append((m_new, alpha * l + jnp.sum(p, axis=-1, keepdims=True), alpha, p))
    return tuple((m_new, l, alpha * acc + _dot(p.astype(BF16), v))
                 for (m_new, l, alpha, p), v, (_, _, acc) in zip(stats, values, carries))


def _causal_keep(t):
    r = lax.broadcasted_iota(jnp.int32, (t, t), 0)
    c = lax.broadcasted_iota(jnp.int32, (t, t), 1)
    return c <= r


def _diff_kernel(lam_ref, gain_ref, rot_ref, cq_ref, sq_ref, ck_ref, sk_ref,
                 q_ref, k_ref, v_ref, o_ref, kr_ref, *, tq, hp, lam_init):
    qi = pl.program_id(2)
    rot = rot_ref[...]
    heads = [slice(h * HEAD_DIM, (h + 1) * HEAD_DIM) for h in range(hp)]

    @pl.when(qi == 0)
    def _():
        for hs in heads:
            kr_ref[:, hs] = _rope(k_ref[:, hs], rot, ck_ref[...], sk_ref[...]).astype(BF16)

    scale2 = LOG2_E / math.sqrt(DIFF_DIM)
    lane = lax.broadcasted_iota(jnp.int32, (tq, HEAD_DIM), 1)
    chains = []
    for hs in heads:
        qf = _rope(q_ref[:, hs], rot, cq_ref[...], sq_ref[...])
        chains.append((jnp.where(lane < DIFF_DIM, qf, 0.0).astype(BF16), hs))
        chains.append((jnp.where(lane >= DIFF_DIM, qf, 0.0).astype(BF16), hs))

    def step(j, carry, masked):
        off = pl.multiple_of(j * tq, tq)
        scores = [_dot_nt(q, kr_ref[pl.ds(off, tq), hs]) * scale2 for q, hs in chains]
        if masked:
            keep = _causal_keep(tq)
            scores = [jnp.where(keep, s, -jnp.inf) for s in scores]
        return _osm(scores, [v_ref[pl.ds(off, tq), hs] for _, hs in chains], carry)

    init = (jnp.full((tq, 1), -jnp.inf, F32), jnp.zeros((tq, 1), F32),
            jnp.zeros((tq, HEAD_DIM), F32))
    carry = lax.fori_loop(0, qi, lambda j, c: step(j, c, False), tuple(init for _ in chains))
    carry = step(qi, carry, True)

    lam_rows = lam_ref[...]
    lam = (jnp.exp(jnp.sum(lam_rows[0:1] * lam_rows[1:2], axis=-1, keepdims=True))
           - jnp.exp(jnp.sum(lam_rows[2:3] * lam_rows[3:4], axis=-1, keepdims=True))
           + lam_init)
    for i, hs in enumerate(heads):
        (_, l1, a1), (_, l2, a2) = carry[2 * i], carry[2 * i + 1]
        out = a1 / l1 - lam * (a2 / l2)
        o_ref[:, hs] = (_rms(out, gain_ref[...]) * (1.0 - lam_init)).astype(o_ref.dtype)


def _diff_attention(proj, lam_rows, gain, tables, *, batch, seq, lam_init, tq=512, hp=4):
    c, s, rot = tables
    nq = seq // tq
    w = hp * HEAD_DIM
    cb = A_OFF // w
    ng = N_HEADS // hp
    return pl.pallas_call(
        functools.partial(_diff_kernel, tq=tq, hp=hp, lam_init=lam_init),
        out_shape=jax.ShapeDtypeStruct((batch * seq, BRANCH_WIDTH), BF16),
        grid=(batch, ng, nq),
        in_specs=[pl.BlockSpec((4, DIFF_DIM), lambda b, h, i: (0, 0)),
                  pl.BlockSpec((1, HEAD_DIM), lambda b, h, i: (0, 0)),
                  pl.BlockSpec((LANES, LANES), lambda b, h, i: (0, 0)),
                  pl.BlockSpec((tq, LANES), lambda b, h, i: (i, 0)),
                  pl.BlockSpec((tq, LANES), lambda b, h, i: (i, 0)),
                  pl.BlockSpec((seq, LANES), lambda b, h, i: (0, 0)),
                  pl.BlockSpec((seq, LANES), lambda b, h, i: (0, 0)),
                  pl.BlockSpec((tq, w), lambda b, h, i: (b * nq + i, cb + h)),
                  pl.BlockSpec((seq, w), lambda b, h, i: (b, cb + ng + h)),
                  pl.BlockSpec((seq, w), lambda b, h, i: (b, cb + 2 * ng + h))],
        out_specs=pl.BlockSpec((tq, w), lambda b, h, i: (b * nq + i, h)),
        scratch_shapes=[pltpu.VMEM((seq, w), BF16)],
        compiler_params=_cparams(("parallel", "parallel", "arbitrary")),
        name="diff_attention",
    )(lam_rows, gain.reshape(1, HEAD_DIM), rot, c, s, c, s, proj, proj, proj)


def _mla_prep_kernel(cq_ref, ckv_ref, kr_ref, dqn_ref, dkvn_ref, wuq_ref, wuk_ref, wuv_ref,
                     rot_ref, c_ref, s_ref, q_out, k_out, v_out):
    rot = rot_ref[...]
    c = c_ref[...]
    s = s_ref[...]
    hq = _rms(cq_ref[...].astype(F32), dqn_ref[...]).astype(BF16)
    hkv = _rms(ckv_ref[...].astype(F32), dkvn_ref[...]).astype(BF16)
    q = _dot(hq, wuq_ref[...])
    kn = _dot(hkv, wuk_ref[...])
    v_out[...] = _dot(hkv, wuv_ref[...]).astype(BF16)
    kr_lane = lax.broadcasted_iota(jnp.int32, kr_ref.shape, 1)
    kr = jnp.where(kr_lane < ROPE_DIM_B, kr_ref[...], jnp.zeros_like(kr_ref[...]))
    k_rope = _rope(kr, rot, c, s).astype(BF16)
    for h in range(N_HEADS):
        lo = h * MLA_QK
        q_out[:, lo:lo + NOPE_DIM] = q[:, lo:lo + NOPE_DIM].astype(BF16)
        q_out[:, lo + NOPE_DIM:lo + MLA_QK] = _rope(
            q[:, lo + NOPE_DIM:lo + MLA_QK].astype(BF16), rot, c, s).astype(BF16)
        k_out[:, lo:lo + NOPE_DIM] = kn[:, h * NOPE_DIM:(h + 1) * NOPE_DIM].astype(BF16)
        k_out[:, lo + NOPE_DIM:lo + MLA_QK] = k_rope


def _mla_prep(proj, dq_norm, dkv_norm, wuq, wuk, wuv, tables, *, seq, tm=512):
    c, s, rot = tables
    t = proj.shape[0]
    nb = seq // tm
    cb = B_OFF // Q_LORA
    full = lambda shape: pl.BlockSpec(shape, lambda i: (0, 0))
    return pl.pallas_call(
        _mla_prep_kernel,
        out_shape=(jax.ShapeDtypeStruct((t, N_HEADS * MLA_QK), BF16),
                   jax.ShapeDtypeStruct((t, N_HEADS * MLA_QK), BF16),
                   jax.ShapeDtypeStruct((t, N_HEADS * HEAD_DIM), BF16)),
        grid=(t // tm,),
        in_specs=[pl.BlockSpec((tm, Q_LORA), lambda i: (i, cb)),
                  pl.BlockSpec((tm, KV_LORA), lambda i: (i, cb + 1)),
                  pl.BlockSpec((tm, LANES), lambda i: (i, (B_OFF + Q_LORA + KV_LORA) // LANES)),
                  full((1, Q_LORA)), full((1, KV_LORA)),
                  full(wuq.shape), full(wuk.shape), full(wuv.shape),
                  full((LANES, LANES)),
                  pl.BlockSpec((tm, LANES), lambda i: (i % nb, 0)),
                  pl.BlockSpec((tm, LANES), lambda i: (i % nb, 0))],
        out_specs=(pl.BlockSpec((tm, N_HEADS * MLA_QK), lambda i: (i, 0)),
                   pl.BlockSpec((tm, N_HEADS * MLA_QK), lambda i: (i, 0)),
                   pl.BlockSpec((tm, N_HEADS * HEAD_DIM), lambda i: (i, 0))),
        compiler_params=_cparams(("parallel",)),
        name="mla_prep",
    )(proj, proj, proj, dq_norm.reshape(1, Q_LORA), dkv_norm.reshape(1, KV_LORA),
      wuq, wuk, wuv, rot, c, s)


def _flash_kernel(q_ref, k_ref, v_ref, o_ref, *, tq, hp, dqk, scale):
    qi = pl.program_id(2)
    qk = [slice(h * dqk, (h + 1) * dqk) for h in range(hp)]
    hv = [slice(h * HEAD_DIM, (h + 1) * HEAD_DIM) for h in range(hp)]
    qs = [q_ref[:, sl] for sl in qk]

    def step(j, carry, masked):
        off = pl.multiple_of(j * tq, tq)
        scores = [_dot_nt(q, k_ref[pl.ds(off, tq), ks]) * (scale * LOG2_E) for q, ks in zip(qs, qk)]
        if masked:
            keep = _causal_keep(tq)
            scores = [jnp.where(keep, s, -jnp.inf) for s in scores]
        return _osm(scores, [v_ref[pl.ds(off, tq), vs] for vs in hv], carry)

    init = (jnp.full((tq, 1), -jnp.inf, F32), jnp.zeros((tq, 1), F32),
            jnp.zeros((tq, HEAD_DIM), F32))
    carry = lax.fori_loop(0, qi, lambda j, c: step(j, c, False), tuple(init for _ in qs))
    carry = step(qi, carry, True)
    for vs, (_, l, acc) in zip(hv, carry):
        o_ref[:, vs] = (acc / l).astype(o_ref.dtype)


def _mla_attention(q, k, v, *, batch, seq, tq=512, hp=4):
    nq = seq // tq
    ng = N_HEADS // hp
    return pl.pallas_call(
        functools.partial(_flash_kernel, tq=tq, hp=hp, dqk=MLA_QK,
                          scale=1.0 / math.sqrt(NOPE_DIM + ROPE_DIM_B)),
        out_shape=jax.ShapeDtypeStruct((batch * seq, BRANCH_WIDTH), BF16),
        grid=(batch, ng, nq),
        in_specs=[pl.BlockSpec((tq, hp * MLA_QK), lambda b, h, i: (b * nq + i, h)),
                  pl.BlockSpec((seq, hp * MLA_QK), lambda b, h, i: (b, h)),
                  pl.BlockSpec((seq, hp * HEAD_DIM), lambda b, h, i: (b, h))],
        out_specs=pl.BlockSpec((tq, hp * HEAD_DIM), lambda b, h, i: (b * nq + i, h)),
        compiler_params=_cparams(("parallel", "parallel", "arbitrary")),
        name="mla_attention",
    )(q, k, v)


DIL_BLOCK = 128


def _dilated_kernel(rot_ref, c_ref, s_ref, *refs, seq):
    in_refs = refs[:9]
    o_ref = refs[9]
    qf, kf, vf, og, lse = refs[10:]
    rot = rot_ref[...]
    c = c_ref[...]
    s = s_ref[...]
    for g in range(N_DIL_GROUPS):
        qf[g] = _rope(in_refs[g][...], rot, c, s)
        kf[g] = _rope(in_refs[3 + g][...], rot, c, s)
        vf[g] = in_refs[6 + g][...].astype(F32)
    scale = 1.0 / math.sqrt(HEAD_DIM)
    blk = DIL_BLOCK

    def rows(start, size, stride):
        return pl.ds(start, size) if stride == 1 else pl.ds(start, size, stride=stride)

    for g, (window, dil) in enumerate(DIL_PAIRS):
        assert window == blk * dil
        span = blk * dil
        n_sub = seq // span
        nk = 2 * blk if n_sub > 1 else blk
        assert n_sub & (n_sub - 1) == 0
        q_rows, k_rows = [], []
        for r in range(dil):
            for cb in range(n_sub):
                q_rows.append(rows(r + cb * span, blk, dil))
                k_rows.append(rows(r + max(cb - 1, 0) * span, nk, dil))
        nb = len(q_rows)
        q = jnp.stack([qf[g, qr, :] for qr in q_rows]).astype(BF16)
        k = jnp.stack([kf[g, kr, :] for kr in k_rows]).astype(BF16)
        v = jnp.stack([vf[g, kr, :] for kr in k_rows]).astype(BF16)
        sc = jnp.einsum("bqd,bkd->bqk", q, k, preferred_element_type=F32) * (scale * LOG2_E)
        first = (lax.broadcasted_iota(jnp.int32, (nb, blk, nk), 0) & (n_sub - 1)) == 0
        dist = (lax.broadcasted_iota(jnp.int32, (nb, blk, nk), 1)
                - lax.broadcasted_iota(jnp.int32, (nb, blk, nk), 2)
                + jnp.where(first, 0, nk - blk))
        sc = jnp.where(dist >= 0, jnp.where(dist <= blk, sc, -jnp.inf), -jnp.inf)
        m = jnp.max(sc, axis=-1, keepdims=True)
        e = jnp.exp2(sc - m)
        den = jnp.sum(e, axis=-1, keepdims=True)
        o = jnp.einsum("bqk,bkd->bqd", e.astype(BF16), v, preferred_element_type=F32) / den
        lg = jnp.broadcast_to(m + jnp.log2(den), (nb, blk, HEAD_DIM))
        for i, qr in enumerate(q_rows):
            og[g, qr, :] = o[i]
            lse[g, qr, :] = lg[i]

    l0, l1, l2 = lse[0], lse[1], lse[2]
    mx = jnp.maximum(jnp.maximum(l0, l1), l2)
    w0, w1, w2 = jnp.exp2(l0 - mx), jnp.exp2(l1 - mx), jnp.exp2(l2 - mx)
    o_ref[...] = ((w0 * og[0] + w1 * og[1] + w2 * og[2]) / (w0 + w1 + w2)).astype(o_ref.dtype)


def _dilated_attention(proj, tables, *, batch, seq):
    c, s, rot = tables
    cb = C_OFF // HEAD_DIM
    nh = N_DIL_GROUPS * N_HEADS

    def col(kind, g):
        return lambda b, h: (b, cb + kind * nh + g * N_HEADS + h)

    in_specs = [pl.BlockSpec((LANES, LANES), lambda b, h: (0, 0)),
                pl.BlockSpec((seq, LANES), lambda b, h: (0, 0)),
                pl.BlockSpec((seq, LANES), lambda b, h: (0, 0))]
    for kind in range(3):
        for g in range(N_DIL_GROUPS):
            in_specs.append(pl.BlockSpec((seq, HEAD_DIM), col(kind, g)))
    return pl.pallas_call(
        functools.partial(_dilated_kernel, seq=seq),
        out_shape=jax.ShapeDtypeStruct((batch * seq, BRANCH_WIDTH), BF16),
        grid=(batch, N_HEADS),
        in_specs=in_specs,
        out_specs=pl.BlockSpec((seq, HEAD_DIM), lambda b, h: (b, h)),
        scratch_shapes=[pltpu.VMEM((N_DIL_GROUPS, seq, HEAD_DIM), F32) for _ in range(5)],
        compiler_params=_cparams(("parallel", "parallel")),
        name="dilated_attention",
    )(rot, c, s, *([proj] * 9))


def _stick_kernel(q_ref, k_ref, v_ref, o_ref, *, tq, hp, scale):
    qi = pl.program_id(2)
    heads = [slice(h * HEAD_DIM, (h + 1) * HEAD_DIM) for h in range(hp)]
    qs = [q_ref[:, hs] for hs in heads]
    r = lax.broadcasted_iota(jnp.int32, (tq, tq), 0)
    c = lax.broadcasted_iota(jnp.int32, (tq, tq), 1)
    later_keys = jnp.where(r > c, 1.0, 0.0).astype(BF16)

    def step(j, carry, diag):
        off = pl.multiple_of(j * tq, tq)
        strict = c < r
        z2 = [_dot_nt(q, k_ref[pl.ds(off, tq), hs]) * (scale * LOG2_E) for q, hs in zip(qs, heads)]
        sp2 = [jnp.maximum(z, 0.0) + jnp.log2(1.0 + jnp.exp2(-jnp.abs(z))) for z in z2]
        log_not = [jnp.where(strict, -sp, 0.0) if diag else -sp for sp in sp2]
        later = []
        for ln, (tail, _) in zip(log_not, carry):
            hi = ln.astype(BF16)
            lo = (ln - hi.astype(F32)).astype(BF16)
            later.append(_dot(hi, later_keys) + _dot(lo, later_keys) + tail)
        a = [jnp.exp2((z - sp) + lt) for z, sp, lt in zip(z2, sp2, later)]
        if diag:
            a = [jnp.where(strict, x, 0.0) for x in a]
        out = []
        for x, ln, hs, (tail, acc) in zip(a, log_not, heads, carry):
            acc = acc + _dot(x.astype(BF16), v_ref[pl.ds(off, tq), hs])
            out.append((tail + jnp.sum(ln, axis=-1, keepdims=True), acc))
        return tuple(out)

    init = (jnp.zeros((tq, 1), F32), jnp.zeros((tq, HEAD_DIM), F32))
    carry = step(qi, tuple(init for _ in heads), True)
    carry = lax.fori_loop(0, qi, lambda t, cr: step(qi - 1 - t, cr, False), carry)
    for hs, (_, acc) in zip(heads, carry):
        o_ref[:, hs] = acc.astype(o_ref.dtype)


def _stick_attention(proj, *, batch, seq, tq=256, hp=4):
    nq = seq // tq
    w = hp * HEAD_DIM
    cb = D_OFF // w
    ng = N_HEADS // hp
    return pl.pallas_call(
        functools.partial(_stick_kernel, tq=tq, hp=hp, scale=1.0 / math.sqrt(HEAD_DIM)),
        out_shape=jax.ShapeDtypeStruct((batch * seq, BRANCH_WIDTH), BF16),
        grid=(batch, ng, nq),
        in_specs=[pl.BlockSpec((tq, w), lambda b, h, i: (b * nq + i, cb + h)),
                  pl.BlockSpec((seq, w), lambda b, h, i: (b, cb + ng + h)),
                  pl.BlockSpec((seq, w), lambda b, h, i: (b, cb + 2 * ng + h))],
        out_specs=pl.BlockSpec((tq, w), lambda b, h, i: (b * nq + i, h)),
        compiler_params=_cparams(("parallel", "parallel", "arbitrary")),
        name="stick_attention",
    )(proj, proj, proj)


def _merge_kernel(ya, yb, yc, yd, g0, g1, g2, g3, wb_ref, o_ref):
    acc = None
    for i, (y, g) in enumerate(((ya, g0), (yb, g1), (yc, g2), (yd, g3))):
        t = g[...].astype(F32) * _dot(y[...], wb_ref[i])
        acc = t if acc is None else acc + t
    o_ref[...] = acc.astype(o_ref.dtype)


def _merge(ys, proj, wb, *, tm=1024, tn=1024):
    t = proj.shape[0]
    d = wb.shape[2]
    nn = d // tn
    g0 = QKV_COLS // tn
    y_spec = pl.BlockSpec((tm, BRANCH_WIDTH), lambda i, j: (i, 0))
    g_specs = [pl.BlockSpec((tm, tn), functools.partial(lambda i, j, b: (i, g0 + b * nn + j), b=b))
               for b in range(N_BRANCH)]
    return pl.pallas_call(
        _merge_kernel,
        out_shape=jax.ShapeDtypeStruct((t, d), BF16),
        grid=(t // tm, nn),
        in_specs=[y_spec] * N_BRANCH + g_specs
                 + [pl.BlockSpec((N_BRANCH, BRANCH_WIDTH, tn), lambda i, j: (0, 0, j))],
        out_specs=pl.BlockSpec((tm, tn), lambda i, j: (i, j)),
        compiler_params=_cparams(("parallel", "arbitrary")),
        name="branch_merge",
    )(*ys, proj, proj, proj, proj, wb)


def _outproj_kernel(x_ref, m_ref, w_ref, o_ref):
    o_ref[...] = x_ref[...] + _dot(m_ref[...], w_ref[...])


def _outproj(x, merged, w, *, tm=1024, tn=1024):
    t, d = x.shape
    return pl.pallas_call(
        _outproj_kernel,
        out_shape=jax.ShapeDtypeStruct((t, d), F32),
        grid=(t // tm, d // tn),
        in_specs=[pl.BlockSpec((tm, tn), lambda i, j: (i, j)),
                  pl.BlockSpec((tm, d), lambda i, j: (i, 0)),
                  pl.BlockSpec((d, tn), lambda i, j: (0, j))],
        out_specs=pl.BlockSpec((tm, tn), lambda i, j: (i, j)),
        compiler_params=_cparams(("parallel", "arbitrary")),
        name="out_proj",
    )(x, merged, w)


def _router_kernel(x_ref, gain_ref, wr_ref, idx_ref, gate_ref):
    h = _rms(x_ref[...], gain_ref[...])
    logits = jnp.dot(h, wr_ref[...], preferred_element_type=F32, precision=lax.Precision.HIGHEST)
    lane = lax.broadcasted_iota(jnp.int32, logits.shape, 1)
    lanef = lane.astype(F32)
    lg = jnp.where(lane < N_EXPERTS, logits, -jnp.inf)
    v1 = jnp.max(lg, axis=-1, keepdims=True)
    i1 = jnp.min(jnp.where(lg == v1, lanef, float(LANES)), axis=-1, keepdims=True)
    lg2 = jnp.where(lanef == i1, -jnp.inf, lg)
    v2 = jnp.max(lg2, axis=-1, keepdims=True)
    i2 = jnp.min(jnp.where(lg2 == v2, lanef, float(LANES)), axis=-1, keepdims=True)
    e2 = jnp.exp(v2 - v1)
    g1 = 1.0 / (1.0 + e2)
    g2 = e2 / (1.0 + e2)
    idx_ref[...] = jnp.where(lane == 0, i1, jnp.where(lane == 1, i2, 0.0)).astype(jnp.int32)
    gate_ref[...] = jnp.where(lane == 0, g1, jnp.where(lane == 1, g2, 0.0))


def _router(x, gain, w_router, *, tm=512):
    t, d = x.shape
    wr = jnp.zeros((d, LANES), F32).at[:, :N_EXPERTS].set(w_router)
    return pl.pallas_call(
        _router_kernel,
        out_shape=(jax.ShapeDtypeStruct((t, LANES), jnp.int32),
                   jax.ShapeDtypeStruct((t, LANES), F32)),
        grid=(t // tm,),
        in_specs=[pl.BlockSpec((tm, d), lambda i: (i, 0)),
                  pl.BlockSpec((1, d), lambda i: (0, 0)),
                  pl.BlockSpec((d, LANES), lambda i: (0, 0))],
        out_specs=(pl.BlockSpec((tm, LANES), lambda i: (i, 0)),
                   pl.BlockSpec((tm, LANES), lambda i: (i, 0))),
        compiler_params=_cparams(("parallel",)),
        name="router",
    )(x, gain.reshape(1, d), wr)


def _start_row_gather(idx_ref, base, n, src_hbm, dst, sem):
    def start(r, _):
        row = idx_ref[base + r]
        pltpu.make_async_copy(src_hbm.at[pl.ds(row, 1), :], dst.at[pl.ds(r, 1), :], sem).start()
        return 0

    lax.fori_loop(0, n, start, 0, unroll=8)


def _wait_row_gather(n, src_hbm, dst, sem):
    pltpu.make_async_copy(src_hbm.at[pl.ds(0, n), :], dst.at[pl.ds(0, n), :], sem).wait()


MOE_CHUNK = 512
MOE_VISIT_CHUNKS = 5
MOE_TF = 512
SWIGLU_VMEM_LIMIT = 62 * 1024 * 1024
DENSE_CHUNK = 512
DENSE_VISIT_CHUNKS = 4
DENSE_TF = 512
WEIGHT_SPLIT = 1


def _moe_kernel(vis_e_ref, vis_ok_ref, vis_row_ref, vis_nch_ref, tail_ref, row_tok_ref,
                x_hbm, gain_ref, *refs, ch, dense):
    del vis_e_ref, vis_ok_ref
    ns = WEIGHT_SPLIT
    wg_refs, wu_refs, wo_refs = refs[:ns], refs[ns:2 * ns], refs[2 * ns:3 * ns]
    y_hbm, h_ref, acc_ref, gsem, osem = refs[3 * ns:]
    v = pl.program_id(0)
    f = pl.program_id(1)
    nv = pl.num_programs(0)
    nf = pl.num_programs(1)
    nch = vis_nch_ref[v]
    row0 = vis_row_ref[v]
    active = nch > 0
    kq = h_ref.shape[1] // ns

    def rows(c):
        return pl.ds(pl.multiple_of(c * ch, ch), ch)

    def step(h):
        parts = [h[:, q * kq:(q + 1) * kq] for q in range(ns)]
        g = sum(_dot(hq, w[...].astype(BF16)) for hq, w in zip(parts, wg_refs))
        u = sum(_dot(hq, w[...].astype(BF16)) for hq, w in zip(parts, wu_refs))
        a = (g * _sigmoid(g) * u).astype(BF16)
        return jnp.concatenate([_dot(a, w[...].astype(BF16)) for w in wo_refs], axis=1)

    def out_copy(c):
        dst = y_hbm.at[pl.ds(pl.multiple_of(row0 + c * ch, ch), ch), :]
        return pltpu.make_async_copy(acc_ref.at[rows(c), :], dst, osem)

    @pl.when(jnp.logical_and(active, f == 0))
    def _():
        def fetch(c, _):
            dst = acc_ref.at[rows(c), :]
            if dense:
                src = x_hbm.at[pl.ds(pl.multiple_of(row0 + c * ch, ch), ch), :]
                pltpu.make_async_copy(src, dst, gsem).start()
            else:
                _start_row_gather(row_tok_ref, row0 + c * ch, ch, x_hbm, dst, gsem)
            return 0

        def arrived(c, _):
            _wait_row_gather(ch, x_hbm, acc_ref.at[rows(c), :], gsem)
            return 0

        lax.fori_loop(0, nch, fetch, 0)
        lax.fori_loop(0, nch, arrived, 0)

        def body(c, _):
            x = acc_ref[rows(c), :]
            h = _rms(x, gain_ref[...]).astype(BF16)
            h_ref[rows(c), :] = h
            acc_ref[rows(c), :] = x + step(h) if dense else step(h)
            return 0

        lax.fori_loop(0, nch, body, 0)

    @pl.when(jnp.logical_and(active, jnp.logical_and(f > 0, f < nf - 1)))
    def _():
        def body(c, _):
            acc_ref[rows(c), :] += step(h_ref[rows(c), :])
            return 0

        lax.fori_loop(0, nch, body, 0)

    @pl.when(jnp.logical_and(active, f == nf - 1))
    def _():
        def body(c, _):
            acc_ref[rows(c), :] += step(h_ref[rows(c), :])
            out_copy(c).start()
            return 0

        def drain(c, _):
            out_copy(c).wait()
            return 0

        lax.fori_loop(0, nch, body, 0)
        lax.fori_loop(0, nch, drain, 0)

    @pl.when(jnp.logical_and(v == nv - 1, f == nf - 1))
    def _():
        first = tail_ref[0]
        n_tail = y_hbm.shape[0] // ch - first
        acc_ref[rows(0), :] = jnp.zeros((ch, acc_ref.shape[1]), acc_ref.dtype)

        def tail_copy(c):
            dst = y_hbm.at[pl.ds(pl.multiple_of((first + c) * ch, ch), ch), :]
            return pltpu.make_async_copy(acc_ref.at[rows(0), :], dst, osem)

        def fill(c, _):
            tail_copy(c).start()
            return 0

        def drain(c, _):
            tail_copy(c).wait()
            return 0

        lax.fori_loop(0, n_tail, fill, 0)
        lax.fori_loop(0, n_tail, drain, 0)


def _dense_visits(t, ch, max_chunks):
    span = ch * max_chunks
    row0 = np.arange(0, t, span, dtype=np.int32)
    n_ch = np.minimum(max_chunks, (t - row0) // ch).astype(np.int32)
    zeros = np.zeros_like(row0)
    return tuple(jnp.asarray(a) for a in (zeros, zeros + 1, row0, n_ch,
                                          np.array([t // ch], np.int32), np.zeros((1,), np.int32)))


def _swiglu_rows(x, gain, tables, w_in, w_out, *, dense, ch, max_chunks, tf, n_visits=None):
    vis_e, vis_ok, vis_row, vis_nch, tail, row_tok = tables
    t, d = x.shape
    ff = w_out.shape[1]
    nf = ff // tf
    p = t if dense else row_tok.shape[0]
    ns = WEIGHT_SPLIT

    def wspec(shape, index):
        def index_map(v, f, e, ok, *_):
            return index(e[v], jnp.where(ok[v] > 0, f, nf - 1))
        return pl.BlockSpec(shape, index_map)

    wg_specs = [wspec((None, d // ns, tf), functools.partial(lambda e, f, q: (e, q, f), q=q))
                for q in range(ns)]
    wu_specs = [wspec((None, d // ns, tf), functools.partial(lambda e, f, q: (e, q, nf + f), q=q))
                for q in range(ns)]
    wo_specs = [wspec((None, tf, d // ns), functools.partial(lambda e, f, q: (e, f, q), q=q))
                for q in range(ns)]
    grid_spec = pltpu.PrefetchScalarGridSpec(
        num_scalar_prefetch=6,
        grid=(vis_e.shape[0] if n_visits is None else n_visits, nf),
        in_specs=[pl.BlockSpec(memory_space=pl.ANY),
                  pl.BlockSpec((1, d), lambda v, f, *_: (0, 0))] + wg_specs + wu_specs + wo_specs,
        out_specs=pl.BlockSpec(memory_space=pl.ANY),
        scratch_shapes=[pltpu.VMEM((max_chunks * ch, d), BF16),
                        pltpu.VMEM((max_chunks * ch, d), F32),
                        pltpu.SemaphoreType.DMA, pltpu.SemaphoreType.DMA],
    )
    return pl.pallas_call(
        functools.partial(_moe_kernel, ch=ch, dense=dense),
        out_shape=jax.ShapeDtypeStruct((p, d), F32),
        grid_spec=grid_spec,
        compiler_params=_cparams(("arbitrary", "arbitrary"), vmem=SWIGLU_VMEM_LIMIT),
        name="dense_ffn" if dense else "moe_experts",
    )(vis_e, vis_ok, vis_row, vis_nch, tail, row_tok, x, gain.reshape(1, d),
      *([w_in] * (2 * ns)), *([w_out] * ns))


def _combine_kernel(pos_ref, x_ref, gate_ref, y_hbm, gain_ref, o_ref, buf, sem, *, tm):
    i = pl.program_id(0)
    n = pl.num_programs(0)
    t = pos_ref.shape[0] // TOP_K
    slot = lax.rem(i, 2)

    def start(step, sl):
        for k in range(TOP_K):
            _start_row_gather(pos_ref, k * t + step * tm, tm, y_hbm, buf.at[sl, k], sem.at[sl, k])

    @pl.when(i == 0)
    def _():
        start(0, 0)

    @pl.when(i + 1 < n)
    def _():
        start(i + 1, 1 - slot)

    for k in range(TOP_K):
        _wait_row_gather(tm, y_hbm, buf.at[slot, k], sem.at[slot, k])
    g = gate_ref[...]
    x = x_ref[...] + (g[:, 0:1] * buf[slot, 0] + g[:, 1:2] * buf[slot, 1])
    if gain_ref is None:
        o_ref[...] = x
    else:
        o_ref[...] = _rms(x, gain_ref[...])


def _moe_combine(x, gate, y_rows, pos, final_gain, *, tm=256):
    t, d = x.shape
    in_specs = [pl.BlockSpec((tm, d), lambda i, ps: (i, 0)),
                pl.BlockSpec((tm, LANES), lambda i, ps: (i, 0)),
                pl.BlockSpec(memory_space=pl.ANY)]
    args = [x, gate, y_rows]
    if final_gain is not None:
        in_specs.append(pl.BlockSpec((1, d), lambda i, ps: (0, 0)))
        args.append(final_gain.reshape(1, d))
        body = functools.partial(_combine_kernel, tm=tm)
    else:
        def body(pos_ref, x_ref, gate_ref, y_hbm, o_ref, *scratch):
            _combine_kernel(pos_ref, x_ref, gate_ref, y_hbm, None, o_ref, *scratch, tm=tm)
    grid_spec = pltpu.PrefetchScalarGridSpec(
        num_scalar_prefetch=1,
        grid=(t // tm,),
        in_specs=in_specs,
        out_specs=pl.BlockSpec((tm, d), lambda i, ps: (i, 0)),
        scratch_shapes=[pltpu.VMEM((2, TOP_K, tm, d), F32), pltpu.SemaphoreType.DMA((2, TOP_K))],
    )
    return pl.pallas_call(
        body,
        out_shape=jax.ShapeDtypeStruct((t, d), F32),
        grid_spec=grid_spec,
        compiler_params=_cparams(("arbitrary",)),
        name="moe_combine",
    )(pos, *args)


def _final_norm_kernel(x_ref, g_ref, o_ref):
    o_ref[...] = _rms(x_ref[...], g_ref[...])


def _final_norm(x, gain, *, tm=512):
    t, d = x.shape
    return pl.pallas_call(
        _final_norm_kernel,
        out_shape=jax.ShapeDtypeStruct((t, d), F32),
        grid=(t // tm,),
        in_specs=[pl.BlockSpec((tm, d), lambda i: (i, 0)), pl.BlockSpec((1, d), lambda i: (0, 0))],
        out_specs=pl.BlockSpec((tm, d), lambda i: (i, 0)),
        compiler_params=_cparams(("parallel",)),
        name="final_norm",
    )(x, gain.reshape(1, d))


def _routing_tables(idx, ch, max_chunks):
    t = idx.shape[0]
    e_flat = idx[:, :TOP_K].T.reshape(-1)
    tok = jnp.tile(jnp.arange(t, dtype=jnp.int32), TOP_K)
    experts = jnp.arange(N_EXPERTS, dtype=jnp.int32)
    onehot = (e_flat[:, None] == experts[None, :]).astype(jnp.int32)
    rank = jnp.sum((jnp.cumsum(onehot, axis=0) - onehot) * onehot, axis=1)
    counts = jnp.sum(onehot, axis=0)
    n_chunk = (counts + ch - 1) // ch
    ends = jnp.cumsum(n_chunk * ch)
    starts = ends - n_chunk * ch
    dest = (starts[e_flat] + rank).astype(jnp.int32)
    p = TOP_K * t + N_EXPERTS * ch
    row_tok = jnp.zeros((p,), jnp.int32).at[dest].set(tok)

    n_vis = (n_chunk + max_chunks - 1) // max_chunks
    per_vis = (n_chunk + jnp.maximum(n_vis, 1) - 1) // jnp.maximum(n_vis, 1)
    v_end = jnp.cumsum(n_vis)
    v_start = v_end - n_vis
    nv_max = (p // ch + (max_chunks - 1) * N_EXPERTS) // max_chunks
    slot = jnp.arange(nv_max, dtype=jnp.int32)
    ok = slot < v_end[-1]
    e_of = jnp.minimum(jnp.sum((slot[:, None] >= v_end[None, :]).astype(jnp.int32), axis=1),
                       N_EXPERTS - 1)
    e_of = jnp.where(ok, e_of, e_of[v_end[-1] - 1])
    k = slot - v_start[e_of]
    n_ch = jnp.where(ok, jnp.clip(n_chunk[e_of] - k * per_vis[e_of], 0, per_vis[e_of]), 0)
    row0 = jnp.where(ok, starts[e_of] + k * per_vis[e_of] * ch, 0)
    tail = (ends[-1:] // ch).astype(jnp.int32)
    tables = (e_of.astype(jnp.int32), ok.astype(jnp.int32), row0.astype(jnp.int32),
              n_ch.astype(jnp.int32), tail, row_tok)
    return tables, dest, v_end[-1].astype(jnp.int32)


def _relayout_mla(w_uq, w_ukv):
    dq = NOPE_DIM + ROPE_DIM_B
    wq = w_uq.reshape(Q_LORA, N_HEADS, dq)
    wq = jnp.concatenate([wq, jnp.zeros((Q_LORA, N_HEADS, MLA_QK - dq), w_uq.dtype)], axis=2)
    wkv = w_ukv.reshape(KV_LORA, N_HEADS, NOPE_DIM + HEAD_DIM)
    wk = wkv[:, :, :NOPE_DIM].reshape(KV_LORA, N_HEADS * NOPE_DIM)
    wv = wkv[:, :, NOPE_DIM:].reshape(KV_LORA, N_HEADS * HEAD_DIM)
    return wq.reshape(Q_LORA, N_HEADS * MLA_QK).astype(BF16), wk.astype(BF16), wv.astype(BF16)


def kernel(x, norm_mix, w_in, dq_norm, dkv_norm, w_uq, w_ukv, lam_q1, lam_k1, lam_q2, lam_k2,
           diff_norm, w_branch, w_out, norm_ffn, w_dense_in, w_dense_out, w_router,
           w_moe_in, w_moe_out, norm_final):
    batch, seq, d = x.shape
    depth = w_in.shape[0]
    t = batch * seq
    xt = x.reshape(t, d)

    rot_a = DIFF_DIM // 4
    tab_a = _rope_tables(seq, ((0, rot_a), (DIFF_DIM, rot_a)))
    tab_b = _rope_tables(seq, ((0, ROPE_DIM_B),))
    tab_c = _rope_tables(seq, ((0, HEAD_DIM // 4),))
    w_in_t = jnp.swapaxes(w_in, 1, 2)

    for layer in range(depth):
        proj = _inproj(_rms_norm_bf16(xt, norm_mix[layer]), w_in_t, layer)

        lam_init = 0.8 - 0.6 * math.exp(-0.3 * layer)
        lam_rows = jnp.stack([lam_q1[layer], lam_k1[layer], lam_q2[layer], lam_k2[layer]])
        y_a = _diff_attention(proj, lam_rows, diff_norm[layer], tab_a,
                              batch=batch, seq=seq, lam_init=lam_init)

        wuq, wuk, wuv = _relayout_mla(w_uq[layer], w_ukv[layer])
        q_b, k_b, v_b = _mla_prep(proj, dq_norm[layer], dkv_norm[layer], wuq, wuk, wuv, tab_b, seq=seq)
        y_b = _mla_attention(q_b, k_b, v_b, batch=batch, seq=seq)

        y_c = _dilated_attention(proj, tab_c, batch=batch, seq=seq)
        y_d = _stick_attention(proj, batch=batch, seq=seq)

        merged = _merge((y_a, y_b, y_c, y_d), proj, w_branch[layer].astype(BF16))
        xt = _outproj(xt, merged, w_out[layer].astype(BF16))

        last = layer == depth - 1
        if layer % 2 == 0:
            m = layer // 2
            xt = _swiglu_rows(xt, norm_ffn[layer], _dense_visits(t, DENSE_CHUNK, DENSE_VISIT_CHUNKS),
                              w_dense_in[m:m + 1], w_dense_out[m:m + 1], dense=True,
                              ch=DENSE_CHUNK, max_chunks=DENSE_VISIT_CHUNKS, tf=DENSE_TF)
            if last:
                xt = _final_norm(xt, norm_final)
        else:
            m = layer // 2
            idx, gate = _router(xt, norm_ffn[layer], w_router[m])
            tables, dest, n_vis = _routing_tables(idx, MOE_CHUNK, MOE_VISIT_CHUNKS)
            y_rows = _swiglu_rows(xt, norm_ffn[layer], tables, w_moe_in[m], w_moe_out[m], dense=False,
                                  ch=MOE_CHUNK, max_chunks=MOE_VISIT_CHUNKS, tf=MOE_TF, n_visits=n_vis)
            xt = _moe_combine(xt, gate, y_rows, dest, norm_final if last else None)
    return xt.reshape(batch, seq, d)
```

```python
import functools
import math

import numpy as np
import jax
import jax.numpy as jnp
from jax import lax
from jax.experimental import pallas as pl
from jax.experimental.pallas import tpu as pltpu

F32 = jnp.float32
BF16 = jnp.bfloat16

HEAD_DIM = 128
ROPE_THETA = 500000.0
NORM_EPS = 1e-6
N_BRANCH = 4
BRANCH_WIDTH = 512
N_HEADS = 4
DIFF_DIM = 64
Q_LORA = 512
KV_LORA = 512
NOPE_DIM = 128
ROPE_DIM_B = 64
MLA_QK = 256
DIL_PAIRS = ((128, 1), (512, 4), (2048, 16))
N_DIL_GROUPS = 3
N_EXPERTS = 8
TOP_K = 2
LANES = 128
LOG2_E = math.log2(math.e)

A_OFF = 0
B_OFF = 1536
C_OFF = 3072
D_OFF = 7680
QKV_COLS = 9216

VMEM_LIMIT = 56 * 1024 * 1024


def _cparams(sem, vmem=VMEM_LIMIT):
    return pltpu.CompilerParams(dimension_semantics=sem, vmem_limit_bytes=vmem)


def _dot(a, b):
    return jnp.dot(a, b, preferred_element_type=F32)


def _dot_nt(a, b):
    return lax.dot_general(a, b, (((1,), (1,)), ((), ())), preferred_element_type=F32)


def _rms(x, gain):
    return x * lax.rsqrt(jnp.mean(x * x, axis=-1, keepdims=True) + NORM_EPS) * gain


def _sigmoid(x):
    return 1.0 / (1.0 + jnp.exp(-x))


def _rope_tables(seq, segments):
    pos = np.arange(seq, dtype=np.float64)
    c = np.ones((seq, LANES), np.float64)
    s = np.zeros((seq, LANES), np.float64)
    r = np.zeros((LANES, LANES), np.float32)
    for start, rot in segments:
        half = rot // 2
        inv_freq = ROPE_THETA ** (-np.arange(0, rot, 2, dtype=np.float64) / rot)
        ang = pos[:, None] * inv_freq[None, :]
        cos, sin = np.cos(ang), np.sin(ang)
        c[:, start:start + half] = cos
        c[:, start + half:start + rot] = cos
        s[:, start:start + half] = -sin
        s[:, start + half:start + rot] = sin
        for i in range(half):
            r[start + half + i, start + i] = 1.0
            r[start + i, start + half + i] = 1.0
    return jnp.asarray(c, F32), jnp.asarray(s, F32), jnp.asarray(r, BF16)


def _rope(x_bf16, rot, c, s):
    return x_bf16.astype(F32) * c + _dot(x_bf16, rot) * s


def _norm_kernel(x_ref, g_ref, o_ref):
    o_ref[...] = _rms(x_ref[...], g_ref[...]).astype(o_ref.dtype)


def _rms_norm_bf16(x, gain, *, tm=512):
    t, d = x.shape
    return pl.pallas_call(
        _norm_kernel,
        out_shape=jax.ShapeDtypeStruct((t, d), BF16),
        grid=(t // tm,),
        in_specs=[pl.BlockSpec((tm, d), lambda i: (i, 0)), pl.BlockSpec((1, d), lambda i: (0, 0))],
        out_specs=pl.BlockSpec((tm, d), lambda i: (i, 0)),
        compiler_params=_cparams(("parallel",)),
        name="mix_norm",
    )(x, gain.reshape(1, d))


B_END = B_OFF + Q_LORA + KV_LORA + ROPE_DIM_B


INPROJ_ROW_SPLIT = 8


def _inproj_kernel(h_ref, *refs, gate_block):
    w_refs, o_ref = refs[:-1], refs[-1]
    j = pl.program_id(1)
    rows = h_ref.shape[0] // INPROJ_ROW_SPLIT
    for q, w_ref in enumerate(w_refs):
        n = w_ref.shape[1]
        w = w_ref[0].astype(BF16)
        for r in range(INPROJ_ROW_SPLIT):
            acc = _dot_nt(h_ref[r * rows:(r + 1) * rows, :], w)
            o_ref[r * rows:(r + 1) * rows, q * n:(q + 1) * n] = jnp.where(
                j >= gate_block, _sigmoid(acc), acc).astype(o_ref.dtype)


def _inproj(h, w_t, layer, *, tm=2048, tn=1024):
    t, d = h.shape
    n_out = w_t.shape[1] + C_OFF - B_END
    assert C_OFF % tn == 0 and n_out % tn == 0
    ns = WEIGHT_SPLIT

    unit = math.gcd(tn, C_OFF - B_END)

    def first_col(j, q):
        k = j * (tn // unit) + q * (tn // ns // unit)
        return unit * jnp.where(j * tn < C_OFF, k, k - (C_OFF - B_END) // unit)

    return pl.pallas_call(
        functools.partial(_inproj_kernel, gate_block=QKV_COLS // tn),
        out_shape=jax.ShapeDtypeStruct((t, n_out), BF16),
        grid=(t // tm, n_out // tn),
        in_specs=[pl.BlockSpec((tm, d), lambda i, j: (i, 0))]
                 + [pl.BlockSpec((pl.Element(1), pl.Element(tn // ns), pl.Element(d)),
                                 functools.partial(
                                     lambda i, j, q: (layer, first_col(j, q), 0), q=q))
                    for q in range(ns)],
        out_specs=pl.BlockSpec((tm, tn), lambda i, j: (i, j)),
        compiler_params=_cparams(("parallel", "arbitrary")),
        name="inproj",
    )(h, *([w_t] * ns))


def _osm(scores, values, carries):
    stats = []
    for s, (m, l, _) in zip(scores, carries):
        m_new = jnp.maximum(m, jnp.max(s, axis=-1, keepdims=True))
        alpha = jnp.exp2(m - m_new)
        p = jnp.exp2(s - m_new)
        stats.append((m_new, alpha * l + jnp.sum(p, axis=-1, keepdims=True), alpha, p))
    return tuple((m_new, l, alpha * acc + _dot(p.astype(BF16), v))
                 for (m_new, l, alpha, p), v, (_, _, acc) in zip(stats, values, carries))


def _causal_keep(t):
    r = lax.broadcasted_iota(jnp.int32, (t, t), 0)
    c = lax.broadcasted_iota(jnp.int32, (t, t), 1)
    return c <= r


def _diff_kernel(lam_ref, gain_ref, rot_ref, cq_ref, sq_ref, ck_ref, sk_ref,
                 q_ref, k_ref, v_ref, o_ref, kr_ref, *, tq, hp, lam_init):
    qi = pl.program_id(2)
    rot = rot_ref[...]
    heads = [slice(h * HEAD_DIM, (h + 1) * HEAD_DIM) for h in range(hp)]

    @pl.when(qi == 0)
    def _():
        for hs in heads:
            kr_ref[:, hs] = _rope(k_ref[:, hs], rot, ck_ref[...], sk_ref[...]).astype(BF16)

    scale2 = LOG2_E / math.sqrt(DIFF_DIM)
    lane = lax.broadcasted_iota(jnp.int32, (tq, HEAD_DIM), 1)
    chains = []
    for hs in heads:
        qf = _rope(q_ref[:, hs], rot, cq_ref[...], sq_ref[...])
        chains.append((jnp.where(lane < DIFF_DIM, qf, 0.0).astype(BF16), hs))
        chains.append((jnp.where(lane >= DIFF_DIM, qf, 0.0).astype(BF16), hs))

    def step(j, carry, masked):
        off = pl.multiple_of(j * tq, tq)
        scores = [_dot_nt(q, kr_ref[pl.ds(off, tq), hs]) * scale2 for q, hs in chains]
        if masked:
            keep = _causal_keep(tq)
            scores = [jnp.where(keep, s, -jnp.inf) for s in scores]
        return _osm(scores, [v_ref[pl.ds(off, tq), hs] for _, hs in chains], carry)

    init = (jnp.full((tq, 1), -jnp.inf, F32), jnp.zeros((tq, 1), F32),
            jnp.zeros((tq, HEAD_DIM), F32))
    carry = lax.fori_loop(0, qi, lambda j, c: step(j, c, False), tuple(init for _ in chains))
    carry = step(qi, carry, True)

    lam_rows = lam_ref[...]
    lam = (jnp.exp(jnp.sum(lam_rows[0:1] * lam_rows[1:2], axis=-1, keepdims=True))
           - jnp.exp(jnp.sum(lam_rows[2:3] * lam_rows[3:4], axis=-1, keepdims=True))
           + lam_init)
    for i, hs in enumerate(heads):
        (_, l1, a1), (_, l2, a2) = carry[2 * i], carry[2 * i + 1]
        out = a1 / l1 - lam * (a2 / l2)
        o_ref[:, hs] = (_rms(out, gain_ref[...]) * (1.0 - lam_init)).astype(o_ref.dtype)


def _diff_attention(proj, lam_rows, gain, tables, *, batch, seq, lam_init, tq=512, hp=4):
    c, s, rot = tables
    nq = seq // tq
    w = hp * HEAD_DIM
    cb = A_OFF // w
    ng = N_HEADS // hp
    return pl.pallas_call(
        functools.partial(_diff_kernel, tq=tq, hp=hp, lam_init=lam_init),
        out_shape=jax.ShapeDtypeStruct((batch * seq, BRANCH_WIDTH), BF16),
        grid=(batch, ng, nq),
        in_specs=[pl.BlockSpec((4, DIFF_DIM), lambda b, h, i: (0, 0)),
                  pl.BlockSpec((1, HEAD_DIM), lambda b, h, i: (0, 0)),
                  pl.BlockSpec((LANES, LANES), lambda b, h, i: (0, 0)),
                  pl.BlockSpec((tq, LANES), lambda b, h, i: (i, 0)),
                  pl.BlockSpec((tq, LANES), lambda b, h, i: (i, 0)),
                  pl.BlockSpec((seq, LANES), lambda b, h, i: (0, 0)),
                  pl.BlockSpec((seq, LANES), lambda b, h, i: (0, 0)),
                  pl.BlockSpec((tq, w), lambda b, h, i: (b * nq + i, cb + h)),
                  pl.BlockSpec((seq, w), lambda b, h, i: (b, cb + ng + h)),
                  pl.BlockSpec((seq, w), lambda b, h, i: (b, cb + 2 * ng + h))],
        out_specs=pl.BlockSpec((tq, w), lambda b, h, i: (b * nq + i, h)),
        scratch_shapes=[pltpu.VMEM((seq, w), BF16)],
        compiler_params=_cparams(("parallel", "parallel", "arbitrary")),
        name="diff_attention",
    )(lam_rows, gain.reshape(1, HEAD_DIM), rot, c, s, c, s, proj, proj, proj)


def _mla_prep_kernel(cq_ref, ckv_ref, kr_ref, dqn_ref, dkvn_ref, wuq_ref, wuk_ref, wuv_ref,
                     rot_ref, c_ref, s_ref, q_out, k_out, v_out):
    rot = rot_ref[...]
    c = c_ref[...]
    s = s_ref[...]
    hq = _rms(cq_ref[...].astype(F32), dqn_ref[...]).astype(BF16)
    hkv = _rms(ckv_ref[...].astype(F32), dkvn_ref[...]).astype(BF16)
    q = _dot(hq, wuq_ref[...])
    kn = _dot(hkv, wuk_ref[...])
    v_out[...] = _dot(hkv, wuv_ref[...]).astype(BF16)
    kr_lane = lax.broadcasted_iota(jnp.int32, kr_ref.shape, 1)
    kr = jnp.where(kr_lane < ROPE_DIM_B, kr_ref[...], jnp.zeros_like(kr_ref[...]))
    k_rope = _rope(kr, rot, c, s).astype(BF16)
    for h in range(N_HEADS):
        lo = h * MLA_QK
        q_out[:, lo:lo + NOPE_DIM] = q[:, lo:lo + NOPE_DIM].astype(BF16)
        q_out[:, lo + NOPE_DIM:lo + MLA_QK] = _rope(
            q[:, lo + NOPE_DIM:lo + MLA_QK].astype(BF16), rot, c, s).astype(BF16)
        k_out[:, lo:lo + NOPE_DIM] = kn[:, h * NOPE_DIM:(h + 1) * NOPE_DIM].astype(BF16)
        k_out[:, lo + NOPE_DIM:lo + MLA_QK] = k_rope


def _mla_prep(proj, dq_norm, dkv_norm, wuq, wuk, wuv, tables, *, seq, tm=512):
    c, s, rot = tables
    t = proj.shape[0]
    nb = seq // tm
    cb = B_OFF // Q_LORA
    full = lambda shape: pl.BlockSpec(shape, lambda i: (0, 0))
    return pl.pallas_call(
        _mla_prep_kernel,
        out_shape=(jax.ShapeDtypeStruct((t, N_HEADS * MLA_QK), BF16),
                   jax.ShapeDtypeStruct((t, N_HEADS * MLA_QK), BF16),
                   jax.ShapeDtypeStruct((t, N_HEADS * HEAD_DIM), BF16)),
        grid=(t // tm,),
        in_specs=[pl.BlockSpec((tm, Q_LORA), lambda i: (i, cb)),
                  pl.BlockSpec((tm, KV_LORA), lambda i: (i, cb + 1)),
                  pl.BlockSpec((tm, LANES), lambda i: (i, (B_OFF + Q_LORA + KV_LORA) // LANES)),
                  full((1, Q_LORA)), full((1, KV_LORA)),
                  full(wuq.shape), full(wuk.shape), full(wuv.shape),
                  full((LANES, LANES)),
                  pl.BlockSpec((tm, LANES), lambda i: (i % nb, 0)),
                  pl.BlockSpec((tm, LANES), lambda i: (i % nb, 0))],
        out_specs=(pl.BlockSpec((tm, N_HEADS * MLA_QK), lambda i: (i, 0)),
                   pl.BlockSpec((tm, N_HEADS * MLA_QK), lambda i: (i, 0)),
                   pl.BlockSpec((tm, N_HEADS * HEAD_DIM), lambda i: (i, 0))),
        compiler_params=_cparams(("parallel",)),
        name="mla_prep",
    )(proj, proj, proj, dq_norm.reshape(1, Q_LORA), dkv_norm.reshape(1, KV_LORA),
      wuq, wuk, wuv, rot, c, s)


def _flash_kernel(q_ref, k_ref, v_ref, o_ref, *, tq, hp, dqk, scale):
    qi = pl.program_id(2)
    qk = [slice(h * dqk, (h + 1) * dqk) for h in range(hp)]
    hv = [slice(h * HEAD_DIM, (h + 1) * HEAD_DIM) for h in range(hp)]
    qs = [q_ref[:, sl] for sl in qk]

    def step(j, carry, masked):
        off = pl.multiple_of(j * tq, tq)
        scores = [_dot_nt(q, k_ref[pl.ds(off, tq), ks]) * (scale * LOG2_E) for q, ks in zip(qs, qk)]
        if masked:
            keep = _causal_keep(tq)
            scores = [jnp.where(keep, s, -jnp.inf) for s in scores]
        return _osm(scores, [v_ref[pl.ds(off, tq), vs] for vs in hv], carry)

    init = (jnp.full((tq, 1), -jnp.inf, F32), jnp.zeros((tq, 1), F32),
            jnp.zeros((tq, HEAD_DIM), F32))
    carry = lax.fori_loop(0, qi, lambda j, c: step(j, c, False), tuple(init for _ in qs))
    carry = step(qi, carry, True)
    for vs, (_, l, acc) in zip(hv, carry):
        o_ref[:, vs] = (acc / l).astype(o_ref.dtype)


def _mla_attention(q, k, v, *, batch, seq, tq=512, hp=4):
    nq = seq // tq
    ng = N_HEADS // hp
    return pl.pallas_call(
        functools.partial(_flash_kernel, tq=tq, hp=hp, dqk=MLA_QK,
                          scale=1.0 / math.sqrt(NOPE_DIM + ROPE_DIM_B)),
        out_shape=jax.ShapeDtypeStruct((batch * seq, BRANCH_WIDTH), BF16),
        grid=(batch, ng, nq),
        in_specs=[pl.BlockSpec((tq, hp * MLA_QK), lambda b, h, i: (b * nq + i, h)),
                  pl.BlockSpec((seq, hp * MLA_QK), lambda b, h, i: (b, h)),
                  pl.BlockSpec((seq, hp * HEAD_DIM), lambda b, h, i: (b, h))],
        out_specs=pl.BlockSpec((tq, hp * HEAD_DIM), lambda b, h, i: (b * nq + i, h)),
        compiler_params=_cparams(("parallel", "parallel", "arbitrary")),
        name="mla_attention",
    )(q, k, v)


DIL_BLOCK = 128


def _dilated_kernel(rot_ref, c_ref, s_ref, *refs, seq):
    in_refs = refs[:9]
    o_ref = refs[9]
    qf, kf, vf, og, lse = refs[10:]
    rot = rot_ref[...]
    c = c_ref[...]
    s = s_ref[...]
    for g in range(N_DIL_GROUPS):
        qf[g] = _rope(in_refs[g][...], rot, c, s)
        kf[g] = _rope(in_refs[3 + g][...], rot, c, s)
        vf[g] = in_refs[6 + g][...].astype(F32)
    scale = 1.0 / math.sqrt(HEAD_DIM)
    blk = DIL_BLOCK

    def rows(start, size, stride):
        return pl.ds(start, size) if stride == 1 else pl.ds(start, size, stride=stride)

    for g, (window, dil) in enumerate(DIL_PAIRS):
        assert window == blk * dil
        span = blk * dil
        n_sub = seq // span
        nk = 2 * blk if n_sub > 1 else blk
        assert n_sub & (n_sub - 1) == 0
        q_rows, k_rows = [], []
        for r in range(dil):
            for cb in range(n_sub):
                q_rows.append(rows(r + cb * span, blk, dil))
                k_rows.append(rows(r + max(cb - 1, 0) * span, nk, dil))
        nb = len(q_rows)
        q = jnp.stack([qf[g, qr, :] for qr in q_rows]).astype(BF16)
        k = jnp.stack([kf[g, kr, :] for kr in k_rows]).astype(BF16)
        v = jnp.stack([vf[g, kr, :] for kr in k_rows]).astype(BF16)
        sc = jnp.einsum("bqd,bkd->bqk", q, k, preferred_element_type=F32) * (scale * LOG2_E)
        first = (lax.broadcasted_iota(jnp.int32, (nb, blk, nk), 0) & (n_sub - 1)) == 0
        dist = (lax.broadcasted_iota(jnp.int32, (nb, blk, nk), 1)
                - lax.broadcasted_iota(jnp.int32, (nb, blk, nk), 2)
                + jnp.where(first, 0, nk - blk))
        sc = jnp.where(dist >= 0, jnp.where(dist <= blk, sc, -jnp.inf), -jnp.inf)
        m = jnp.max(sc, axis=-1, keepdims=True)
        e = jnp.exp2(sc - m)
        den = jnp.sum(e, axis=-1, keepdims=True)
        o = jnp.einsum("bqk,bkd->bqd", e.astype(BF16), v, preferred_element_type=F32) / den
        lg = jnp.broadcast_to(m + jnp.log2(den), (nb, blk, HEAD_DIM))
        for i, qr in enumerate(q_rows):
            og[g, qr, :] = o[i]
            lse[g, qr, :] = lg[i]

    l0, l1, l2 = lse[0], lse[1], lse[2]
    mx = jnp.maximum(jnp.maximum(l0, l1), l2)
    w0, w1, w2 = jnp.exp2(l0 - mx), jnp.exp2(l1 - mx), jnp.exp2(l2 - mx)
    o_ref[...] = ((w0 * og[0] + w1 * og[1] + w2 * og[2]) / (w0 + w1 + w2)).astype(o_ref.dtype)


def _dilated_attention(proj, tables, *, batch, seq):
    c, s, rot = tables
    cb = C_OFF // HEAD_DIM
    nh = N_DIL_GROUPS * N_HEADS

    def col(kind, g):
        return lambda b, h: (b, cb + kind * nh + g * N_HEADS + h)

    in_specs = [pl.BlockSpec((LANES, LANES), lambda b, h: (0, 0)),
                pl.BlockSpec((seq, LANES), lambda b, h: (0, 0)),
                pl.BlockSpec((seq, LANES), lambda b, h: (0, 0))]
    for kind in range(3):
        for g in range(N_DIL_GROUPS):
            in_specs.append(pl.BlockSpec((seq, HEAD_DIM), col(kind, g)))
    return pl.pallas_call(
        functools.partial(_dilated_kernel, seq=seq),
        out_shape=jax.ShapeDtypeStruct((batch * seq, BRANCH_WIDTH), BF16),
        grid=(batch, N_HEADS),
        in_specs=in_specs,
        out_specs=pl.BlockSpec((seq, HEAD_DIM), lambda b, h: (b, h)),
        scratch_shapes=[pltpu.VMEM((N_DIL_GROUPS, seq, HEAD_DIM), F32) for _ in range(5)],
        compiler_params=_cparams(("parallel", "parallel")),
        name="dilated_attention",
    )(rot, c, s, *([proj] * 9))


def _stick_kernel(q_ref, k_ref, v_ref, o_ref, *, tq, hp, scale):
    qi = pl.program_id(2)
    heads = [slice(h * HEAD_DIM, (h + 1) * HEAD_DIM) for h in range(hp)]
    qs = [q_ref[:, hs] for hs in heads]
    r = lax.broadcasted_iota(jnp.int32, (tq, tq), 0)
    c = lax.broadcasted_iota(jnp.int32, (tq, tq), 1)
    later_keys = jnp.where(r > c, 1.0, 0.0).astype(BF16)

    def step(j, carry, diag):
        off = pl.multiple_of(j * tq, tq)
        strict = c < r
        z2 = [_dot_nt(q, k_ref[pl.ds(off, tq), hs]) * (scale * LOG2_E) for q, hs in zip(qs, heads)]
        sp2 = [jnp.maximum(z, 0.0) + jnp.log2(1.0 + jnp.exp2(-jnp.abs(z))) for z in z2]
        log_not = [jnp.where(strict, -sp, 0.0) if diag else -sp for sp in sp2]
        later = []
        for ln, (tail, _) in zip(log_not, carry):
            hi = ln.astype(BF16)
            lo = (ln - hi.astype(F32)).astype(BF16)
            later.append(_dot(hi, later_keys) + _dot(lo, later_keys) + tail)
        a = [jnp.exp2((z - sp) + lt) for z, sp, lt in zip(z2, sp2, later)]
        if diag:
            a = [jnp.where(strict, x, 0.0) for x in a]
        out = []
        for x, ln, hs, (tail, acc) in zip(a, log_not, heads, carry):
            acc = acc + _dot(x.astype(BF16), v_ref[pl.ds(off, tq), hs])
            out.append((tail + jnp.sum(ln, axis=-1, keepdims=True), acc))
        return tuple(out)

    init = (jnp.zeros((tq, 1), F32), jnp.zeros((tq, HEAD_DIM), F32))
    carry = step(qi, tuple(init for _ in heads), True)
    carry = lax.fori_loop(0, qi, lambda t, cr: step(qi - 1 - t, cr, False), carry)
    for hs, (_, acc) in zip(heads, carry):
        o_ref[:, hs] = acc.astype(o_ref.dtype)


def _stick_attention(proj, *, batch, seq, tq=256, hp=4):
    nq = seq // tq
    w = hp * HEAD_DIM
    cb = D_OFF // w
    ng = N_HEADS // hp
    return pl.pallas_call(
        functools.partial(_stick_kernel, tq=tq, hp=hp, scale=1.0 / math.sqrt(HEAD_DIM)),
        out_shape=jax.ShapeDtypeStruct((batch * seq, BRANCH_WIDTH), BF16),
        grid=(batch, ng, nq),
        in_specs=[pl.BlockSpec((tq, w), lambda b, h, i: (b * nq + i, cb + h)),
                  pl.BlockSpec((seq, w), lambda b, h, i: (b, cb + ng + h)),
                  pl.BlockSpec((seq, w), lambda b, h, i: (b, cb + 2 * ng + h))],
        out_specs=pl.BlockSpec((tq, w), lambda b, h, i: (b * nq + i, h)),
        compiler_params=_cparams(("parallel", "parallel", "arbitrary")),
        name="stick_attention",
    )(proj, proj, proj)


def _merge_kernel(ya, yb, yc, yd, g0, g1, g2, g3, wb_ref, o_ref):
    acc = None
    for i, (y, g) in enumerate(((ya, g0), (yb, g1), (yc, g2), (yd, g3))):
        t = g[...].astype(F32) * _dot(y[...], wb_ref[i])
        acc = t if acc is None else acc + t
    o_ref[...] = acc.astype(o_ref.dtype)


def _merge(ys, proj, wb, *, tm=1024, tn=1024):
    t = proj.shape[0]
    d = wb.shape[2]
    nn = d // tn
    g0 = QKV_COLS // tn
    y_spec = pl.BlockSpec((tm, BRANCH_WIDTH), lambda i, j: (i, 0))
    g_specs = [pl.BlockSpec((tm, tn), functools.partial(lambda i, j, b: (i, g0 + b * nn + j), b=b))
               for b in range(N_BRANCH)]
    return pl.pallas_call(
        _merge_kernel,
        out_shape=jax.ShapeDtypeStruct((t, d), BF16),
        grid=(t // tm, nn),
        in_specs=[y_spec] * N_BRANCH + g_specs
                 + [pl.BlockSpec((N_BRANCH, BRANCH_WIDTH, tn), lambda i, j: (0, 0, j))],
        out_specs=pl.BlockSpec((tm, tn), lambda i, j: (i, j)),
        compiler_params=_cparams(("parallel", "arbitrary")),
        name="branch_merge",
    )(*ys, proj, proj, proj, proj, wb)


def _outproj_kernel(x_ref, m_ref, w_ref, o_ref):
    o_ref[...] = x_ref[...] + _dot(m_ref[...], w_ref[...])


def _outproj(x, merged, w, *, tm=1024, tn=1024):
    t, d = x.shape
    return pl.pallas_call(
        _outproj_kernel,
        out_shape=jax.ShapeDtypeStruct((t, d), F32),
        grid=(t // tm, d // tn),
        in_specs=[pl.BlockSpec((tm, tn), lambda i, j: (i, j)),
                  pl.BlockSpec((tm, d), lambda i, j: (i, 0)),
                  pl.BlockSpec((d, tn), lambda i, j: (0, j))],
        out_specs=pl.BlockSpec((tm, tn), lambda i, j: (i, j)),
        compiler_params=_cparams(("parallel", "arbitrary")),
        name="out_proj",
    )(x, merged, w)


def _router_kernel(x_ref, gain_ref, wr_ref, idx_ref, gate_ref):
    h = _rms(x_ref[...], gain_ref[...])
    logits = jnp.dot(h, wr_ref[...], preferred_element_type=F32, precision=lax.Precision.HIGHEST)
    lane = lax.broadcasted_iota(jnp.int32, logits.shape, 1)
    lanef = lane.astype(F32)
    lg = jnp.where(lane < N_EXPERTS, logits, -jnp.inf)
    v1 = jnp.max(lg, axis=-1, keepdims=True)
    i1 = jnp.min(jnp.where(lg == v1, lanef, float(LANES)), axis=-1, keepdims=True)
    lg2 = jnp.where(lanef == i1, -jnp.inf, lg)
    v2 = jnp.max(lg2, axis=-1, keepdims=True)
    i2 = jnp.min(jnp.where(lg2 == v2, lanef, float(LANES)), axis=-1, keepdims=True)
    e2 = jnp.exp(v2 - v1)
    g1 = 1.0 / (1.0 + e2)
    g2 = e2 / (1.0 + e2)
    idx_ref[...] = jnp.where(lane == 0, i1, jnp.where(lane == 1, i2, 0.0)).astype(jnp.int32)
    gate_ref[...] = jnp.where(lane == 0, g1, jnp.where(lane == 1, g2, 0.0))


def _router(x, gain, w_router, *, tm=512):
    t, d = x.shape
    wr = jnp.zeros((d, LANES), F32).at[:, :N_EXPERTS].set(w_router)
    return pl.pallas_call(
        _router_kernel,
        out_shape=(jax.ShapeDtypeStruct((t, LANES), jnp.int32),
                   jax.ShapeDtypeStruct((t, LANES), F32)),
        grid=(t // tm,),
        in_specs=[pl.BlockSpec((tm, d), lambda i: (i, 0)),
                  pl.BlockSpec((1, d), lambda i: (0, 0)),
                  pl.BlockSpec((d, LANES), lambda i: (0, 0))],
        out_specs=(pl.BlockSpec((tm, LANES), lambda i: (i, 0)),
                   pl.BlockSpec((tm, LANES), lambda i: (i, 0))),
        compiler_params=_cparams(("parallel",)),
        name="router",
    )(x, gain.reshape(1, d), wr)


def _start_row_gather(idx_ref, base, n, src_hbm, dst, sem):
    def start(r, _):
        row = idx_ref[base + r]
        pltpu.make_async_copy(src_hbm.at[pl.ds(row, 1), :], dst.at[pl.ds(r, 1), :], sem).start()
        return 0

    lax.fori_loop(0, n, start, 0, unroll=8)


def _wait_row_gather(n, src_hbm, dst, sem):
    pltpu.make_async_copy(src_hbm.at[pl.ds(0, n), :], dst.at[pl.ds(0, n), :], sem).wait()


MOE_CHUNK = 512
MOE_VISIT_CHUNKS = 5
MOE_TF = 512
SWIGLU_VMEM_LIMIT = 62 * 1024 * 1024
DENSE_CHUNK = 512
DENSE_VISIT_CHUNKS = 4
DENSE_TF = 512
WEIGHT_SPLIT = 1


def _moe_kernel(vis_e_ref, vis_ok_ref, vis_row_ref, vis_nch_ref, tail_ref, row_tok_ref,
                x_hbm, gain_ref, *refs, ch, dense):
    del vis_e_ref, vis_ok_ref
    ns = WEIGHT_SPLIT
    wg_refs, wu_refs, wo_refs = refs[:ns], refs[ns:2 * ns], refs[2 * ns:3 * ns]
    y_hbm, h_ref, acc_ref, gsem, osem = refs[3 * ns:]
    v = pl.program_id(0)
    f = pl.program_id(1)
    nv = pl.num_programs(0)
    nf = pl.num_programs(1)
    half = ch // 2
    n_unit = vis_nch_ref[v]
    n_full = n_unit // 2
    odd = n_unit - 2 * n_full
    row0 = vis_row_ref[v]
    active = n_unit > 0
    kq = h_ref.shape[1] // ns

    def pieces(fn):
        def full(c, _):
            fn(pl.multiple_of(c * ch, ch), ch)
            return 0

        lax.fori_loop(0, n_full, full, 0)

        @pl.when(odd == 1)
        def _():
            fn(pl.multiple_of(n_full * ch, half), half)

    def step(h):
        parts = [h[:, q * kq:(q + 1) * kq] for q in range(ns)]
        g = sum(_dot(hq, w[...].astype(BF16)) for hq, w in zip(parts, wg_refs))
        u = sum(_dot(hq, w[...].astype(BF16)) for hq, w in zip(parts, wu_refs))
        a = (g * _sigmoid(g) * u).astype(BF16)
        return jnp.concatenate([_dot(a, w[...].astype(BF16)) for w in wo_refs], axis=1)

    def accumulate(r, n):
        acc_ref[pl.ds(r, n), :] += step(h_ref[pl.ds(r, n), :])

    def out_copy(r, n):
        dst = y_hbm.at[pl.ds(pl.multiple_of(row0 + r, half), n), :]
        return pltpu.make_async_copy(acc_ref.at[pl.ds(r, n), :], dst, osem)

    @pl.when(jnp.logical_and(active, f == 0))
    def _():
        def fetch(r, n):
            dst = acc_ref.at[pl.ds(r, n), :]
            if dense:
                src = x_hbm.at[pl.ds(pl.multiple_of(row0 + r, half), n), :]
                pltpu.make_async_copy(src, dst, gsem).start()
            else:
                _start_row_gather(row_tok_ref, row0 + r, n, x_hbm, dst, gsem)

        def arrived(r, n):
            _wait_row_gather(n, x_hbm, acc_ref.at[pl.ds(r, n), :], gsem)

        def first_sweep(r, n):
            x = acc_ref[pl.ds(r, n), :]
            h = _rms(x, gain_ref[...]).astype(BF16)
            h_ref[pl.ds(r, n), :] = h
            acc_ref[pl.ds(r, n), :] = x + step(h) if dense else step(h)

        pieces(fetch)
        pieces(arrived)
        pieces(first_sweep)

    @pl.when(jnp.logical_and(active, jnp.logical_and(f > 0, f < nf - 1)))
    def _():
        pieces(accumulate)

    @pl.when(jnp.logical_and(active, f == nf - 1))
    def _():
        def last_sweep(r, n):
            accumulate(r, n)
            out_copy(r, n).start()

        pieces(last_sweep)
        pieces(lambda r, n: out_copy(r, n).wait())

    @pl.when(jnp.logical_and(v == nv - 1, f == nf - 1))
    def _():
        first = tail_ref[0]
        n_tail = y_hbm.shape[0] // half - first
        acc_ref[pl.ds(0, half), :] = jnp.zeros((half, acc_ref.shape[1]), acc_ref.dtype)

        def tail_copy(c):
            dst = y_hbm.at[pl.ds(pl.multiple_of((first + c) * half, half), half), :]
            return pltpu.make_async_copy(acc_ref.at[pl.ds(0, half), :], dst, osem)

        def fill(c, _):
            tail_copy(c).start()
            return 0

        def drain(c, _):
            tail_copy(c).wait()
            return 0

        lax.fori_loop(0, n_tail, fill, 0)
        lax.fori_loop(0, n_tail, drain, 0)


def _dense_visits(t, ch, max_chunks):
    span = ch * max_chunks
    half = ch // 2
    row0 = np.arange(0, t, span, dtype=np.int32)
    n_unit = np.minimum(2 * max_chunks, (t - row0) // half).astype(np.int32)
    zeros = np.zeros_like(row0)
    return tuple(jnp.asarray(a) for a in (zeros, zeros + 1, row0, n_unit,
                                          np.array([t // half], np.int32), np.zeros((1,), np.int32)))


def _swiglu_rows(x, gain, tables, w_in, w_out, *, dense, ch, max_chunks, tf, n_visits=None):
    vis_e, vis_ok, vis_row, vis_nch, tail, row_tok = tables
    t, d = x.shape
    ff = w_out.shape[1]
    nf = ff // tf
    p = t if dense else row_tok.shape[0]
    ns = WEIGHT_SPLIT

    def wspec(shape, index):
        def index_map(v, f, e, ok, *_):
            return index(e[v], jnp.where(ok[v] > 0, f, nf - 1))
        return pl.BlockSpec(shape, index_map)

    wg_specs = [wspec((None, d // ns, tf), functools.partial(lambda e, f, q: (e, q, f), q=q))
                for q in range(ns)]
    wu_specs = [wspec((None, d // ns, tf), functools.partial(lambda e, f, q: (e, q, nf + f), q=q))
                for q in range(ns)]
    wo_specs = [wspec((None, tf, d // ns), functools.partial(lambda e, f, q: (e, f, q), q=q))
                for q in range(ns)]
    grid_spec = pltpu.PrefetchScalarGridSpec(
        num_scalar_prefetch=6,
        grid=(vis_e.shape[0] if n_visits is None else n_visits, nf),
        in_specs=[pl.BlockSpec(memory_space=pl.ANY),
                  pl.BlockSpec((1, d), lambda v, f, *_: (0, 0))] + wg_specs + wu_specs + wo_specs,
        out_specs=pl.BlockSpec(memory_space=pl.ANY),
        scratch_shapes=[pltpu.VMEM((max_chunks * ch, d), BF16),
                        pltpu.VMEM((max_chunks * ch, d), F32),
                        pltpu.SemaphoreType.DMA, pltpu.SemaphoreType.DMA],
    )
    return pl.pallas_call(
        functools.partial(_moe_kernel, ch=ch, dense=dense),
        out_shape=jax.ShapeDtypeStruct((p, d), F32),
        grid_spec=grid_spec,
        compiler_params=_cparams(("arbitrary", "arbitrary"), vmem=SWIGLU_VMEM_LIMIT),
        name="dense_ffn" if dense else "moe_experts",
    )(vis_e, vis_ok, vis_row, vis_nch, tail, row_tok, x, gain.reshape(1, d),
      *([w_in] * (2 * ns)), *([w_out] * ns))


def _combine_kernel(pos_ref, x_ref, gate_ref, y_hbm, gain_ref, o_ref, buf, sem, *, tm):
    i = pl.program_id(0)
    n = pl.num_programs(0)
    t = pos_ref.shape[0] // TOP_K
    slot = lax.rem(i, 2)

    def start(step, sl):
        for k in range(TOP_K):
            _start_row_gather(pos_ref, k * t + step * tm, tm, y_hbm, buf.at[sl, k], sem.at[sl, k])

    @pl.when(i == 0)
    def _():
        start(0, 0)

    @pl.when(i + 1 < n)
    def _():
        start(i + 1, 1 - slot)

    for k in range(TOP_K):
        _wait_row_gather(tm, y_hbm, buf.at[slot, k], sem.at[slot, k])
    g = gate_ref[...]
    x = x_ref[...] + (g[:, 0:1] * buf[slot, 0] + g[:, 1:2] * buf[slot, 1])
    if gain_ref is None:
        o_ref[...] = x
    else:
        o_ref[...] = _rms(x, gain_ref[...])


def _moe_combine(x, gate, y_rows, pos, final_gain, *, tm=256):
    t, d = x.shape
    in_specs = [pl.BlockSpec((tm, d), lambda i, ps: (i, 0)),
                pl.BlockSpec((tm, LANES), lambda i, ps: (i, 0)),
                pl.BlockSpec(memory_space=pl.ANY)]
    args = [x, gate, y_rows]
    if final_gain is not None:
        in_specs.append(pl.BlockSpec((1, d), lambda i, ps: (0, 0)))
        args.append(final_gain.reshape(1, d))
        body = functools.partial(_combine_kernel, tm=tm)
    else:
        def body(pos_ref, x_ref, gate_ref, y_hbm, o_ref, *scratch):
            _combine_kernel(pos_ref, x_ref, gate_ref, y_hbm, None, o_ref, *scratch, tm=tm)
    grid_spec = pltpu.PrefetchScalarGridSpec(
        num_scalar_prefetch=1,
        grid=(t // tm,),
        in_specs=in_specs,
        out_specs=pl.BlockSpec((tm, d), lambda i, ps: (i, 0)),
        scratch_shapes=[pltpu.VMEM((2, TOP_K, tm, d), F32), pltpu.SemaphoreType.DMA((2, TOP_K))],
    )
    return pl.pallas_call(
        body,
        out_shape=jax.ShapeDtypeStruct((t, d), F32),
        grid_spec=grid_spec,
        compiler_params=_cparams(("arbitrary",)),
        name="moe_combine",
    )(pos, *args)


def _final_norm_kernel(x_ref, g_ref, o_ref):
    o_ref[...] = _rms(x_ref[...], g_ref[...])


def _final_norm(x, gain, *, tm=512):
    t, d = x.shape
    return pl.pallas_call(
        _final_norm_kernel,
        out_shape=jax.ShapeDtypeStruct((t, d), F32),
        grid=(t // tm,),
        in_specs=[pl.BlockSpec((tm, d), lambda i: (i, 0)), pl.BlockSpec((1, d), lambda i: (0, 0))],
        out_specs=pl.BlockSpec((tm, d), lambda i: (i, 0)),
        compiler_params=_cparams(("parallel",)),
        name="final_norm",
    )(x, gain.reshape(1, d))


def _routing_tables(idx, ch, max_chunks):
    ch, max_chunks = ch // 2, 2 * max_chunks
    t = idx.shape[0]
    e_flat = idx[:, :TOP_K].T.reshape(-1)
    tok = jnp.tile(jnp.arange(t, dtype=jnp.int32), TOP_K)
    experts = jnp.arange(N_EXPERTS, dtype=jnp.int32)
    onehot = (e_flat[:, None] == experts[None, :]).astype(jnp.int32)
    rank = jnp.sum((jnp.cumsum(onehot, axis=0) - onehot) * onehot, axis=1)
    counts = jnp.sum(onehot, axis=0)
    n_chunk = (counts + ch - 1) // ch
    ends = jnp.cumsum(n_chunk * ch)
    starts = ends - n_chunk * ch
    dest = (starts[e_flat] + rank).astype(jnp.int32)
    p = TOP_K * t + N_EXPERTS * ch
    row_tok = jnp.zeros((p,), jnp.int32).at[dest].set(tok)

    n_vis = (n_chunk + max_chunks - 1) // max_chunks
    per_vis = (n_chunk + jnp.maximum(n_vis, 1) - 1) // jnp.maximum(n_vis, 1)
    v_end = jnp.cumsum(n_vis)
    v_start = v_end - n_vis
    nv_max = (p // ch + (max_chunks - 1) * N_EXPERTS) // max_chunks
    slot = jnp.arange(nv_max, dtype=jnp.int32)
    ok = slot < v_end[-1]
    e_of = jnp.minimum(jnp.sum((slot[:, None] >= v_end[None, :]).astype(jnp.int32), axis=1),
                       N_EXPERTS - 1)
    e_of = jnp.where(ok, e_of, e_of[v_end[-1] - 1])
    k = slot - v_start[e_of]
    n_ch = jnp.where(ok, jnp.clip(n_chunk[e_of] - k * per_vis[e_of], 0, per_vis[e_of]), 0)
    row0 = jnp.where(ok, starts[e_of] + k * per_vis[e_of] * ch, 0)
    tail = (ends[-1:] // ch).astype(jnp.int32)
    tables = (e_of.astype(jnp.int32), ok.astype(jnp.int32), row0.astype(jnp.int32),
              n_ch.astype(jnp.int32), tail, row_tok)
    return tables, dest, v_end[-1].astype(jnp.int32)


def _relayout_mla(w_uq, w_ukv):
    dq = NOPE_DIM + ROPE_DIM_B
    wq = w_uq.reshape(Q_LORA, N_HEADS, dq)
    wq = jnp.concatenate([wq, jnp.zeros((Q_LORA, N_HEADS, MLA_QK - dq), w_uq.dtype)], axis=2)
    wkv = w_ukv.reshape(KV_LORA, N_HEADS, NOPE_DIM + HEAD_DIM)
    wk = wkv[:, :, :NOPE_DIM].reshape(KV_LORA, N_HEADS * NOPE_DIM)
    wv = wkv[:, :, NOPE_DIM:].reshape(KV_LORA, N_HEADS * HEAD_DIM)
    return wq.reshape(Q_LORA, N_HEADS * MLA_QK).astype(BF16), wk.astype(BF16), wv.astype(BF16)


def kernel(x, norm_mix, w_in, dq_norm, dkv_norm, w_uq, w_ukv, lam_q1, lam_k1, lam_q2, lam_k2,
           diff_norm, w_branch, w_out, norm_ffn, w_dense_in, w_dense_out, w_router,
           w_moe_in, w_moe_out, norm_final):
    batch, seq, d = x.shape
    depth = w_in.shape[0]
    t = batch * seq
    xt = x.reshape(t, d)

    rot_a = DIFF_DIM // 4
    tab_a = _rope_tables(seq, ((0, rot_a), (DIFF_DIM, rot_a)))
    tab_b = _rope_tables(seq, ((0, ROPE_DIM_B),))
    tab_c = _rope_tables(seq, ((0, HEAD_DIM // 4),))
    w_in_t = jnp.swapaxes(w_in, 1, 2)

    for layer in range(depth):
        proj = _inproj(_rms_norm_bf16(xt, norm_mix[layer]), w_in_t, layer)

        lam_init = 0.8 - 0.6 * math.exp(-0.3 * layer)
        lam_rows = jnp.stack([lam_q1[layer], lam_k1[layer], lam_q2[layer], lam_k2[layer]])
        y_a = _diff_attention(proj, lam_rows, diff_norm[layer], tab_a,
                              batch=batch, seq=seq, lam_init=lam_init)

        wuq, wuk, wuv = _relayout_mla(w_uq[layer], w_ukv[layer])
        q_b, k_b, v_b = _mla_prep(proj, dq_norm[layer], dkv_norm[layer], wuq, wuk, wuv, tab_b, seq=seq)
        y_b = _mla_attention(q_b, k_b, v_b, batch=batch, seq=seq)

        y_c = _dilated_attention(proj, tab_c, batch=batch, seq=seq)
        y_d = _stick_attention(proj, batch=batch, seq=seq)

        merged = _merge((y_a, y_b, y_c, y_d), proj, w_branch[layer].astype(BF16))
        xt = _outproj(xt, merged, w_out[layer].astype(BF16))

        last = layer == depth - 1
        if layer % 2 == 0:
            m = layer // 2
            xt = _swiglu_rows(xt, norm_ffn[layer], _dense_visits(t, DENSE_CHUNK, DENSE_VISIT_CHUNKS),
                              w_dense_in[m:m + 1], w_dense_out[m:m + 1], dense=True,
                              ch=DENSE_CHUNK, max_chunks=DENSE_VISIT_CHUNKS, tf=DENSE_TF)
            if last:
                xt = _final_norm(xt, norm_final)
        else:
            m = layer // 2
            idx, gate = _router(xt, norm_ffn[layer], w_router[m])
            tables, dest, n_vis = _routing_tables(idx, MOE_CHUNK, MOE_VISIT_CHUNKS)
            y_rows = _swiglu_rows(xt, norm_ffn[layer], tables, w_moe_in[m], w_moe_out[m], dense=False,
                                  ch=MOE_CHUNK, max_chunks=MOE_VISIT_CHUNKS, tf=MOE_TF, n_visits=n_vis)
            xt = _moe_combine(xt, gate, y_rows, dest, norm_final if last else None)
    return xt.reshape(batch, seq, d)
```

```python
import functools
import math

import numpy as np
import jax
import jax.numpy as jnp
from jax import lax
from jax.experimental import pallas as pl
from jax.experimental.pallas import tpu as pltpu

F32 = jnp.float32
BF16 = jnp.bfloat16

HEAD_DIM = 128
ROPE_THETA = 500000.0
NORM_EPS = 1e-6
N_BRANCH = 4
BRANCH_WIDTH = 512
N_HEADS = 4
DIFF_DIM = 64
Q_LORA = 512
KV_LORA = 512
NOPE_DIM = 128
ROPE_DIM_B = 64
MLA_QK = 256
DIL_PAIRS = ((128, 1), (512, 4), (2048, 16))
N_DIL_GROUPS = 3
N_EXPERTS = 8
TOP_K = 2
LANES = 128
LOG2_E = math.log2(math.e)

A_OFF = 0
B_OFF = 1536
C_OFF = 3072
D_OFF = 7680
QKV_COLS = 9216

VMEM_LIMIT = 56 * 1024 * 1024


def _cparams(sem, vmem=VMEM_LIMIT):
    return pltpu.CompilerParams(dimension_semantics=sem, vmem_limit_bytes=vmem)


def _dot(a, b):
    return jnp.dot(a, b, preferred_element_type=F32)


def _dot_nt(a, b):
    return lax.dot_general(a, b, (((1,), (1,)), ((), ())), preferred_element_type=F32)


def _rms(x, gain):
    return x * lax.rsqrt(jnp.mean(x * x, axis=-1, keepdims=True) + NORM_EPS) * gain


def _sigmoid(x):
    return 1.0 / (1.0 + jnp.exp(-x))


def _rope_tables(seq, segments):
    pos = np.arange(seq, dtype=np.float64)
    c = np.ones((seq, LANES), np.float64)
    s = np.zeros((seq, LANES), np.float64)
    r = np.zeros((LANES, LANES), np.float32)
    for start, rot in segments:
        half = rot // 2
        inv_freq = ROPE_THETA ** (-np.arange(0, rot, 2, dtype=np.float64) / rot)
        ang = pos[:, None] * inv_freq[None, :]
        cos, sin = np.cos(ang), np.sin(ang)
        c[:, start:start + half] = cos
        c[:, start + half:start + rot] = cos
        s[:, start:start + half] = -sin
        s[:, start + half:start + rot] = sin
        for i in range(half):
            r[start + half + i, start + i] = 1.0
            r[start + i, start + half + i] = 1.0
    return jnp.asarray(c, F32), jnp.asarray(s, F32), jnp.asarray(r, BF16)


def _rope(x_bf16, rot, c, s):
    return x_bf16.astype(F32) * c + _dot(x_bf16, rot) * s


def _norm_kernel(x_ref, g_ref, o_ref):
    o_ref[...] = _rms(x_ref[...], g_ref[...]).astype(o_ref.dtype)


def _rms_norm_bf16(x, gain, *, tm=512):
    t, d = x.shape
    return pl.pallas_call(
        _norm_kernel,
        out_shape=jax.ShapeDtypeStruct((t, d), BF16),
        grid=(t // tm,),
        in_specs=[pl.BlockSpec((tm, d), lambda i: (i, 0)), pl.BlockSpec((1, d), lambda i: (0, 0))],
        out_specs=pl.BlockSpec((tm, d), lambda i: (i, 0)),
        compiler_params=_cparams(("parallel",)),
        name="mix_norm",
    )(x, gain.reshape(1, d))


B_END = B_OFF + Q_LORA + KV_LORA + ROPE_DIM_B


INPROJ_ROW_SPLIT = 8


def _inproj_kernel(h_ref, *refs, gate_block):
    w_refs, o_ref = refs[:-1], refs[-1]
    j = pl.program_id(1)
    rows = h_ref.shape[0] // INPROJ_ROW_SPLIT
    for q, w_ref in enumerate(w_refs):
        n = w_ref.shape[1]
        w = w_ref[0].astype(BF16)
        for r in range(INPROJ_ROW_SPLIT):
            acc = _dot_nt(h_ref[r * rows:(r + 1) * rows, :], w)
            o_ref[r * rows:(r + 1) * rows, q * n:(q + 1) * n] = jnp.where(
                j >= gate_block, _sigmoid(acc), acc).astype(o_ref.dtype)


def _inproj(h, w_t, layer, *, tm=2048, tn=1024):
    t, d = h.shape
    n_out = w_t.shape[1] + C_OFF - B_END
    assert C_OFF % tn == 0 and n_out % tn == 0
    ns = WEIGHT_SPLIT

    unit = math.gcd(tn, C_OFF - B_END)

    def first_col(j, q):
        k = j * (tn // unit) + q * (tn // ns // unit)
        return unit * jnp.where(j * tn < C_OFF, k, k - (C_OFF - B_END) // unit)

    return pl.pallas_call(
        functools.partial(_inproj_kernel, gate_block=QKV_COLS // tn),
        out_shape=jax.ShapeDtypeStruct((t, n_out), BF16),
        grid=(t // tm, n_out // tn),
        in_specs=[pl.BlockSpec((tm, d), lambda i, j: (i, 0))]
                 + [pl.BlockSpec((pl.Element(1), pl.Element(tn // ns), pl.Element(d)),
                                 functools.partial(
                                     lambda i, j, q: (layer, first_col(j, q), 0), q=q))
                    for q in range(ns)],
        out_specs=pl.BlockSpec((tm, tn), lambda i, j: (i, j)),
        compiler_params=_cparams(("parallel", "arbitrary")),
        name="inproj",
    )(h, *([w_t] * ns))


def _osm(scores, values, carries):
    stats = []
    for s, (m, l, _) in zip(scores, carries):
        m_new = jnp.maximum(m, jnp.max(s, axis=-1, keepdims=True))
        alpha = jnp.exp2(m - m_new)
        p = jnp.exp2(s - m_new)
        stats.append((m_new, alpha * l + jnp.sum(p, axis=-1, keepdims=True), alpha, p))
    return tuple((m_new, l, alpha * acc + _dot(p.astype(BF16), v))
                 for (m_new, l, alpha, p), v, (_, _, acc) in zip(stats, values, carries))


def _causal_keep(t):
    r = lax.broadcasted_iota(jnp.int32, (t, t), 0)
    c = lax.broadcasted_iota(jnp.int32, (t, t), 1)
    return c <= r


def _diff_kernel(lam_ref, gain_ref, rot_ref, cq_ref, sq_ref, ck_ref, sk_ref,
                 q_ref, k_ref, v_ref, o_ref, kr_ref, *, tq, hp, lam_init):
    qi = pl.program_id(2)
    rot = rot_ref[...]
    heads = [slice(h * HEAD_DIM, (h + 1) * HEAD_DIM) for h in range(hp)]

    @pl.when(qi == 0)
    def _():
        for hs in heads:
            kr_ref[:, hs] = _rope(k_ref[:, hs], rot, ck_ref[...], sk_ref[...]).astype(BF16)

    scale2 = LOG2_E / math.sqrt(DIFF_DIM)
    lane = lax.broadcasted_iota(jnp.int32, (tq, HEAD_DIM), 1)
    chains = []
    for hs in heads:
        qf = _rope(q_ref[:, hs], rot, cq_ref[...], sq_ref[...])
        chains.append((jnp.where(lane < DIFF_DIM, qf, 0.0).astype(BF16), hs))
        chains.append((jnp.where(lane >= DIFF_DIM, qf, 0.0).astype(BF16), hs))

    def step(j, carry, masked):
        off = pl.multiple_of(j * tq, tq)
        scores = [_dot_nt(q, kr_ref[pl.ds(off, tq), hs]) * scale2 for q, hs in chains]
        if masked:
            keep = _causal_keep(tq)
            scores = [jnp.where(keep, s, -jnp.inf) for s in scores]
        return _osm(scores, [v_ref[pl.ds(off, tq), hs] for _, hs in chains], carry)

    init = (jnp.full((tq, 1), -jnp.inf, F32), jnp.zeros((tq, 1), F32),
            jnp.zeros((tq, HEAD_DIM), F32))
    carry = lax.fori_loop(0, qi, lambda j, c: step(j, c, False), tuple(init for _ in chains))
    carry = step(qi, carry, True)

    lam_rows = lam_ref[...]
    lam = (jnp.exp(jnp.sum(lam_rows[0:1] * lam_rows[1:2], axis=-1, keepdims=True))
           - jnp.exp(jnp.sum(lam_rows[2:3] * lam_rows[3:4], axis=-1, keepdims=True))
           + lam_init)
    for i, hs in enumerate(heads):
        (_, l1, a1), (_, l2, a2) = carry[2 * i], carry[2 * i + 1]
        out = a1 / l1 - lam * (a2 / l2)
        o_ref[:, hs] = (_rms(out, gain_ref[...]) * (1.0 - lam_init)).astype(o_ref.dtype)


def _diff_attention(proj, lam_rows, gain, tables, *, batch, seq, lam_init, tq=512, hp=4):
    c, s, rot = tables
    nq = seq // tq
    w = hp * HEAD_DIM
    cb = A_OFF // w
    ng = N_HEADS // hp
    return pl.pallas_call(
        functools.partial(_diff_kernel, tq=tq, hp=hp, lam_init=lam_init),
        out_shape=jax.ShapeDtypeStruct((batch * seq, BRANCH_WIDTH), BF16),
        grid=(batch, ng, nq),
        in_specs=[pl.BlockSpec((4, DIFF_DIM), lambda b, h, i: (0, 0)),
                  pl.BlockSpec((1, HEAD_DIM), lambda b, h, i: (0, 0)),
                  pl.BlockSpec((LANES, LANES), lambda b, h, i: (0, 0)),
                  pl.BlockSpec((tq, LANES), lambda b, h, i: (i, 0)),
                  pl.BlockSpec((tq, LANES), lambda b, h, i: (i, 0)),
                  pl.BlockSpec((seq, LANES), lambda b, h, i: (0, 0)),
                  pl.BlockSpec((seq, LANES), lambda b, h, i: (0, 0)),
                  pl.BlockSpec((tq, w), lambda b, h, i: (b * nq + i, cb + h)),
                  pl.BlockSpec((seq, w), lambda b, h, i: (b, cb + ng + h)),
                  pl.BlockSpec((seq, w), lambda b, h, i: (b, cb + 2 * ng + h))],
        out_specs=pl.BlockSpec((tq, w), lambda b, h, i: (b * nq + i, h)),
        scratch_shapes=[pltpu.VMEM((seq, w), BF16)],
        compiler_params=_cparams(("parallel", "parallel", "arbitrary")),
        name="diff_attention",
    )(lam_rows, gain.reshape(1, HEAD_DIM), rot, c, s, c, s, proj, proj, proj)


def _mla_prep_kernel(cq_ref, ckv_ref, kr_ref, dqn_ref, dkvn_ref, wuq_ref, wuk_ref, wuv_ref,
                     rot_ref, c_ref, s_ref, q_out, k_out, v_out):
    rot = rot_ref[...]
    c = c_ref[...]
    s = s_ref[...]
    hq = _rms(cq_ref[...].astype(F32), dqn_ref[...]).astype(BF16)
    hkv = _rms(ckv_ref[...].astype(F32), dkvn_ref[...]).astype(BF16)
    q = _dot(hq, wuq_ref[...])
    kn = _dot(hkv, wuk_ref[...])
    v_out[...] = _dot(hkv, wuv_ref[...]).astype(BF16)
    kr_lane = lax.broadcasted_iota(jnp.int32, kr_ref.shape, 1)
    kr = jnp.where(kr_lane < ROPE_DIM_B, kr_ref[...], jnp.zeros_like(kr_ref[...]))
    k_rope = _rope(kr, rot, c, s).astype(BF16)
    for h in range(N_HEADS):
        lo = h * MLA_QK
        q_out[:, lo:lo + NOPE_DIM] = q[:, lo:lo + NOPE_DIM].astype(BF16)
        q_out[:, lo + NOPE_DIM:lo + MLA_QK] = _rope(
            q[:, lo + NOPE_DIM:lo + MLA_QK].astype(BF16), rot, c, s).astype(BF16)
        k_out[:, lo:lo + NOPE_DIM] = kn[:, h * NOPE_DIM:(h + 1) * NOPE_DIM].astype(BF16)
        k_out[:, lo + NOPE_DIM:lo + MLA_QK] = k_rope


def _mla_prep(proj, dq_norm, dkv_norm, wuq, wuk, wuv, tables, *, seq, tm=512):
    c, s, rot = tables
    t = proj.shape[0]
    nb = seq // tm
    cb = B_OFF // Q_LORA
    full = lambda shape: pl.BlockSpec(shape, lambda i: (0, 0))
    return pl.pallas_call(
        _mla_prep_kernel,
        out_shape=(jax.ShapeDtypeStruct((t, N_HEADS * MLA_QK), BF16),
                   jax.ShapeDtypeStruct((t, N_HEADS * MLA_QK), BF16),
                   jax.ShapeDtypeStruct((t, N_HEADS * HEAD_DIM), BF16)),
        grid=(t // tm,),
        in_specs=[pl.BlockSpec((tm, Q_LORA), lambda i: (i, cb)),
                  pl.BlockSpec((tm, KV_LORA), lambda i: (i, cb + 1)),
                  pl.BlockSpec((tm, LANES), lambda i: (i, (B_OFF + Q_LORA + KV_LORA) // LANES)),
                  full((1, Q_LORA)), full((1, KV_LORA)),
                  full(wuq.shape), full(wuk.shape), full(wuv.shape),
                  full((LANES, LANES)),
                  pl.BlockSpec((tm, LANES), lambda i: (i % nb, 0)),
                  pl.BlockSpec((tm, LANES), lambda i: (i % nb, 0))],
        out_specs=(pl.BlockSpec((tm, N_HEADS * MLA_QK), lambda i: (i, 0)),
                   pl.BlockSpec((tm, N_HEADS * MLA_QK), lambda i: (i, 0)),
                   pl.BlockSpec((tm, N_HEADS * HEAD_DIM), lambda i: (i, 0))),
        compiler_params=_cparams(("parallel",)),
        name="mla_prep",
    )(proj, proj, proj, dq_norm.reshape(1, Q_LORA), dkv_norm.reshape(1, KV_LORA),
      wuq, wuk, wuv, rot, c, s)


def _flash_kernel(q_ref, k_ref, v_ref, o_ref, *, tq, hp, dqk, scale):
    qi = pl.program_id(2)
    qk = [slice(h * dqk, (h + 1) * dqk) for h in range(hp)]
    hv = [slice(h * HEAD_DIM, (h + 1) * HEAD_DIM) for h in range(hp)]
    qs = [q_ref[:, sl] for sl in qk]

    def step(j, carry, masked):
        off = pl.multiple_of(j * tq, tq)
        scores = [_dot_nt(q, k_ref[pl.ds(off, tq), ks]) * (scale * LOG2_E) for q, ks in zip(qs, qk)]
        if masked:
            keep = _causal_keep(tq)
            scores = [jnp.where(keep, s, -jnp.inf) for s in scores]
        return _osm(scores, [v_ref[pl.ds(off, tq), vs] for vs in hv], carry)

    init = (jnp.full((tq, 1), -jnp.inf, F32), jnp.zeros((tq, 1), F32),
            jnp.zeros((tq, HEAD_DIM), F32))
    carry = lax.fori_loop(0, qi, lambda j, c: step(j, c, False), tuple(init for _ in qs))
    carry = step(qi, carry, True)
    for vs, (_, l, acc) in zip(hv, carry):
        o_ref[:, vs] = (acc / l).astype(o_ref.dtype)


def _mla_attention(q, k, v, *, batch, seq, tq=512, hp=4):
    nq = seq // tq
    ng = N_HEADS // hp
    return pl.pallas_call(
        functools.partial(_flash_kernel, tq=tq, hp=hp, dqk=MLA_QK,
                          scale=1.0 / math.sqrt(NOPE_DIM + ROPE_DIM_B)),
        out_shape=jax.ShapeDtypeStruct((batch * seq, BRANCH_WIDTH), BF16),
        grid=(batch, ng, nq),
        in_specs=[pl.BlockSpec((tq, hp * MLA_QK), lambda b, h, i: (b * nq + i, h)),
                  pl.BlockSpec((seq, hp * MLA_QK), lambda b, h, i: (b, h)),
                  pl.BlockSpec((seq, hp * HEAD_DIM), lambda b, h, i: (b, h))],
        out_specs=pl.BlockSpec((tq, hp * HEAD_DIM), lambda b, h, i: (b * nq + i, h)),
        compiler_params=_cparams(("parallel", "parallel", "arbitrary")),
        name="mla_attention",
    )(q, k, v)


DIL_BLOCK = 128


def _dilated_kernel(rot_ref, c_ref, s_ref, *refs, seq):
    in_refs = refs[:9]
    o_ref = refs[9]
    qf, kf, vf, og, lse = refs[10:]
    rot = rot_ref[...]
    c = c_ref[...]
    s = s_ref[...]
    for g in range(N_DIL_GROUPS):
        qf[g] = _rope(in_refs[g][...], rot, c, s)
        kf[g] = _rope(in_refs[3 + g][...], rot, c, s)
        vf[g] = in_refs[6 + g][...].astype(F32)
    scale = 1.0 / math.sqrt(HEAD_DIM)
    blk = DIL_BLOCK

    def rows(start, size, stride):
        return pl.ds(start, size) if stride == 1 else pl.ds(start, size, stride=stride)

    for g, (window, dil) in enumerate(DIL_PAIRS):
        assert window == blk * dil
        span = blk * dil
        n_sub = seq // span
        nk = 2 * blk if n_sub > 1 else blk
        assert n_sub & (n_sub - 1) == 0
        q_rows, k_rows = [], []
        for r in range(dil):
            for cb in range(n_sub):
                q_rows.append(rows(r + cb * span, blk, dil))
                k_rows.append(rows(r + max(cb - 1, 0) * span, nk, dil))
        nb = len(q_rows)
        q = jnp.stack([qf[g, qr, :] for qr in q_rows]).astype(BF16)
        k = jnp.stack([kf[g, kr, :] for kr in k_rows]).astype(BF16)
        v = jnp.stack([vf[g, kr, :] for kr in k_rows]).astype(BF16)
        sc = jnp.einsum("bqd,bkd->bqk", q, k, preferred_element_type=F32) * (scale * LOG2_E)
        first = (lax.broadcasted_iota(jnp.int32, (nb, blk, nk), 0) & (n_sub - 1)) == 0
        dist = (lax.broadcasted_iota(jnp.int32, (nb, blk, nk), 1)
                - lax.broadcasted_iota(jnp.int32, (nb, blk, nk), 2)
                + jnp.where(first, 0, nk - blk))
        sc = jnp.where(dist >= 0, jnp.where(dist <= blk, sc, -jnp.inf), -jnp.inf)
        m = jnp.max(sc, axis=-1, keepdims=True)
        e = jnp.exp2(sc - m)
        den = jnp.sum(e, axis=-1, keepdims=True)
        o = jnp.einsum("bqk,bkd->bqd", e.astype(BF16), v, preferred_element_type=F32) / den
        lg = jnp.broadcast_to(m + jnp.log2(den), (nb, blk, HEAD_DIM))
        for i, qr in enumerate(q_rows):
            og[g, qr, :] = o[i]
            lse[g, qr, :] = lg[i]

    l0, l1, l2 = lse[0], lse[1], lse[2]
    mx = jnp.maximum(jnp.maximum(l0, l1), l2)
    w0, w1, w2 = jnp.exp2(l0 - mx), jnp.exp2(l1 - mx), jnp.exp2(l2 - mx)
    o_ref[...] = ((w0 * og[0] + w1 * og[1] + w2 * og[2]) / (w0 + w1 + w2)).astype(o_ref.dtype)


def _dilated_attention(proj, tables, *, batch, seq):
    c, s, rot = tables
    cb = C_OFF // HEAD_DIM
    nh = N_DIL_GROUPS * N_HEADS

    def col(kind, g):
        return lambda b, h: (b, cb + kind * nh + g * N_HEADS + h)

    in_specs = [pl.BlockSpec((LANES, LANES), lambda b, h: (0, 0)),
                pl.BlockSpec((seq, LANES), lambda b, h: (0, 0)),
                pl.BlockSpec((seq, LANES), lambda b, h: (0, 0))]
    for kind in range(3):
        for g in range(N_DIL_GROUPS):
            in_specs.append(pl.BlockSpec((seq, HEAD_DIM), col(kind, g)))
    return pl.pallas_call(
        functools.partial(_dilated_kernel, seq=seq),
        out_shape=jax.ShapeDtypeStruct((batch * seq, BRANCH_WIDTH), BF16),
        grid=(batch, N_HEADS),
        in_specs=in_specs,
        out_specs=pl.BlockSpec((seq, HEAD_DIM), lambda b, h: (b, h)),
        scratch_shapes=[pltpu.VMEM((N_DIL_GROUPS, seq, HEAD_DIM), F32) for _ in range(5)],
        compiler_params=_cparams(("parallel", "parallel")),
        name="dilated_attention",
    )(rot, c, s, *([proj] * 9))


def _stick_kernel(q_ref, k_ref, v_ref, o_ref, *, tq, hp, scale):
    qi = pl.program_id(2)
    heads = [slice(h * HEAD_DIM, (h + 1) * HEAD_DIM) for h in range(hp)]
    qs = [q_ref[:, hs] for hs in heads]
    r = lax.broadcasted_iota(jnp.int32, (tq, tq), 0)
    c = lax.broadcasted_iota(jnp.int32, (tq, tq), 1)
    later_keys = jnp.where(r > c, 1.0, 0.0).astype(BF16)

    def step(j, carry, diag):
        off = pl.multiple_of(j * tq, tq)
        strict = c < r
        z2 = [_dot_nt(q, k_ref[pl.ds(off, tq), hs]) * (scale * LOG2_E) for q, hs in zip(qs, heads)]
        sp2 = [jnp.maximum(z, 0.0) + jnp.log2(1.0 + jnp.exp2(-jnp.abs(z))) for z in z2]
        log_not = [jnp.where(strict, -sp, 0.0) if diag else -sp for sp in sp2]
        later = []
        for ln, (tail, _) in zip(log_not, carry):
            hi = ln.astype(BF16)
            lo = (ln - hi.astype(F32)).astype(BF16)
            later.append(_dot(hi, later_keys) + _dot(lo, later_keys) + tail)
        a = [jnp.exp2((z - sp) + lt) for z, sp, lt in zip(z2, sp2, later)]
        if diag:
            a = [jnp.where(strict, x, 0.0) for x in a]
        out = []
        for x, ln, hs, (tail, acc) in zip(a, log_not, heads, carry):
            acc = acc + _dot(x.astype(BF16), v_ref[pl.ds(off, tq), hs])
            out.append((tail + jnp.sum(ln, axis=-1, keepdims=True), acc))
        return tuple(out)

    init = (jnp.zeros((tq, 1), F32), jnp.zeros((tq, HEAD_DIM), F32))
    carry = step(qi, tuple(init for _ in heads), True)
    carry = lax.fori_loop(0, qi, lambda t, cr: step(qi - 1 - t, cr, False), carry)
    for hs, (_, acc) in zip(heads, carry):
        o_ref[:, hs] = acc.astype(o_ref.dtype)


def _stick_attention(proj, *, batch, seq, tq=256, hp=4):
    nq = seq // tq
    w = hp * HEAD_DIM
    cb = D_OFF // w
    ng = N_HEADS // hp
    return pl.pallas_call(
        functools.partial(_stick_kernel, tq=tq, hp=hp, scale=1.0 / math.sqrt(HEAD_DIM)),
        out_shape=jax.ShapeDtypeStruct((batch * seq, BRANCH_WIDTH), BF16),
        grid=(batch, ng, nq),
        in_specs=[pl.BlockSpec((tq, w), lambda b, h, i: (b * nq + i, cb + h)),
                  pl.BlockSpec((seq, w), lambda b, h, i: (b, cb + ng + h)),
                  pl.BlockSpec((seq, w), lambda b, h, i: (b, cb + 2 * ng + h))],
        out_specs=pl.BlockSpec((tq, w), lambda b, h, i: (b * nq + i, h)),
        compiler_params=_cparams(("parallel", "parallel", "arbitrary")),
        name="stick_attention",
    )(proj, proj, proj)


def _merge_kernel(ya, yb, yc, yd, g0, g1, g2, g3, wb_ref, o_ref):
    acc = None
    for i, (y, g) in enumerate(((ya, g0), (yb, g1), (yc, g2), (yd, g3))):
        t = g[...].astype(F32) * _dot(y[...], wb_ref[i])
        acc = t if acc is None else acc + t
    o_ref[...] = acc.astype(o_ref.dtype)


def _merge(ys, proj, wb, *, tm=1024, tn=1024):
    t = proj.shape[0]
    d = wb.shape[2]
    nn = d // tn
    g0 = QKV_COLS // tn
    y_spec = pl.BlockSpec((tm, BRANCH_WIDTH), lambda i, j: (i, 0))
    g_specs = [pl.BlockSpec((tm, tn), functools.partial(lambda i, j, b: (i, g0 + b * nn + j), b=b))
               for b in range(N_BRANCH)]
    return pl.pallas_call(
        _merge_kernel,
        out_shape=jax.ShapeDtypeStruct((t, d), BF16),
        grid=(t // tm, nn),
        in_specs=[y_spec] * N_BRANCH + g_specs
                 + [pl.BlockSpec((N_BRANCH, BRANCH_WIDTH, tn), lambda i, j: (0, 0, j))],
        out_specs=pl.BlockSpec((tm, tn), lambda i, j: (i, j)),
        compiler_params=_cparams(("parallel", "arbitrary")),
        name="branch_merge",
    )(*ys, proj, proj, proj, proj, wb)


def _outproj_kernel(x_ref, m_ref, w_ref, o_ref):
    o_ref[...] = x_ref[...] + _dot(m_ref[...], w_ref[...])


def _outproj(x, merged, w, *, tm=1024, tn=1024):
    t, d = x.shape
    return pl.pallas_call(
        _outproj_kernel,
        out_shape=jax.ShapeDtypeStruct((t, d), F32),
        grid=(t // tm, d // tn),
        in_specs=[pl.BlockSpec((tm, tn), lambda i, j: (i, j)),
                  pl.BlockSpec((tm, d), lambda i, j: (i, 0)),
                  pl.BlockSpec((d, tn), lambda i, j: (0, j))],
        out_specs=pl.BlockSpec((tm, tn), lambda i, j: (i, j)),
        compiler_params=_cparams(("parallel", "arbitrary")),
        name="out_proj",
    )(x, merged, w)


def _router_kernel(x_ref, gain_ref, wr_ref, idx_ref, gate_ref):
    h = _rms(x_ref[...], gain_ref[...])
    logits = jnp.dot(h, wr_ref[...], preferred_element_type=F32, precision=lax.Precision.HIGHEST)
    lane = lax.broadcasted_iota(jnp.int32, logits.shape, 1)
    lanef = lane.astype(F32)
    lg = jnp.where(lane < N_EXPERTS, logits, -jnp.inf)
    v1 = jnp.max(lg, axis=-1, keepdims=True)
    i1 = jnp.min(jnp.where(lg == v1, lanef, float(LANES)), axis=-1, keepdims=True)
    lg2 = jnp.where(lanef == i1, -jnp.inf, lg)
    v2 = jnp.max(lg2, axis=-1, keepdims=True)
    i2 = jnp.min(jnp.where(lg2 == v2, lanef, float(LANES)), axis=-1, keepdims=True)
    e2 = jnp.exp(v2 - v1)
    g1 = 1.0 / (1.0 + e2)
    g2 = e2 / (1.0 + e2)
    idx_ref[...] = jnp.where(lane == 0, i1, jnp.where(lane == 1, i2, 0.0)).astype(jnp.int32)
    gate_ref[...] = jnp.where(lane == 0, g1, jnp.where(lane == 1, g2, 0.0))


def _router(x, gain, w_router, *, tm=512):
    t, d = x.shape
    wr = jnp.zeros((d, LANES), F32).at[:, :N_EXPERTS].set(w_router)
    return pl.pallas_call(
        _router_kernel,
        out_shape=(jax.ShapeDtypeStruct((t, LANES), jnp.int32),
                   jax.ShapeDtypeStruct((t, LANES), F32)),
        grid=(t // tm,),
        in_specs=[pl.BlockSpec((tm, d), lambda i: (i, 0)),
                  pl.BlockSpec((1, d), lambda i: (0, 0)),
                  pl.BlockSpec((d, LANES), lambda i: (0, 0))],
        out_specs=(pl.BlockSpec((tm, LANES), lambda i: (i, 0)),
                   pl.BlockSpec((tm, LANES), lambda i: (i, 0))),
        compiler_params=_cparams(("parallel",)),
        name="router",
    )(x, gain.reshape(1, d), wr)


def _start_row_gather(idx_ref, base, n, src_hbm, dst, sem):
    def start(r, _):
        row = idx_ref[base + r]
        pltpu.make_async_copy(src_hbm.at[pl.ds(row, 1), :], dst.at[pl.ds(r, 1), :], sem).start()
        return 0

    lax.fori_loop(0, n, start, 0, unroll=8)


def _wait_row_gather(n, src_hbm, dst, sem):
    pltpu.make_async_copy(src_hbm.at[pl.ds(0, n), :], dst.at[pl.ds(0, n), :], sem).wait()


MOE_CHUNK = 512
MOE_VISIT_CHUNKS = 5
MOE_TF = 512
SWIGLU_VMEM_LIMIT = 62 * 1024 * 1024
DENSE_CHUNK = 512
DENSE_VISIT_CHUNKS = 4
DENSE_TF = 512
WEIGHT_SPLIT = 1


def _moe_kernel(vis_e_ref, vis_ok_ref, vis_row_ref, vis_nch_ref, tail_ref, row_tok_ref,
                x_hbm, gain_ref, *refs, ch, dense):
    del vis_e_ref, vis_ok_ref
    ns = WEIGHT_SPLIT
    wg_refs, wu_refs, wo_refs = refs[:ns], refs[ns:2 * ns], refs[2 * ns:3 * ns]
    y_hbm, h_ref, acc_ref, gsem, osem = refs[3 * ns:]
    v = pl.program_id(0)
    f = pl.program_id(1)
    nv = pl.num_programs(0)
    nf = pl.num_programs(1)
    half = ch // 2
    n_unit = vis_nch_ref[v]
    n_full = n_unit // 2
    odd = n_unit - 2 * n_full
    row0 = vis_row_ref[v]
    active = n_unit > 0
    kq = h_ref.shape[1] // ns

    def pieces(fn):
        def full(c, _):
            fn(pl.multiple_of(c * ch, ch), ch)
            return 0

        lax.fori_loop(0, n_full, full, 0)

        @pl.when(odd == 1)
        def _():
            fn(pl.multiple_of(n_full * ch, half), half)

    def step(h):
        parts = [h[:, q * kq:(q + 1) * kq] for q in range(ns)]
        g = sum(_dot(hq, w[...].astype(BF16)) for hq, w in zip(parts, wg_refs))
        u = sum(_dot(hq, w[...].astype(BF16)) for hq, w in zip(parts, wu_refs))
        a = (g * _sigmoid(g) * u).astype(BF16)
        return jnp.concatenate([_dot(a, w[...].astype(BF16)) for w in wo_refs], axis=1)

    def accumulate(r, n):
        acc_ref[pl.ds(r, n), :] += step(h_ref[pl.ds(r, n), :])

    def out_copy(r, n):
        dst = y_hbm.at[pl.ds(pl.multiple_of(row0 + r, half), n), :]
        return pltpu.make_async_copy(acc_ref.at[pl.ds(r, n), :], dst, osem)

    @pl.when(jnp.logical_and(active, f == 0))
    def _():
        def fetch(r, n):
            dst = acc_ref.at[pl.ds(r, n), :]
            if dense:
                src = x_hbm.at[pl.ds(pl.multiple_of(row0 + r, half), n), :]
                pltpu.make_async_copy(src, dst, gsem).start()
            else:
                _start_row_gather(row_tok_ref, row0 + r, n, x_hbm, dst, gsem)

        def arrived(r, n):
            _wait_row_gather(n, x_hbm, acc_ref.at[pl.ds(r, n), :], gsem)

        def first_sweep(r, n):
            x = acc_ref[pl.ds(r, n), :]
            h = _rms(x, gain_ref[...]).astype(BF16)
            h_ref[pl.ds(r, n), :] = h
            acc_ref[pl.ds(r, n), :] = x + step(h) if dense else step(h)

        pieces(fetch)
        pieces(arrived)
        pieces(first_sweep)

    @pl.when(jnp.logical_and(active, jnp.logical_and(f > 0, f < nf - 1)))
    def _():
        pieces(accumulate)

    @pl.when(jnp.logical_and(active, f == nf - 1))
    def _():
        def last_sweep(r, n):
            accumulate(r, n)
            out_copy(r, n).start()

        pieces(last_sweep)
        pieces(lambda r, n: out_copy(r, n).wait())

    @pl.when(jnp.logical_and(v == nv - 1, f == nf - 1))
    def _():
        first = tail_ref[0]
        n_tail = y_hbm.shape[0] // half - first
        acc_ref[pl.ds(0, half), :] = jnp.zeros((half, acc_ref.shape[1]), acc_ref.dtype)

        def tail_copy(c):
            dst = y_hbm.at[pl.ds(pl.multiple_of((first + c) * half, half), half), :]
            return pltpu.make_async_copy(acc_ref.at[pl.ds(0, half), :], dst, osem)

        def fill(c, _):
            tail_copy(c).start()
            return 0

        def drain(c, _):
            tail_copy(c).wait()
            return 0

        lax.fori_loop(0, n_tail, fill, 0)
        lax.fori_loop(0, n_tail, drain, 0)


def _dense_visits(t, ch, max_chunks):
    span = ch * max_chunks
    half = ch // 2
    row0 = np.arange(0, t, span, dtype=np.int32)
    n_unit = np.minimum(2 * max_chunks, (t - row0) // half).astype(np.int32)
    zeros = np.zeros_like(row0)
    return tuple(jnp.asarray(a) for a in (zeros, zeros + 1, row0, n_unit,
                                          np.array([t // half], np.int32), np.zeros((1,), np.int32)))


def _swiglu_rows(x, gain, tables, w_in, w_out, *, dense, ch, max_chunks, tf, n_visits=None):
    vis_e, vis_ok, vis_row, vis_nch, tail, row_tok = tables
    t, d = x.shape
    ff = w_out.shape[1]
    nf = ff // tf
    p = t if dense else row_tok.shape[0]
    ns = WEIGHT_SPLIT

    def wspec(shape, index):
        def index_map(v, f, e, ok, *_):
            return index(e[v], jnp.where(ok[v] > 0, f, nf - 1))
        return pl.BlockSpec(shape, index_map)

    wg_specs = [wspec((None, d // ns, tf), functools.partial(lambda e, f, q: (e, q, f), q=q))
                for q in range(ns)]
    wu_specs = [wspec((None, d // ns, tf), functools.partial(lambda e, f, q: (e, q, nf + f), q=q))
                for q in range(ns)]
    wo_specs = [wspec((None, tf, d // ns), functools.partial(lambda e, f, q: (e, f, q), q=q))
                for q in range(ns)]
    grid_spec = pltpu.PrefetchScalarGridSpec(
        num_scalar_prefetch=6,
        grid=(vis_e.shape[0] if n_visits is None else n_visits, nf),
        in_specs=[pl.BlockSpec(memory_space=pl.ANY),
                  pl.BlockSpec((1, d), lambda v, f, *_: (0, 0))] + wg_specs + wu_specs + wo_specs,
        out_specs=pl.BlockSpec(memory_space=pl.ANY),
        scratch_shapes=[pltpu.VMEM((max_chunks * ch, d), BF16),
                        pltpu.VMEM((max_chunks * ch, d), F32),
                        pltpu.SemaphoreType.DMA, pltpu.SemaphoreType.DMA],
    )
    return pl.pallas_call(
        functools.partial(_moe_kernel, ch=ch, dense=dense),
        out_shape=jax.ShapeDtypeStruct((p, d), F32),
        grid_spec=grid_spec,
        compiler_params=_cparams(("arbitrary", "arbitrary"), vmem=SWIGLU_VMEM_LIMIT),
        name="dense_ffn" if dense else "moe_experts",
    )(vis_e, vis_ok, vis_row, vis_nch, tail, row_tok, x, gain.reshape(1, d),
      *([w_in] * (2 * ns)), *([w_out] * ns))


def _combine_kernel(pos_ref, x_ref, gate_ref, y_hbm, gain_ref, o_ref, buf, sem, *, tm):
    i = pl.program_id(0)
    n = pl.num_programs(0)
    t = pos_ref.shape[0] // TOP_K
    slot = lax.rem(i, 2)

    def start(step, sl):
        for k in range(TOP_K):
            _start_row_gather(pos_ref, k * t + step * tm, tm, y_hbm, buf.at[sl, k], sem.at[sl, k])

    @pl.when(i == 0)
    def _():
        start(0, 0)

    @pl.when(i + 1 < n)
    def _():
        start(i + 1, 1 - slot)

    for k in range(TOP_K):
        _wait_row_gather(tm, y_hbm, buf.at[slot, k], sem.at[slot, k])
    g = gate_ref[...]
    x = x_ref[...] + (g[:, 0:1] * buf[slot, 0] + g[:, 1:2] * buf[slot, 1])
    if gain_ref is None:
        o_ref[...] = x
    else:
        o_ref[...] = _rms(x, gain_ref[...])


def _moe_combine(x, gate, y_rows, pos, final_gain, *, tm=512):
    t, d = x.shape
    in_specs = [pl.BlockSpec((tm, d), lambda i, ps: (i, 0)),
                pl.BlockSpec((tm, LANES), lambda i, ps: (i, 0)),
                pl.BlockSpec(memory_space=pl.ANY)]
    args = [x, gate, y_rows]
    if final_gain is not None:
        in_specs.append(pl.BlockSpec((1, d), lambda i, ps: (0, 0)))
        args.append(final_gain.reshape(1, d))
        body = functools.partial(_combine_kernel, tm=tm)
    else:
        def body(pos_ref, x_ref, gate_ref, y_hbm, o_ref, *scratch):
            _combine_kernel(pos_ref, x_ref, gate_ref, y_hbm, None, o_ref, *scratch, tm=tm)
    grid_spec = pltpu.PrefetchScalarGridSpec(
        num_scalar_prefetch=1,
        grid=(t // tm,),
        in_specs=in_specs,
        out_specs=pl.BlockSpec((tm, d), lambda i, ps: (i, 0)),
        scratch_shapes=[pltpu.VMEM((2, TOP_K, tm, d), F32), pltpu.SemaphoreType.DMA((2, TOP_K))],
    )
    return pl.pallas_call(
        body,
        out_shape=jax.ShapeDtypeStruct((t, d), F32),
        grid_spec=grid_spec,
        compiler_params=_cparams(("arbitrary",)),
        name="moe_combine",
    )(pos, *args)


def _final_norm_kernel(x_ref, g_ref, o_ref):
    o_ref[...] = _rms(x_ref[...], g_ref[...])


def _final_norm(x, gain, *, tm=512):
    t, d = x.shape
    return pl.pallas_call(
        _final_norm_kernel,
        out_shape=jax.ShapeDtypeStruct((t, d), F32),
        grid=(t // tm,),
        in_specs=[pl.BlockSpec((tm, d), lambda i: (i, 0)), pl.BlockSpec((1, d), lambda i: (0, 0))],
        out_specs=pl.BlockSpec((tm, d), lambda i: (i, 0)),
        compiler_params=_cparams(("parallel",)),
        name="final_norm",
    )(x, gain.reshape(1, d))


def _routing_tables(idx, ch, max_chunks):
    ch, max_chunks = ch // 2, 2 * max_chunks
    t = idx.shape[0]
    e_flat = idx[:, :TOP_K].T.reshape(-1)
    tok = jnp.tile(jnp.arange(t, dtype=jnp.int32), TOP_K)
    experts = jnp.arange(N_EXPERTS, dtype=jnp.int32)
    onehot = (e_flat[:, None] == experts[None, :]).astype(jnp.int32)
    rank = jnp.sum((jnp.cumsum(onehot, axis=0) - onehot) * onehot, axis=1)
    counts = jnp.sum(onehot, axis=0)
    n_chunk = (counts + ch - 1) // ch
    ends = jnp.cumsum(n_chunk * ch)
    starts = ends - n_chunk * ch
    dest = (starts[e_flat] + rank).astype(jnp.int32)
    p = TOP_K * t + N_EXPERTS * ch
    row_tok = jnp.zeros((p,), jnp.int32).at[dest].set(tok)

    n_vis = (n_chunk + max_chunks - 1) // max_chunks
    per_vis = (n_chunk + jnp.maximum(n_vis, 1) - 1) // jnp.maximum(n_vis, 1)
    v_end = jnp.cumsum(n_vis)
    v_start = v_end - n_vis
    nv_max = (p // ch + (max_chunks - 1) * N_EXPERTS) // max_chunks
    slot = jnp.arange(nv_max, dtype=jnp.int32)
    ok = slot < v_end[-1]
    e_of = jnp.minimum(jnp.sum((slot[:, None] >= v_end[None, :]).astype(jnp.int32), axis=1),
                       N_EXPERTS - 1)
    e_of = jnp.where(ok, e_of, e_of[v_end[-1] - 1])
    k = slot - v_start[e_of]
    n_ch = jnp.where(ok, jnp.clip(n_chunk[e_of] - k * per_vis[e_of], 0, per_vis[e_of]), 0)
    row0 = jnp.where(ok, starts[e_of] + k * per_vis[e_of] * ch, 0)
    tail = (ends[-1:] // ch).astype(jnp.int32)
    tables = (e_of.astype(jnp.int32), ok.astype(jnp.int32), row0.astype(jnp.int32),
              n_ch.astype(jnp.int32), tail, row_tok)
    return tables, dest, v_end[-1].astype(jnp.int32)


def _relayout_mla(w_uq, w_ukv):
    dq = NOPE_DIM + ROPE_DIM_B
    wq = w_uq.reshape(Q_LORA, N_HEADS, dq)
    wq = jnp.concatenate([wq, jnp.zeros((Q_LORA, N_HEADS, MLA_QK - dq), w_uq.dtype)], axis=2)
    wkv = w_ukv.reshape(KV_LORA, N_HEADS, NOPE_DIM + HEAD_DIM)
    wk = wkv[:, :, :NOPE_DIM].reshape(KV_LORA, N_HEADS * NOPE_DIM)
    wv = wkv[:, :, NOPE_DIM:].reshape(KV_LORA, N_HEADS * HEAD_DIM)
    return wq.reshape(Q_LORA, N_HEADS * MLA_QK).astype(BF16), wk.astype(BF16), wv.astype(BF16)


def kernel(x, norm_mix, w_in, dq_norm, dkv_norm, w_uq, w_ukv, lam_q1, lam_k1, lam_q2, lam_k2,
           diff_norm, w_branch, w_out, norm_ffn, w_dense_in, w_dense_out, w_router,
           w_moe_in, w_moe_out, norm_final):
    batch, seq, d = x.shape
    depth = w_in.shape[0]
    t = batch * seq
    xt = x.reshape(t, d)

    rot_a = DIFF_DIM // 4
    tab_a = _rope_tables(seq, ((0, rot_a), (DIFF_DIM, rot_a)))
    tab_b = _rope_tables(seq, ((0, ROPE_DIM_B),))
    tab_c = _rope_tables(seq, ((0, HEAD_DIM // 4),))
    w_in_t = jnp.swapaxes(w_in, 1, 2)

    for layer in range(depth):
        proj = _inproj(_rms_norm_bf16(xt, norm_mix[layer]), w_in_t, layer)

        lam_init = 0.8 - 0.6 * math.exp(-0.3 * layer)
        lam_rows = jnp.stack([lam_q1[layer], lam_k1[layer], lam_q2[layer], lam_k2[layer]])
        y_a = _diff_attention(proj, lam_rows, diff_norm[layer], tab_a,
                              batch=batch, seq=seq, lam_init=lam_init)

        wuq, wuk, wuv = _relayout_mla(w_uq[layer], w_ukv[layer])
        q_b, k_b, v_b = _mla_prep(proj, dq_norm[layer], dkv_norm[layer], wuq, wuk, wuv, tab_b, seq=seq)
        y_b = _mla_attention(q_b, k_b, v_b, batch=batch, seq=seq)

        y_c = _dilated_attention(proj, tab_c, batch=batch, seq=seq)
        y_d = _stick_attention(proj, batch=batch, seq=seq)

        merged = _merge((y_a, y_b, y_c, y_d), proj, w_branch[layer].astype(BF16))
        xt = _outproj(xt, merged, w_out[layer].astype(BF16))

        last = layer == depth - 1
        if layer % 2 == 0:
            m = layer // 2
            xt = _swiglu_rows(xt, norm_ffn[layer], _dense_visits(t, DENSE_CHUNK, DENSE_VISIT_CHUNKS),
                              w_dense_in[m:m + 1], w_dense_out[m:m + 1], dense=True,
                              ch=DENSE_CHUNK, max_chunks=DENSE_VISIT_CHUNKS, tf=DENSE_TF)
            if last:
                xt = _final_norm(xt, norm_final)
        else:
            m = layer // 2
            idx, gate = _router(xt, norm_ffn[layer], w_router[m])
            tables, dest, n_vis = _routing_tables(idx, MOE_CHUNK, MOE_VISIT_CHUNKS)
            y_rows = _swiglu_rows(xt, norm_ffn[layer], tables, w_moe_in[m], w_moe_out[m], dense=False,
                                  ch=MOE_CHUNK, max_chunks=MOE_VISIT_CHUNKS, tf=MOE_TF, n_visits=n_vis)
            xt = _moe_combine(xt, gate, y_rows, dest, norm_final if last else None)
    return xt.reshape(batch, seq, d)
```

```python
import functools
import math

import numpy as np
import jax
import jax.numpy as jnp
from jax import lax
from jax.experimental import pallas as pl
from jax.experimental.pallas import tpu as pltpu

F32 = jnp.float32
BF16 = jnp.bfloat16

HEAD_DIM = 128
ROPE_THETA = 500000.0
NORM_EPS = 1e-6
N_BRANCH = 4
BRANCH_WIDTH = 512
N_HEADS = 4
DIFF_DIM = 64
Q_LORA = 512
KV_LORA = 512
NOPE_DIM = 128
ROPE_DIM_B = 64
MLA_QK = 256
DIL_PAIRS = ((128, 1), (512, 4), (2048, 16))
N_DIL_GROUPS = 3
N_EXPERTS = 8
TOP_K = 2
LANES = 128
LOG2_E = math.log2(math.e)

A_OFF = 0
B_OFF = 1536
C_OFF = 3072
D_OFF = 7680
QKV_COLS = 9216

VMEM_LIMIT = 56 * 1024 * 1024


def _cparams(sem, vmem=VMEM_LIMIT):
    return pltpu.CompilerParams(dimension_semantics=sem, vmem_limit_bytes=vmem)


def _dot(a, b):
    return jnp.dot(a, b, preferred_element_type=F32)


def _dot_nt(a, b):
    return lax.dot_general(a, b, (((1,), (1,)), ((), ())), preferred_element_type=F32)


def _rms(x, gain):
    return x * lax.rsqrt(jnp.mean(x * x, axis=-1, keepdims=True) + NORM_EPS) * gain


def _sigmoid(x):
    return 1.0 / (1.0 + jnp.exp(-x))


def _rope_tables(seq, segments):
    pos = np.arange(seq, dtype=np.float64)
    c = np.ones((seq, LANES), np.float64)
    s = np.zeros((seq, LANES), np.float64)
    r = np.zeros((LANES, LANES), np.float32)
    for start, rot in segments:
        half = rot // 2
        inv_freq = ROPE_THETA ** (-np.arange(0, rot, 2, dtype=np.float64) / rot)
        ang = pos[:, None] * inv_freq[None, :]
        cos, sin = np.cos(ang), np.sin(ang)
        c[:, start:start + half] = cos
        c[:, start + half:start + rot] = cos
        s[:, start:start + half] = -sin
        s[:, start + half:start + rot] = sin
        for i in range(half):
            r[start + half + i, start + i] = 1.0
            r[start + i, start + half + i] = 1.0
    return jnp.asarray(c, F32), jnp.asarray(s, F32), jnp.asarray(r, BF16)


def _rope(x_bf16, rot, c, s):
    return x_bf16.astype(F32) * c + _dot(x_bf16, rot) * s


def _norm_kernel(x_ref, g_ref, o_ref):
    o_ref[...] = _rms(x_ref[...], g_ref[...]).astype(o_ref.dtype)


def _rms_norm_bf16(x, gain, *, tm=512):
    t, d = x.shape
    return pl.pallas_call(
        _norm_kernel,
        out_shape=jax.ShapeDtypeStruct((t, d), BF16),
        grid=(t // tm,),
        in_specs=[pl.BlockSpec((tm, d), lambda i: (i, 0)), pl.BlockSpec((1, d), lambda i: (0, 0))],
        out_specs=pl.BlockSpec((tm, d), lambda i: (i, 0)),
        compiler_params=_cparams(("parallel",)),
        name="mix_norm",
    )(x, gain.reshape(1, d))


B_END = B_OFF + Q_LORA + KV_LORA + ROPE_DIM_B


INPROJ_ROW_SPLIT = 8


def _inproj_kernel(h_ref, *refs, gate_block):
    w_refs, o_ref = refs[:-1], refs[-1]
    j = pl.program_id(1)
    rows = h_ref.shape[0] // INPROJ_ROW_SPLIT
    for q, w_ref in enumerate(w_refs):
        n = w_ref.shape[1]
        w = w_ref[0].astype(BF16)
        for r in range(INPROJ_ROW_SPLIT):
            acc = _dot_nt(h_ref[r * rows:(r + 1) * rows, :], w)
            o_ref[r * rows:(r + 1) * rows, q * n:(q + 1) * n] = jnp.where(
                j >= gate_block, _sigmoid(acc), acc).astype(o_ref.dtype)


def _inproj(h, w_t, layer, *, tm=2048, tn=1024):
    t, d = h.shape
    n_out = w_t.shape[1] + C_OFF - B_END
    assert C_OFF % tn == 0 and n_out % tn == 0
    ns = WEIGHT_SPLIT

    unit = math.gcd(tn, C_OFF - B_END)

    def first_col(j, q):
        k = j * (tn // unit) + q * (tn // ns // unit)
        return unit * jnp.where(j * tn < C_OFF, k, k - (C_OFF - B_END) // unit)

    return pl.pallas_call(
        functools.partial(_inproj_kernel, gate_block=QKV_COLS // tn),
        out_shape=jax.ShapeDtypeStruct((t, n_out), BF16),
        grid=(t // tm, n_out // tn),
        in_specs=[pl.BlockSpec((tm, d), lambda i, j: (i, 0))]
                 + [pl.BlockSpec((pl.Element(1), pl.Element(tn // ns), pl.Element(d)),
                                 functools.partial(
                                     lambda i, j, q: (layer, first_col(j, q), 0), q=q))
                    for q in range(ns)],
        out_specs=pl.BlockSpec((tm, tn), lambda i, j: (i, j)),
        compiler_params=_cparams(("parallel", "arbitrary")),
        name="inproj",
    )(h, *([w_t] * ns))


def _osm(scores, values, carries):
    stats = []
    for s, (m, l, _) in zip(scores, carries):
        m_new = jnp.maximum(m, jnp.max(s, axis=-1, keepdims=True))
        alpha = jnp.exp2(m - m_new)
        p = jnp.exp2(s - m_new)
        stats.append((m_new, alpha * l + jnp.sum(p, axis=-1, keepdims=True), alpha, p))
    return tuple((m_new, l, alpha * acc + _dot(p.astype(BF16), v))
                 for (m_new, l, alpha, p), v, (_, _, acc) in zip(stats, values, carries))


def _causal_keep(t):
    r = lax.broadcasted_iota(jnp.int32, (t, t), 0)
    c = lax.broadcasted_iota(jnp.int32, (t, t), 1)
    return c <= r


def _diff_kernel(lam_ref, gain_ref, rot_ref, cq_ref, sq_ref, ck_ref, sk_ref,
                 q_ref, k_ref, v_ref, o_ref, kr_ref, *, tq, hp, lam_init):
    qi = pl.program_id(2)
    rot = rot_ref[...]
    heads = [slice(h * HEAD_DIM, (h + 1) * HEAD_DIM) for h in range(hp)]

    @pl.when(qi == 0)
    def _():
        for hs in heads:
            kr_ref[:, hs] = _rope(k_ref[:, hs], rot, ck_ref[...], sk_ref[...]).astype(BF16)

    scale2 = LOG2_E / math.sqrt(DIFF_DIM)
    lane = lax.broadcasted_iota(jnp.int32, (tq, HEAD_DIM), 1)
    chains = []
    for hs in heads:
        qf = _rope(q_ref[:, hs], rot, cq_ref[...], sq_ref[...])
        chains.append((jnp.where(lane < DIFF_DIM, qf, 0.0).astype(BF16), hs))
        chains.append((jnp.where(lane >= DIFF_DIM, qf, 0.0).astype(BF16), hs))

    def step(j, carry, masked):
        off = pl.multiple_of(j * tq, tq)
        scores = [_dot_nt(q, kr_ref[pl.ds(off, tq), hs]) * scale2 for q, hs in chains]
        if masked:
            keep = _causal_keep(tq)
            scores = [jnp.where(keep, s, -jnp.inf) for s in scores]
        return _osm(scores, [v_ref[pl.ds(off, tq), hs] for _, hs in chains], carry)

    init = (jnp.full((tq, 1), -jnp.inf, F32), jnp.zeros((tq, 1), F32),
            jnp.zeros((tq, HEAD_DIM), F32))
    carry = lax.fori_loop(0, qi, lambda j, c: step(j, c, False), tuple(init for _ in chains))
    carry = step(qi, carry, True)

    lam_rows = lam_ref[...]
    lam = (jnp.exp(jnp.sum(lam_rows[0:1] * lam_rows[1:2], axis=-1, keepdims=True))
           - jnp.exp(jnp.sum(lam_rows[2:3] * lam_rows[3:4], axis=-1, keepdims=True))
           + lam_init)
    for i, hs in enumerate(heads):
        (_, l1, a1), (_, l2, a2) = carry[2 * i], carry[2 * i + 1]
        out = a1 / l1 - lam * (a2 / l2)
        o_ref[:, hs] = (_rms(out, gain_ref[...]) * (1.0 - lam_init)).astype(o_ref.dtype)


def _diff_attention(proj, lam_rows, gain, tables, *, batch, seq, lam_init, tq=512, hp=4):
    c, s, rot = tables
    nq = seq // tq
    w = hp * HEAD_DIM
    cb = A_OFF // w
    ng = N_HEADS // hp
    return pl.pallas_call(
        functools.partial(_diff_kernel, tq=tq, hp=hp, lam_init=lam_init),
        out_shape=jax.ShapeDtypeStruct((batch * seq, BRANCH_WIDTH), BF16),
        grid=(batch, ng, nq),
        in_specs=[pl.BlockSpec((4, DIFF_DIM), lambda b, h, i: (0, 0)),
                  pl.BlockSpec((1, HEAD_DIM), lambda b, h, i: (0, 0)),
                  pl.BlockSpec((LANES, LANES), lambda b, h, i: (0, 0)),
                  pl.BlockSpec((tq, LANES), lambda b, h, i: (i, 0)),
                  pl.BlockSpec((tq, LANES), lambda b, h, i: (i, 0)),
                  pl.BlockSpec((seq, LANES), lambda b, h, i: (0, 0)),
                  pl.BlockSpec((seq, LANES), lambda b, h, i: (0, 0)),
                  pl.BlockSpec((tq, w), lambda b, h, i: (b * nq + i, cb + h)),
                  pl.BlockSpec((seq, w), lambda b, h, i: (b, cb + ng + h)),
                  pl.BlockSpec((seq, w), lambda b, h, i: (b, cb + 2 * ng + h))],
        out_specs=pl.BlockSpec((tq, w), lambda b, h, i: (b * nq + i, h)),
        scratch_shapes=[pltpu.VMEM((seq, w), BF16)],
        compiler_params=_cparams(("parallel", "parallel", "arbitrary")),
        name="diff_attention",
    )(lam_rows, gain.reshape(1, HEAD_DIM), rot, c, s, c, s, proj, proj, proj)


def _mla_prep_kernel(cq_ref, ckv_ref, kr_ref, dqn_ref, dkvn_ref, wuq_ref, wuk_ref, wuv_ref,
                     rot_ref, c_ref, s_ref, q_out, k_out, v_out):
    rot = rot_ref[...]
    c = c_ref[...]
    s = s_ref[...]
    hq = _rms(cq_ref[...].astype(F32), dqn_ref[...]).astype(BF16)
    hkv = _rms(ckv_ref[...].astype(F32), dkvn_ref[...]).astype(BF16)
    q = _dot(hq, wuq_ref[...])
    kn = _dot(hkv, wuk_ref[...])
    v_out[...] = _dot(hkv, wuv_ref[...]).astype(BF16)
    kr_lane = lax.broadcasted_iota(jnp.int32, kr_ref.shape, 1)
    kr = jnp.where(kr_lane < ROPE_DIM_B, kr_ref[...], jnp.zeros_like(kr_ref[...]))
    k_rope = _rope(kr, rot, c, s).astype(BF16)
    for h in range(N_HEADS):
        lo = h * MLA_QK
        q_out[:, lo:lo + NOPE_DIM] = q[:, lo:lo + NOPE_DIM].astype(BF16)
        q_out[:, lo + NOPE_DIM:lo + MLA_QK] = _rope(
            q[:, lo + NOPE_DIM:lo + MLA_QK].astype(BF16), rot, c, s).astype(BF16)
        k_out[:, lo:lo + NOPE_DIM] = kn[:, h * NOPE_DIM:(h + 1) * NOPE_DIM].astype(BF16)
        k_out[:, lo + NOPE_DIM:lo + MLA_QK] = k_rope


def _mla_prep(proj, dq_norm, dkv_norm, wuq, wuk, wuv, tables, *, seq, tm=512):
    c, s, rot = tables
    t = proj.shape[0]
    nb = seq // tm
    cb = B_OFF // Q_LORA
    full = lambda shape: pl.BlockSpec(shape, lambda i: (0, 0))
    return pl.pallas_call(
        _mla_prep_kernel,
        out_shape=(jax.ShapeDtypeStruct((t, N_HEADS * MLA_QK), BF16),
                   jax.ShapeDtypeStruct((t, N_HEADS * MLA_QK), BF16),
                   jax.ShapeDtypeStruct((t, N_HEADS * HEAD_DIM), BF16)),
        grid=(t // tm,),
        in_specs=[pl.BlockSpec((tm, Q_LORA), lambda i: (i, cb)),
                  pl.BlockSpec((tm, KV_LORA), lambda i: (i, cb + 1)),
                  pl.BlockSpec((tm, LANES), lambda i: (i, (B_OFF + Q_LORA + KV_LORA) // LANES)),
                  full((1, Q_LORA)), full((1, KV_LORA)),
                  full(wuq.shape), full(wuk.shape), full(wuv.shape),
                  full((LANES, LANES)),
                  pl.BlockSpec((tm, LANES), lambda i: (i % nb, 0)),
                  pl.BlockSpec((tm, LANES), lambda i: (i % nb, 0))],
        out_specs=(pl.BlockSpec((tm, N_HEADS * MLA_QK), lambda i: (i, 0)),
                   pl.BlockSpec((tm, N_HEADS * MLA_QK), lambda i: (i, 0)),
                   pl.BlockSpec((tm, N_HEADS * HEAD_DIM), lambda i: (i, 0))),
        compiler_params=_cparams(("parallel",)),
        name="mla_prep",
    )(proj, proj, proj, dq_norm.reshape(1, Q_LORA), dkv_norm.reshape(1, KV_LORA),
      wuq, wuk, wuv, rot, c, s)


def _flash_kernel(q_ref, k_ref, v_ref, o_ref, *, tq, hp, dqk, scale):
    qi = pl.program_id(2)
    qk = [slice(h * dqk, (h + 1) * dqk) for h in range(hp)]
    hv = [slice(h * HEAD_DIM, (h + 1) * HEAD_DIM) for h in range(hp)]
    qs = [q_ref[:, sl] for sl in qk]

    def step(j, carry, masked):
        off = pl.multiple_of(j * tq, tq)
        scores = [_dot_nt(q, k_ref[pl.ds(off, tq), ks]) * (scale * LOG2_E) for q, ks in zip(qs, qk)]
        if masked:
            keep = _causal_keep(tq)
            scores = [jnp.where(keep, s, -jnp.inf) for s in scores]
        return _osm(scores, [v_ref[pl.ds(off, tq), vs] for vs in hv], carry)

    init = (jnp.full((tq, 1), -jnp.inf, F32), jnp.zeros((tq, 1), F32),
            jnp.zeros((tq, HEAD_DIM), F32))
    carry = lax.fori_loop(0, qi, lambda j, c: step(j, c, False), tuple(init for _ in qs))
    carry = step(qi, carry, True)
    for vs, (_, l, acc) in zip(hv, carry):
        o_ref[:, vs] = (acc / l).astype(o_ref.dtype)


def _mla_attention(q, k, v, *, batch, seq, tq=512, hp=4):
    nq = seq // tq
    ng = N_HEADS // hp
    return pl.pallas_call(
        functools.partial(_flash_kernel, tq=tq, hp=hp, dqk=MLA_QK,
                          scale=1.0 / math.sqrt(NOPE_DIM + ROPE_DIM_B)),
        out_shape=jax.ShapeDtypeStruct((batch * seq, BRANCH_WIDTH), BF16),
        grid=(batch, ng, nq),
        in_specs=[pl.BlockSpec((tq, hp * MLA_QK), lambda b, h, i: (b * nq + i, h)),
                  pl.BlockSpec((seq, hp * MLA_QK), lambda b, h, i: (b, h)),
                  pl.BlockSpec((seq, hp * HEAD_DIM), lambda b, h, i: (b, h))],
        out_specs=pl.BlockSpec((tq, hp * HEAD_DIM), lambda b, h, i: (b * nq + i, h)),
        compiler_params=_cparams(("parallel", "parallel", "arbitrary")),
        name="mla_attention",
    )(q, k, v)


DIL_BLOCK = 128


def _dilated_kernel(rot_ref, c_ref, s_ref, *refs, seq):
    in_refs = refs[:9]
    o_ref = refs[9]
    qf, kf, vf, og, lse = refs[10:]
    rot = rot_ref[...]
    c = c_ref[...]
    s = s_ref[...]
    for g in range(N_DIL_GROUPS):
        qf[g] = _rope(in_refs[g][...], rot, c, s)
        kf[g] = _rope(in_refs[3 + g][...], rot, c, s)
        vf[g] = in_refs[6 + g][...].astype(F32)
    scale = 1.0 / math.sqrt(HEAD_DIM)
    blk = DIL_BLOCK

    def rows(start, size, stride):
        return pl.ds(start, size) if stride == 1 else pl.ds(start, size, stride=stride)

    for g, (window, dil) in enumerate(DIL_PAIRS):
        assert window == blk * dil
        span = blk * dil
        n_sub = seq // span
        nk = 2 * blk if n_sub > 1 else blk
        assert n_sub & (n_sub - 1) == 0
        q_rows, k_rows = [], []
        for r in range(dil):
            for cb in range(n_sub):
                q_rows.append(rows(r + cb * span, blk, dil))
                k_rows.append(rows(r + max(cb - 1, 0) * span, nk, dil))
        nb = len(q_rows)
        q = jnp.stack([qf[g, qr, :] for qr in q_rows]).astype(BF16)
        k = jnp.stack([kf[g, kr, :] for kr in k_rows]).astype(BF16)
        v = jnp.stack([vf[g, kr, :] for kr in k_rows]).astype(BF16)
        sc = jnp.einsum("bqd,bkd->bqk", q, k, preferred_element_type=F32) * (scale * LOG2_E)
        first = (lax.broadcasted_iota(jnp.int32, (nb, blk, nk), 0) & (n_sub - 1)) == 0
        dist = (lax.broadcasted_iota(jnp.int32, (nb, blk, nk), 1)
                - lax.broadcasted_iota(jnp.int32, (nb, blk, nk), 2)
                + jnp.where(first, 0, nk - blk))
        sc = jnp.where(dist >= 0, jnp.where(dist <= blk, sc, -jnp.inf), -jnp.inf)
        m = jnp.max(sc, axis=-1, keepdims=True)
        e = jnp.exp2(sc - m)
        den = jnp.sum(e, axis=-1, keepdims=True)
        o = jnp.einsum("bqk,bkd->bqd", e.astype(BF16), v, preferred_element_type=F32) / den
        lg = jnp.broadcast_to(m + jnp.log2(den), (nb, blk, HEAD_DIM))
        for i, qr in enumerate(q_rows):
            og[g, qr, :] = o[i]
            lse[g, qr, :] = lg[i]

    l0, l1, l2 = lse[0], lse[1], lse[2]
    mx = jnp.maximum(jnp.maximum(l0, l1), l2)
    w0, w1, w2 = jnp.exp2(l0 - mx), jnp.exp2(l1 - mx), jnp.exp2(l2 - mx)
    o_ref[...] = ((w0 * og[0] + w1 * og[1] + w2 * og[2]) / (w0 + w1 + w2)).astype(o_ref.dtype)


def _dilated_attention(proj, tables, *, batch, seq):
    c, s, rot = tables
    cb = C_OFF // HEAD_DIM
    nh = N_DIL_GROUPS * N_HEADS

    def col(kind, g):
        return lambda b, h: (b, cb + kind * nh + g * N_HEADS + h)

    in_specs = [pl.BlockSpec((LANES, LANES), lambda b, h: (0, 0)),
                pl.BlockSpec((seq, LANES), lambda b, h: (0, 0)),
                pl.BlockSpec((seq, LANES), lambda b, h: (0, 0))]
    for kind in range(3):
        for g in range(N_DIL_GROUPS):
            in_specs.append(pl.BlockSpec((seq, HEAD_DIM), col(kind, g)))
    return pl.pallas_call(
        functools.partial(_dilated_kernel, seq=seq),
        out_shape=jax.ShapeDtypeStruct((batch * seq, BRANCH_WIDTH), BF16),
        grid=(batch, N_HEADS),
        in_specs=in_specs,
        out_specs=pl.BlockSpec((seq, HEAD_DIM), lambda b, h: (b, h)),
        scratch_shapes=[pltpu.VMEM((N_DIL_GROUPS, seq, HEAD_DIM), F32) for _ in range(5)],
        compiler_params=_cparams(("parallel", "parallel")),
        name="dilated_attention",
    )(rot, c, s, *([proj] * 9))


def _stick_kernel(q_ref, k_ref, v_ref, o_ref, *, tq, hp, scale):
    qi = pl.program_id(2)
    heads = [slice(h * HEAD_DIM, (h + 1) * HEAD_DIM) for h in range(hp)]
    qs = [q_ref[:, hs] for hs in heads]
    r = lax.broadcasted_iota(jnp.int32, (tq, tq), 0)
    c = lax.broadcasted_iota(jnp.int32, (tq, tq), 1)
    later_keys = jnp.where(r > c, 1.0, 0.0).astype(BF16)

    def step(j, carry, diag):
        off = pl.multiple_of(j * tq, tq)
        strict = c < r
        z2 = [_dot_nt(q, k_ref[pl.ds(off, tq), hs]) * (scale * LOG2_E) for q, hs in zip(qs, heads)]
        sp2 = [jnp.maximum(z, 0.0) + jnp.log2(1.0 + jnp.exp2(-jnp.abs(z))) for z in z2]
        log_not = [jnp.where(strict, -sp, 0.0) if diag else -sp for sp in sp2]
        later = []
        for ln, (tail, _) in zip(log_not, carry):
            hi = ln.astype(BF16)
            lo = (ln - hi.astype(F32)).astype(BF16)
            later.append(_dot(hi, later_keys) + _dot(lo, later_keys) + tail)
        a = [jnp.exp2((z - sp) + lt) for z, sp, lt in zip(z2, sp2, later)]
        if diag:
            a = [jnp.where(strict, x, 0.0) for x in a]
        out = []
        for x, ln, hs, (tail, acc) in zip(a, log_not, heads, carry):
            acc = acc + _dot(x.astype(BF16), v_ref[pl.ds(off, tq), hs])
            out.append((tail + jnp.sum(ln, axis=-1, keepdims=True), acc))
        return tuple(out)

    init = (jnp.zeros((tq, 1), F32), jnp.zeros((tq, HEAD_DIM), F32))
    carry = step(qi, tuple(init for _ in heads), True)
    carry = lax.fori_loop(0, qi, lambda t, cr: step(qi - 1 - t, cr, False), carry)
    for hs, (_, acc) in zip(heads, carry):
        o_ref[:, hs] = acc.astype(o_ref.dtype)


def _stick_attention(proj, *, batch, seq, tq=256, hp=4):
    nq = seq // tq
    w = hp * HEAD_DIM
    cb = D_OFF // w
    ng = N_HEADS // hp
    return pl.pallas_call(
        functools.partial(_stick_kernel, tq=tq, hp=hp, scale=1.0 / math.sqrt(HEAD_DIM)),
        out_shape=jax.ShapeDtypeStruct((batch * seq, BRANCH_WIDTH), BF16),
        grid=(batch, ng, nq),
        in_specs=[pl.BlockSpec((tq, w), lambda b, h, i: (b * nq + i, cb + h)),
                  pl.BlockSpec((seq, w), lambda b, h, i: (b, cb + ng + h)),
                  pl.BlockSpec((seq, w), lambda b, h, i: (b, cb + 2 * ng + h))],
        out_specs=pl.BlockSpec((tq, w), lambda b, h, i: (b * nq + i, h)),
        compiler_params=_cparams(("parallel", "parallel", "arbitrary")),
        name="stick_attention",
    )(proj, proj, proj)


def _merge_kernel(ya, yb, yc, yd, g0, g1, g2, g3, wb_ref, o_ref):
    acc = None
    for i, (y, g) in enumerate(((ya, g0), (yb, g1), (yc, g2), (yd, g3))):
        t = g[...].astype(F32) * _dot(y[...], wb_ref[i])
        acc = t if acc is None else acc + t
    o_ref[...] = acc.astype(o_ref.dtype)


def _merge(ys, proj, wb, *, tm=1024, tn=1024):
    t = proj.shape[0]
    d = wb.shape[2]
    nn = d // tn
    g0 = QKV_COLS // tn
    y_spec = pl.BlockSpec((tm, BRANCH_WIDTH), lambda i, j: (i, 0))
    g_specs = [pl.BlockSpec((tm, tn), functools.partial(lambda i, j, b: (i, g0 + b * nn + j), b=b))
               for b in range(N_BRANCH)]
    return pl.pallas_call(
        _merge_kernel,
        out_shape=jax.ShapeDtypeStruct((t, d), BF16),
        grid=(t // tm, nn),
        in_specs=[y_spec] * N_BRANCH + g_specs
                 + [pl.BlockSpec((N_BRANCH, BRANCH_WIDTH, tn), lambda i, j: (0, 0, j))],
        out_specs=pl.BlockSpec((tm, tn), lambda i, j: (i, j)),
        compiler_params=_cparams(("parallel", "arbitrary")),
        name="branch_merge",
    )(*ys, proj, proj, proj, proj, wb)


def _outproj_kernel(x_ref, m_ref, w_ref, o_ref):
    o_ref[...] = x_ref[...] + _dot(m_ref[...], w_ref[...])


def _outproj(x, merged, w, *, tm=1024, tn=1024):
    t, d = x.shape
    return pl.pallas_call(
        _outproj_kernel,
        out_shape=jax.ShapeDtypeStruct((t, d), F32),
        grid=(t // tm, d // tn),
        in_specs=[pl.BlockSpec((tm, tn), lambda i, j: (i, j)),
                  pl.BlockSpec((tm, d), lambda i, j: (i, 0)),
                  pl.BlockSpec((d, tn), lambda i, j: (0, j))],
        out_specs=pl.BlockSpec((tm, tn), lambda i, j: (i, j)),
        compiler_params=_cparams(("parallel", "arbitrary")),
        name="out_proj",
    )(x, merged, w)


def _router_kernel(x_ref, gain_ref, wr_ref, idx_ref, gate_ref):
    h = _rms(x_ref[...], gain_ref[...])
    logits = jnp.dot(h, wr_ref[...], preferred_element_type=F32, precision=lax.Precision.HIGHEST)
    lane = lax.broadcasted_iota(jnp.int32, logits.shape, 1)
    lanef = lane.astype(F32)
    lg = jnp.where(lane < N_EXPERTS, logits, -jnp.inf)
    v1 = jnp.max(lg, axis=-1, keepdims=True)
    i1 = jnp.min(jnp.where(lg == v1, lanef, float(LANES)), axis=-1, keepdims=True)
    lg2 = jnp.where(lanef == i1, -jnp.inf, lg)
    v2 = jnp.max(lg2, axis=-1, keepdims=True)
    i2 = jnp.min(jnp.where(lg2 == v2, lanef, float(LANES)), axis=-1, keepdims=True)
    e2 = jnp.exp(v2 - v1)
    g1 = 1.0 / (1.0 + e2)
    g2 = e2 / (1.0 + e2)
    idx_ref[...] = jnp.where(lane == 0, i1, jnp.where(lane == 1, i2, 0.0)).astype(jnp.int32)
    gate_ref[...] = jnp.where(lane == 0, g1, jnp.where(lane == 1, g2, 0.0))


def _router(x, gain, w_router, *, tm=512):
    t, d = x.shape
    wr = jnp.zeros((d, LANES), F32).at[:, :N_EXPERTS].set(w_router)
    return pl.pallas_call(
        _router_kernel,
        out_shape=(jax.ShapeDtypeStruct((t, LANES), jnp.int32),
                   jax.ShapeDtypeStruct((t, LANES), F32)),
        grid=(t // tm,),
        in_specs=[pl.BlockSpec((tm, d), lambda i: (i, 0)),
                  pl.BlockSpec((1, d), lambda i: (0, 0)),
                  pl.BlockSpec((d, LANES), lambda i: (0, 0))],
        out_specs=(pl.BlockSpec((tm, LANES), lambda i: (i, 0)),
                   pl.BlockSpec((tm, LANES), lambda i: (i, 0))),
        compiler_params=_cparams(("parallel",)),
        name="router",
    )(x, gain.reshape(1, d), wr)


def _start_row_gather(idx_ref, base, n, src_hbm, dst, sem):
    group = 8
    assert n % group == 0

    def start(g, _):
        for j in range(group):
            r = g * group + j
            row = idx_ref[base + r]
            pltpu.make_async_copy(src_hbm.at[pl.ds(row, 1), :], dst.at[pl.ds(r, 1), :],
                                  sem).start(priority=j % 2)
        return 0

    lax.fori_loop(0, n // group, start, 0)


def _wait_row_gather(n, src_hbm, dst, sem):
    pltpu.make_async_copy(src_hbm.at[pl.ds(0, n), :], dst.at[pl.ds(0, n), :], sem).wait()


MOE_CHUNK = 512
MOE_VISIT_CHUNKS = 5
MOE_TF = 512
SWIGLU_VMEM_LIMIT = 62 * 1024 * 1024
DENSE_CHUNK = 512
DENSE_VISIT_CHUNKS = 4
DENSE_TF = 512
WEIGHT_SPLIT = 1


def _moe_kernel(vis_e_ref, vis_ok_ref, vis_row_ref, vis_nch_ref, tail_ref, row_tok_ref,
                x_hbm, gain_ref, *refs, ch, dense):
    del vis_e_ref, vis_ok_ref
    ns = WEIGHT_SPLIT
    wg_refs, wu_refs, wo_refs = refs[:ns], refs[ns:2 * ns], refs[2 * ns:3 * ns]
    y_hbm, h_ref, acc_ref, gsem, osem = refs[3 * ns:]
    v = pl.program_id(0)
    f = pl.program_id(1)
    nv = pl.num_programs(0)
    nf = pl.num_programs(1)
    half = ch // 2
    n_unit = vis_nch_ref[v]
    n_full = n_unit // 2
    odd = n_unit - 2 * n_full
    row0 = vis_row_ref[v]
    active = n_unit > 0
    kq = h_ref.shape[1] // ns

    def pieces(fn):
        def full(c, _):
            fn(pl.multiple_of(c * ch, ch), ch)
            return 0

        lax.fori_loop(0, n_full, full, 0)

        @pl.when(odd == 1)
        def _():
            fn(pl.multiple_of(n_full * ch, half), half)

    def step(h):
        parts = [h[:, q * kq:(q + 1) * kq] for q in range(ns)]
        g = sum(_dot(hq, w[...].astype(BF16)) for hq, w in zip(parts, wg_refs))
        u = sum(_dot(hq, w[...].astype(BF16)) for hq, w in zip(parts, wu_refs))
        a = (g * _sigmoid(g) * u).astype(BF16)
        return jnp.concatenate([_dot(a, w[...].astype(BF16)) for w in wo_refs], axis=1)

    def accumulate(r, n):
        acc_ref[pl.ds(r, n), :] += step(h_ref[pl.ds(r, n), :])

    def out_copy(r, n):
        dst = y_hbm.at[pl.ds(pl.multiple_of(row0 + r, half), n), :]
        return pltpu.make_async_copy(acc_ref.at[pl.ds(r, n), :], dst, osem)

    @pl.when(jnp.logical_and(active, f == 0))
    def _():
        def fetch(r, n):
            dst = acc_ref.at[pl.ds(r, n), :]
            if dense:
                src = x_hbm.at[pl.ds(pl.multiple_of(row0 + r, half), n), :]
                pltpu.make_async_copy(src, dst, gsem).start()
            else:
                _start_row_gather(row_tok_ref, row0 + r, n, x_hbm, dst, gsem)

        def arrived(r, n):
            _wait_row_gather(n, x_hbm, acc_ref.at[pl.ds(r, n), :], gsem)

        def first_sweep(r, n):
            x = acc_ref[pl.ds(r, n), :]
            h = _rms(x, gain_ref[...]).astype(BF16)
            h_ref[pl.ds(r, n), :] = h
            acc_ref[pl.ds(r, n), :] = x + step(h) if dense else step(h)

        pieces(fetch)
        pieces(arrived)
        pieces(first_sweep)

    @pl.when(jnp.logical_and(active, jnp.logical_and(f > 0, f < nf - 1)))
    def _():
        pieces(accumulate)

    @pl.when(jnp.logical_and(active, f == nf - 1))
    def _():
        def last_sweep(r, n):
            accumulate(r, n)
            out_copy(r, n).start()

        pieces(last_sweep)
        pieces(lambda r, n: out_copy(r, n).wait())

    @pl.when(jnp.logical_and(v == nv - 1, f == nf - 1))
    def _():
        first = tail_ref[0]
        n_tail = y_hbm.shape[0] // half - first
        acc_ref[pl.ds(0, half), :] = jnp.zeros((half, acc_ref.shape[1]), acc_ref.dtype)

        def tail_copy(c):
            dst = y_hbm.at[pl.ds(pl.multiple_of((first + c) * half, half), half), :]
            return pltpu.make_async_copy(acc_ref.at[pl.ds(0, half), :], dst, osem)

        def fill(c, _):
            tail_copy(c).start()
            return 0

        def drain(c, _):
            tail_copy(c).wait()
            return 0

        lax.fori_loop(0, n_tail, fill, 0)
        lax.fori_loop(0, n_tail, drain, 0)


def _dense_visits(t, ch, max_chunks):
    span = ch * max_chunks
    half = ch // 2
    row0 = np.arange(0, t, span, dtype=np.int32)
    n_unit = np.minimum(2 * max_chunks, (t - row0) // half).astype(np.int32)
    zeros = np.zeros_like(row0)
    return tuple(jnp.asarray(a) for a in (zeros, zeros + 1, row0, n_unit,
                                          np.array([t // half], np.int32), np.zeros((1,), np.int32)))


def _swiglu_rows(x, gain, tables, w_in, w_out, *, dense, ch, max_chunks, tf, n_visits=None):
    vis_e, vis_ok, vis_row, vis_nch, tail, row_tok = tables
    t, d = x.shape
    ff = w_out.shape[1]
    nf = ff // tf
    p = t if dense else row_tok.shape[0]
    ns = WEIGHT_SPLIT

    def wspec(shape, index):
        def index_map(v, f, e, ok, *_):
            return index(e[v], jnp.where(ok[v] > 0, f, nf - 1))
        return pl.BlockSpec(shape, index_map)

    wg_specs = [wspec((None, d // ns, tf), functools.partial(lambda e, f, q: (e, q, f), q=q))
                for q in range(ns)]
    wu_specs = [wspec((None, d // ns, tf), functools.partial(lambda e, f, q: (e, q, nf + f), q=q))
                for q in range(ns)]
    wo_specs = [wspec((None, tf, d // ns), functools.partial(lambda e, f, q: (e, f, q), q=q))
                for q in range(ns)]
    grid_spec = pltpu.PrefetchScalarGridSpec(
        num_scalar_prefetch=6,
        grid=(vis_e.shape[0] if n_visits is None else n_visits, nf),
        in_specs=[pl.BlockSpec(memory_space=pl.ANY),
                  pl.BlockSpec((1, d), lambda v, f, *_: (0, 0))] + wg_specs + wu_specs + wo_specs,
        out_specs=pl.BlockSpec(memory_space=pl.ANY),
        scratch_shapes=[pltpu.VMEM((max_chunks * ch, d), BF16),
                        pltpu.VMEM((max_chunks * ch, d), F32),
                        pltpu.SemaphoreType.DMA, pltpu.SemaphoreType.DMA],
    )
    return pl.pallas_call(
        functools.partial(_moe_kernel, ch=ch, dense=dense),
        out_shape=jax.ShapeDtypeStruct((p, d), F32),
        grid_spec=grid_spec,
        compiler_params=_cparams(("arbitrary", "arbitrary"), vmem=SWIGLU_VMEM_LIMIT),
        name="dense_ffn" if dense else "moe_experts",
    )(vis_e, vis_ok, vis_row, vis_nch, tail, row_tok, x, gain.reshape(1, d),
      *([w_in] * (2 * ns)), *([w_out] * ns))


def _combine_kernel(pos_ref, x_ref, gate_ref, y_hbm, gain_ref, o_ref, buf, sem, *, tm):
    i = pl.program_id(0)
    n = pl.num_programs(0)
    t = pos_ref.shape[0] // TOP_K
    slot = lax.rem(i, 2)

    def start(step, sl):
        for k in range(TOP_K):
            _start_row_gather(pos_ref, k * t + step * tm, tm, y_hbm, buf.at[sl, k], sem.at[sl, k])

    @pl.when(i == 0)
    def _():
        start(0, 0)

    @pl.when(i + 1 < n)
    def _():
        start(i + 1, 1 - slot)

    for k in range(TOP_K):
        _wait_row_gather(tm, y_hbm, buf.at[slot, k], sem.at[slot, k])
    g = gate_ref[...]
    x = x_ref[...] + (g[:, 0:1] * buf[slot, 0] + g[:, 1:2] * buf[slot, 1])
    if gain_ref is None:
        o_ref[...] = x
    else:
        o_ref[...] = _rms(x, gain_ref[...])


def _moe_combine(x, gate, y_rows, pos, final_gain, *, tm=256):
    t, d = x.shape
    in_specs = [pl.BlockSpec((tm, d), lambda i, ps: (i, 0)),
                pl.BlockSpec((tm, LANES), lambda i, ps: (i, 0)),
                pl.BlockSpec(memory_space=pl.ANY)]
    args = [x, gate, y_rows]
    if final_gain is not None:
        in_specs.append(pl.BlockSpec((1, d), lambda i, ps: (0, 0)))
        args.append(final_gain.reshape(1, d))
        body = functools.partial(_combine_kernel, tm=tm)
    else:
        def body(pos_ref, x_ref, gate_ref, y_hbm, o_ref, *scratch):
            _combine_kernel(pos_ref, x_ref, gate_ref, y_hbm, None, o_ref, *scratch, tm=tm)
    grid_spec = pltpu.PrefetchScalarGridSpec(
        num_scalar_prefetch=1,
        grid=(t // tm,),
        in_specs=in_specs,
        out_specs=pl.BlockSpec((tm, d), lambda i, ps: (i, 0)),
        scratch_shapes=[pltpu.VMEM((2, TOP_K, tm, d), F32), pltpu.SemaphoreType.DMA((2, TOP_K))],
    )
    return pl.pallas_call(
        body,
        out_shape=jax.ShapeDtypeStruct((t, d), F32),
        grid_spec=grid_spec,
        compiler_params=_cparams(("arbitrary",)),
        name="moe_combine",
    )(pos, *args)


def _final_norm_kernel(x_ref, g_ref, o_ref):
    o_ref[...] = _rms(x_ref[...], g_ref[...])


def _final_norm(x, gain, *, tm=512):
    t, d = x.shape
    return pl.pallas_call(
        _final_norm_kernel,
        out_shape=jax.ShapeDtypeStruct((t, d), F32),
        grid=(t // tm,),
        in_specs=[pl.BlockSpec((tm, d), lambda i: (i, 0)), pl.BlockSpec((1, d), lambda i: (0, 0))],
        out_specs=pl.BlockSpec((tm, d), lambda i: (i, 0)),
        compiler_params=_cparams(("parallel",)),
        name="final_norm",
    )(x, gain.reshape(1, d))


def _routing_tables(idx, ch, max_chunks):
    ch, max_chunks = ch // 2, 2 * max_chunks
    t = idx.shape[0]
    e_flat = idx[:, :TOP_K].T.reshape(-1)
    tok = jnp.tile(jnp.arange(t, dtype=jnp.int32), TOP_K)
    experts = jnp.arange(N_EXPERTS, dtype=jnp.int32)
    onehot = (e_flat[:, None] == experts[None, :]).astype(jnp.int32)
    rank = jnp.sum((jnp.cumsum(onehot, axis=0) - onehot) * onehot, axis=1)
    counts = jnp.sum(onehot, axis=0)
    n_chunk = (counts + ch - 1) // ch
    ends = jnp.cumsum(n_chunk * ch)
    starts = ends - n_chunk * ch
    dest = (starts[e_flat] + rank).astype(jnp.int32)
    p = TOP_K * t + N_EXPERTS * ch
    row_tok = jnp.zeros((p,), jnp.int32).at[dest].set(tok)

    n_vis = (n_chunk + max_chunks - 1) // max_chunks
    per_vis = (n_chunk + jnp.maximum(n_vis, 1) - 1) // jnp.maximum(n_vis, 1)
    v_end = jnp.cumsum(n_vis)
    v_start = v_end - n_vis
    nv_max = (p // ch + (max_chunks - 1) * N_EXPERTS) // max_chunks
    slot = jnp.arange(nv_max, dtype=jnp.int32)
    ok = slot < v_end[-1]
    e_of = jnp.minimum(jnp.sum((slot[:, None] >= v_end[None, :]).astype(jnp.int32), axis=1),
                       N_EXPERTS - 1)
    e_of = jnp.where(ok, e_of, e_of[v_end[-1] - 1])
    k = slot - v_start[e_of]
    n_ch = jnp.where(ok, jnp.clip(n_chunk[e_of] - k * per_vis[e_of], 0, per_vis[e_of]), 0)
    row0 = jnp.where(ok, starts[e_of] + k * per_vis[e_of] * ch, 0)
    tail = (ends[-1:] // ch).astype(jnp.int32)
    tables = (e_of.astype(jnp.int32), ok.astype(jnp.int32), row0.astype(jnp.int32),
              n_ch.astype(jnp.int32), tail, row_tok)
    return tables, dest, v_end[-1].astype(jnp.int32)


def _relayout_mla(w_uq, w_ukv):
    dq = NOPE_DIM + ROPE_DIM_B
    wq = w_uq.reshape(Q_LORA, N_HEADS, dq)
    wq = jnp.concatenate([wq, jnp.zeros((Q_LORA, N_HEADS, MLA_QK - dq), w_uq.dtype)], axis=2)
    wkv = w_ukv.reshape(KV_LORA, N_HEADS, NOPE_DIM + HEAD_DIM)
    wk = wkv[:, :, :NOPE_DIM].reshape(KV_LORA, N_HEADS * NOPE_DIM)
    wv = wkv[:, :, NOPE_DIM:].reshape(KV_LORA, N_HEADS * HEAD_DIM)
    return wq.reshape(Q_LORA, N_HEADS * MLA_QK).astype(BF16), wk.astype(BF16), wv.astype(BF16)


def kernel(x, norm_mix, w_in, dq_norm, dkv_norm, w_uq, w_ukv, lam_q1, lam_k1, lam_q2, lam_k2,
           diff_norm, w_branch, w_out, norm_ffn, w_dense_in, w_dense_out, w_router,
           w_moe_in, w_moe_out, norm_final):
    batch, seq, d = x.shape
    depth = w_in.shape[0]
    t = batch * seq
    xt = x.reshape(t, d)

    rot_a = DIFF_DIM // 4
    tab_a = _rope_tables(seq, ((0, rot_a), (DIFF_DIM, rot_a)))
    tab_b = _rope_tables(seq, ((0, ROPE_DIM_B),))
    tab_c = _rope_tables(seq, ((0, HEAD_DIM // 4),))
    w_in_t = jnp.swapaxes(w_in, 1, 2)

    for layer in range(depth):
        proj = _inproj(_rms_norm_bf16(xt, norm_mix[layer]), w_in_t, layer)

        lam_init = 0.8 - 0.6 * math.exp(-0.3 * layer)
        lam_rows = jnp.stack([lam_q1[layer], lam_k1[layer], lam_q2[layer], lam_k2[layer]])
        y_a = _diff_attention(proj, lam_rows, diff_norm[layer], tab_a,
                              batch=batch, seq=seq, lam_init=lam_init)

        wuq, wuk, wuv = _relayout_mla(w_uq[layer], w_ukv[layer])
        q_b, k_b, v_b = _mla_prep(proj, dq_norm[layer], dkv_norm[layer], wuq, wuk, wuv, tab_b, seq=seq)
        y_b = _mla_attention(q_b, k_b, v_b, batch=batch, seq=seq)

        y_c = _dilated_attention(proj, tab_c, batch=batch, seq=seq)
        y_d = _stick_attention(proj, batch=batch, seq=seq)

        merged = _merge((y_a, y_b, y_c, y_d), proj, w_branch[layer].astype(BF16))
        xt = _outproj(xt, merged, w_out[layer].astype(BF16))

        last = layer == depth - 1
        if layer % 2 == 0:
            m = layer // 2
            xt = _swiglu_rows(xt, norm_ffn[layer], _dense_visits(t, DENSE_CHUNK, DENSE_VISIT_CHUNKS),
                              w_dense_in[m:m + 1], w_dense_out[m:m + 1], dense=True,
                              ch=DENSE_CHUNK, max_chunks=DENSE_VISIT_CHUNKS, tf=DENSE_TF)
            if last:
                xt = _final_norm(xt, norm_final)
        else:
            m = layer // 2
            idx, gate = _router(xt, norm_ffn[layer], w_router[m])
            tables, dest, n_vis = _routing_tables(idx, MOE_CHUNK, MOE_VISIT_CHUNKS)
            y_rows = _swiglu_rows(xt, norm_ffn[layer], tables, w_moe_in[m], w_moe_out[m], dense=False,
                                  ch=MOE_CHUNK, max_chunks=MOE_VISIT_CHUNKS, tf=MOE_TF, n_visits=n_vis)
            xt = _moe_combine(xt, gate, y_rows, dest, norm_final if last else None)
    return xt.reshape(batch, seq, d)
```
